```python
import jax
import jax.numpy as jnp
from jax import lax
import numpy as np

D_MODEL = 1024
BATCH = 8
SEQ = 2048
DEPTH = 2
DEC_BATCH = 32
DEC_SEQ = 8
PAST_LEN = 8192
PAGE_SIZE = 128

N_MIXERS = 2
N_ATT_LAYERS = (DEPTH + N_MIXERS - 1) // N_MIXERS
N_REC_LAYERS = DEPTH // N_MIXERS
N_HEADS = 16
HEAD_DIM = D_MODEL // N_HEADS
MOBA_BLOCK = 256
MOBA_TOPK = 3
PROMPT_Q_CHUNK = 8
SAMPLE_Q_CHUNK = 1
D_RNN = D_MODEL
LRU_BLOCK_WIDTH = 256
N_LRU_BLOCKS = D_RNN // LRU_BLOCK_WIDTH
CONV_W = 4
LRU_C = 8.0
N_EXPERTS = 32
TOP_K = 4
D_EXPERT = D_MODEL
SWIGLU_LIMIT = 7.0
SWIGLU_ALPHA = 1.702
EXPERT_GROUP_ROWS = 128
DEEPNORM_ALPHA = (2 * DEPTH) ** 0.25
DEEPNORM_BETA = (8 * DEPTH) ** -0.25
LN_EPS = 1e-5
NEG_INF = -1e30

kernel_name = 'moba_rglru_hybrid_decode_step'


def layer_norm(x, g, b):
    xf = x.astype(jnp.float32)
    mu = jnp.mean(xf, axis=-1, keepdims=True)
    xc = xf - mu
    var = jnp.mean(xc * xc, axis=-1, keepdims=True)
    return (xc * lax.rsqrt(var + LN_EPS) * g.astype(jnp.float32) + b.astype(jnp.float32)).astype(x.dtype)


def alibi_slopes(n_heads):
    return 2.0 ** (-8.0 * jnp.arange(1, n_heads + 1, dtype=jnp.float32) / n_heads)


def qkv_heads(x, w_qkv):
    b, t, _ = x.shape
    qkv = (x @ w_qkv).reshape(b, t, 3, N_HEADS, HEAD_DIM)
    return qkv[:, :, 0], qkv[:, :, 1], qkv[:, :, 2]


def select_blocks(qt, kmean, qpos, n_sel):
    g = jnp.einsum('bhqd,bhnd->bhqn', qt.astype(jnp.float32), kmean)
    b, h, q, nb = g.shape
    own = qpos // MOBA_BLOCK
    past = jnp.arange(nb)[None, :] < own[:, None]
    g = jnp.where(past, g, NEG_INF)
    _, top = lax.top_k(g, n_sel)
    own_b = jnp.broadcast_to(own[:, None], (b, h, q, 1)).astype(jnp.int32)
    blk = jnp.concatenate([top.astype(jnp.int32), own_b], axis=-1)
    valid_sel = jnp.broadcast_to(jnp.arange(n_sel)[None, :] < own[:, None], (b, h, q, n_sel))
    valid = jnp.concatenate([valid_sel, jnp.ones((b, h, q, 1), dtype=bool)], axis=-1)
    return blk, valid


def moba_core(q, qpos, blk, valid, k_sel, v_sel):
    s = jnp.einsum('bhqd,bhqnpd->bhqnp', q, k_sel).astype(jnp.float32) * (HEAD_DIM ** -0.5)
    kpos = blk[..., None] * MOBA_BLOCK + jnp.arange(MOBA_BLOCK, dtype=jnp.int32)
    dist = qpos[None, None, :, None, None] - kpos
    slopes = alibi_slopes(N_HEADS)[None, :, None, None, None]
    s = s - slopes * dist.astype(jnp.float32)
    s = jnp.where(valid[..., None] & (dist >= 0), s, NEG_INF)
    shp = s.shape
    p = jax.nn.softmax(s.reshape(shp[0], shp[1], shp[2], -1), axis=-1).reshape(shp)
    return jnp.einsum('bhqnp,bhqnpd->bhqd', p.astype(v_sel.dtype), v_sel)


def moba_attend(qt, qpos, blk, valid, gather, q_chunk):
    b, h, t, dh = qt.shape
    nc = t // q_chunk

    def split(a):
        return jnp.moveaxis(a.reshape(b, h, nc, q_chunk, *a.shape[3:]), 2, 0)

    def body(args):
        q_c, pos_c, blk_c, valid_c = args
        k_sel, v_sel = gather(blk_c)
        return moba_core(q_c, pos_c, blk_c, valid_c, k_sel, v_sel)

    o = lax.map(body, (split(qt), qpos.reshape(nc, q_chunk), split(blk), split(valid)))
    return jnp.moveaxis(o, 0, 2).reshape(b, h, t, dh)


def moba_prompt(x, w_qkv, w_o):
    b, t, _ = x.shape
    q, k, v = qkv_heads(x, w_qkv)
    nb = -(-t // MOBA_BLOCK)
    pad = ((0, 0), (0, nb * MOBA_BLOCK - t), (0, 0), (0, 0))

    def blocks(a):
        return jnp.pad(a, pad).reshape(b, nb, MOBA_BLOCK, N_HEADS, HEAD_DIM).transpose(0, 3, 1, 2, 4)

    kb, vb = blocks(k), blocks(v)
    kmean = jnp.mean(kb.astype(jnp.float32), axis=3)
    qt = q.transpose(0, 2, 1, 3)
    qpos = jnp.arange(t, dtype=jnp.int32)
    blk, valid = select_blocks(qt, kmean, qpos, min(MOBA_TOPK, nb))
    bi = jnp.arange(b)[:, None, None, None]
    hi = jnp.arange(N_HEADS)[None, :, None, None]

    def gather(blk_c):
        return kb[bi, hi, blk_c], vb[bi, hi, blk_c]

    o = moba_attend(qt, qpos, blk, valid, gather, PROMPT_Q_CHUNK)
    y = o.transpose(0, 2, 1, 3).reshape(b, t, D_MODEL) @ w_o
    return y, k, v


def moba_sample(x, k_pool, v_pool, page_table, w_qkv, w_o):
    b, t, _ = x.shape
    n_pages = page_table.shape[1]
    past_len = n_pages * PAGE_SIZE
    ppb = MOBA_BLOCK // PAGE_SIZE
    q, k, v = qkv_heads(x, w_qkv)
    nb = -(-(past_len + t) // MOBA_BLOCK)
    n_new = nb * ppb - n_pages
    pad = ((0, 0), (0, n_new * PAGE_SIZE - t), (0, 0), (0, 0))
    k_new = jnp.pad(k, pad).reshape(b, n_new, PAGE_SIZE, N_HEADS, HEAD_DIM)
    v_new = jnp.pad(v, pad).reshape(b, n_new, PAGE_SIZE, N_HEADS, HEAD_DIM)
    page_sum = jnp.concatenate([
        jnp.sum(k_pool[page_table].astype(jnp.float32), axis=2),
        jnp.sum(k_new.astype(jnp.float32), axis=2)], axis=1)
    kmean = (page_sum.reshape(b, nb, ppb, N_HEADS, HEAD_DIM).sum(axis=2) / MOBA_BLOCK).transpose(0, 2, 1, 3)
    qt = q.transpose(0, 2, 1, 3)
    qpos = past_len + jnp.arange(t, dtype=jnp.int32)
    blk, valid = select_blocks(qt, kmean, qpos, min(MOBA_TOPK, nb))
    bi5 = jnp.arange(b)[:, None, None, None, None]
    bi6 = bi5[..., None]
    hi6 = jnp.arange(N_HEADS)[None, :, None, None, None, None]
    rows = jnp.arange(PAGE_SIZE)

    def gather(blk_c):
        lp = blk_c[..., None] * ppb + jnp.arange(ppb)
        is_past = (lp < n_pages)[..., None, None]
        phys = page_table[bi5, jnp.clip(lp, 0, n_pages - 1)][..., None]
        new_idx = jnp.clip(lp - n_pages, 0, n_new - 1)[..., None]

        def pick(pool, new):
            sel = jnp.where(is_past, pool[phys, rows, hi6], new[bi6, new_idx, rows, hi6])
            return sel.reshape(*blk_c.shape, MOBA_BLOCK, HEAD_DIM)

        return pick(k_pool, k_new), pick(v_pool, v_new)

    o = moba_attend(qt, qpos, blk, valid, gather, SAMPLE_Q_CHUNK)
    y = o.transpose(0, 2, 1, 3).reshape(b, t, D_MODEL) @ w_o
    return y, k, v


def block_diag(x, w, bias):
    lead = x.shape[:-1]
    xb = x.reshape(*lead, N_LRU_BLOCKS, LRU_BLOCK_WIDTH)
    y = jnp.einsum('...hi,hij->...hj', xb, w) + bias
    return y.reshape(*lead, D_RNN)


def rglru_block(x, conv_state, h0, w_in, b_in, conv_w, conv_b, w_gate_a, b_gate_a,
                w_gate_x, b_gate_x, lru_lambda, w_out, b_out):
    b, t, _ = x.shape
    u = x @ w_in + b_in
    y_branch = jax.nn.gelu(u[..., :D_RNN], approximate=True)
    x_branch = u[..., D_RNN:]
    xc_in = jnp.concatenate([conv_state.astype(x_branch.dtype), x_branch], axis=1)
    xc = conv_b + sum(xc_in[:, j:j + t] * conv_w[j] for j in range(CONV_W))
    xf = xc.astype(jnp.float32)
    r = jax.nn.sigmoid(block_diag(xf, w_gate_a.astype(jnp.float32), b_gate_a.astype(jnp.float32)))
    i = jax.nn.sigmoid(block_diag(xf, w_gate_x.astype(jnp.float32), b_gate_x.astype(jnp.float32)))
    log_a = LRU_C * r * jax.nn.log_sigmoid(lru_lambda.astype(jnp.float32))
    a = jnp.exp(log_a)
    inp = jnp.sqrt(-jnp.expm1(2.0 * log_a)) * (i * xf)

    def step(h, ab):
        a_t, x_t = ab
        h = a_t * h + x_t
        return h, h

    h_last, hs = lax.scan(step, h0.astype(jnp.float32), (jnp.swapaxes(a, 0, 1), jnp.swapaxes(inp, 0, 1)))
    hs = jnp.swapaxes(hs, 0, 1).astype(x.dtype)
    y = (hs * y_branch) @ w_out + b_out
    return y, xc_in[:, t:], h_last


def moe_ffn(x, w_router, b_router, w_up, b_up, w_down, b_down):
    b, t, d = x.shape
    n = b * t
    rg = EXPERT_GROUP_ROWS
    xf = x.reshape(n, d)
    logits = xf.astype(jnp.float32) @ w_router.astype(jnp.float32) + b_router.astype(jnp.float32)
    top_val, top_e = lax.top_k(logits, TOP_K)
    gates = jax.nn.softmax(top_val, axis=-1)
    n_slots = n * TOP_K
    flat_e = top_e.reshape(-1)
    flat_tok = jnp.arange(n_slots, dtype=jnp.int32) // TOP_K
    order = jnp.argsort(flat_e)
    sorted_e = flat_e[order]
    counts = jnp.bincount(flat_e, length=N_EXPERTS)
    start = jnp.cumsum(counts) - counts
    padded = (counts + rg - 1) // rg * rg
    pend = jnp.cumsum(padded)
    pstart = pend - padded
    dest = pstart[sorted_e] + jnp.arange(n_slots, dtype=jnp.int32) - start[sorted_e]
    n_groups = -(-n_slots // rg) + N_EXPERTS
    n_rows = n_groups * rg
    tok_buf = jnp.full((n_rows,), n, jnp.int32).at[dest].set(flat_tok[order])
    gate_buf = jnp.zeros((n_rows,), jnp.float32).at[dest].set(gates.reshape(-1)[order])
    group_e = jnp.minimum(jnp.searchsorted(pend, jnp.arange(n_groups, dtype=jnp.int32) * rg, side='right'),
                          N_EXPERTS - 1)
    x_pad = jnp.concatenate([xf, jnp.zeros((1, d), xf.dtype)], axis=0)
    xg = x_pad[tok_buf].reshape(n_groups, rg, d)

    def expert_group(args):
        xb, e = args
        gu = xb @ w_up[e] + b_up[e]
        gate = jnp.minimum(gu[:, :D_EXPERT], SWIGLU_LIMIT)
        up = jnp.clip(gu[:, D_EXPERT:], -SWIGLU_LIMIT, SWIGLU_LIMIT)
        glu = gate * jax.nn.sigmoid(SWIGLU_ALPHA * gate)
        return ((up + 1.0) * glu) @ w_down[e] + b_down[e]

    yg = lax.map(expert_group, (xg, group_e))
    y = jnp.zeros((n + 1, d), jnp.float32).at[tok_buf].add(yg.reshape(n_rows, d).astype(jnp.float32) * gate_buf[:, None])
    return y[:n].reshape(b, t, d).astype(x.dtype)


def setup_inputs(seed: int = 0) -> dict:
    key = jax.random.key(seed)
    ks = jax.random.split(key, 32)
    f32 = jnp.float32

    def nrm(k, shape, scale):
        return scale * jax.random.normal(k, shape, f32)

    n_pages = PAST_LEN // PAGE_SIZE
    n_phys = (DEC_BATCH * n_pages * 5) // 4
    page_table = jax.random.permutation(ks[0], n_phys)[: DEC_BATCH * n_pages].reshape(DEC_BATCH, n_pages).astype(jnp.int32)
    qkv_scale = jnp.concatenate([jnp.ones((2 * D_MODEL,), f32), jnp.full((D_MODEL,), DEEPNORM_BETA, f32)])
    lam_p = jax.random.uniform(ks[17], (N_REC_LAYERS, D_RNN), f32, 0.9, 0.999) ** (1.0 / LRU_C)
    lru_lambda = jnp.log(lam_p) - jnp.log1p(-lam_p)
    kv_shape = (N_ATT_LAYERS, n_phys, PAGE_SIZE, N_HEADS, HEAD_DIM)
    return {
        'x_prompt': nrm(ks[1], (BATCH, SEQ, D_MODEL), 1.0),
        'x_sample': nrm(ks[2], (DEC_BATCH, DEC_SEQ, D_MODEL), 1.0),
        'cache_k': nrm(ks[3], kv_shape, 1.0),
        'cache_v': nrm(ks[4], kv_shape, DEEPNORM_BETA),
        'state_conv': nrm(ks[5], (N_REC_LAYERS, DEC_BATCH, CONV_W - 1, D_RNN), 1.0),
        'state_h': nrm(ks[6], (N_REC_LAYERS, DEC_BATCH, D_RNN), 0.5),
        'page_table': page_table,
        'ln_g': 1.0 + nrm(ks[7], (DEPTH, 2, D_MODEL), 0.01),
        'ln_b': nrm(ks[8], (DEPTH, 2, D_MODEL), 0.01),
        'w_qkv': nrm(ks[9], (N_ATT_LAYERS, D_MODEL, 3 * D_MODEL), D_MODEL ** -0.5) * qkv_scale,
        'w_o': nrm(ks[10], (N_ATT_LAYERS, D_MODEL, D_MODEL), DEEPNORM_BETA * D_MODEL ** -0.5),
        'w_in': nrm(ks[11], (N_REC_LAYERS, D_MODEL, 2 * D_RNN), D_MODEL ** -0.5),
        'b_in': nrm(ks[12], (N_REC_LAYERS, 2 * D_RNN), 0.01),
        'conv_w': nrm(ks[13], (N_REC_LAYERS, CONV_W, D_RNN), CONV_W ** -0.5),
        'conv_b': nrm(ks[14], (N_REC_LAYERS, D_RNN), 0.01),
        'w_gate_a': nrm(ks[15], (N_REC_LAYERS, N_LRU_BLOCKS, LRU_BLOCK_WIDTH, LRU_BLOCK_WIDTH), LRU_BLOCK_WIDTH ** -0.5),
        'b_gate_a': nrm(ks[16], (N_REC_LAYERS, N_LRU_BLOCKS, LRU_BLOCK_WIDTH), 0.01),
        'w_gate_x': nrm(ks[18], (N_REC_LAYERS, N_LRU_BLOCKS, LRU_BLOCK_WIDTH, LRU_BLOCK_WIDTH), LRU_BLOCK_WIDTH ** -0.5),
        'b_gate_x': nrm(ks[19], (N_REC_LAYERS, N_LRU_BLOCKS, LRU_BLOCK_WIDTH), 0.01),
        'lru_lambda': lru_lambda,
        'w_out': nrm(ks[20], (N_REC_LAYERS, D_RNN, D_MODEL), DEEPNORM_BETA * D_RNN ** -0.5),
        'b_out': nrm(ks[21], (N_REC_LAYERS, D_MODEL), 0.01),
        'w_router': nrm(ks[22], (DEPTH, D_MODEL, N_EXPERTS), D_MODEL ** -0.5),
        'b_router': nrm(ks[23], (DEPTH, N_EXPERTS), 0.01),
        'w_up': nrm(ks[24], (DEPTH, N_EXPERTS, D_MODEL, 2 * D_EXPERT), D_MODEL ** -0.5),
        'b_up': nrm(ks[25], (DEPTH, N_EXPERTS, 2 * D_EXPERT), 0.01),
        'w_down': nrm(ks[26], (DEPTH, N_EXPERTS, D_EXPERT, D_MODEL), DEEPNORM_BETA * D_EXPERT ** -0.5),
        'b_down': nrm(ks[27], (DEPTH, N_EXPERTS, D_MODEL), 0.01),
    }


def reference(x_prompt, x_sample, cache_k, cache_v, state_conv, state_h, page_table,
              ln_g, ln_b, w_qkv, w_o, w_in, b_in, conv_w, conv_b, w_gate_a, b_gate_a,
              w_gate_x, b_gate_x, lru_lambda, w_out, b_out,
              w_router, b_router, w_up, b_up, w_down, b_down):
    xp, xs = x_prompt, x_sample
    kp_l, vp_l, ks_l, vs_l = [], [], [], []
    cp_l, hp_l, cs_l, hs_l = [], [], [], []
    for layer in range(DEPTH):
        j = layer // N_MIXERS
        if layer % N_MIXERS == 0:
            mp, kp, vp = moba_prompt(xp, w_qkv[j], w_o[j])
            ms, kn, vn = moba_sample(xs, cache_k[j], cache_v[j], page_table, w_qkv[j], w_o[j])
            kp_l.append(kp)
            vp_l.append(vp)
            ks_l.append(kn)
            vs_l.append(vn)
        else:
            rec = (w_in[j], b_in[j], conv_w[j], conv_b[j], w_gate_a[j], b_gate_a[j],
                   w_gate_x[j], b_gate_x[j], lru_lambda[j], w_out[j], b_out[j])
            bp = xp.shape[0]
            conv0 = jnp.zeros((bp, CONV_W - 1, D_RNN), xp.dtype)
            h0 = jnp.zeros((bp, D_RNN), jnp.float32)
            mp, cp, hp = rglru_block(xp, conv0, h0, *rec)
            ms, cn, hn = rglru_block(xs, state_conv[j], state_h[j], *rec)
            cp_l.append(cp)
            hp_l.append(hp)
            cs_l.append(cn)
            hs_l.append(hn)
        xp = layer_norm(DEEPNORM_ALPHA * xp + mp, ln_g[layer, 0], ln_b[layer, 0])
        xs = layer_norm(DEEPNORM_ALPHA * xs + ms, ln_g[layer, 0], ln_b[layer, 0])
        moe = (w_router[layer], b_router[layer], w_up[layer], b_up[layer], w_down[layer], b_down[layer])
        xp = layer_norm(DEEPNORM_ALPHA * xp + moe_ffn(xp, *moe), ln_g[layer, 1], ln_b[layer, 1])
        xs = layer_norm(DEEPNORM_ALPHA * xs + moe_ffn(xs, *moe), ln_g[layer, 1], ln_b[layer, 1])
    return (xp, xs, jnp.stack(kp_l), jnp.stack(vp_l), jnp.stack(cp_l), jnp.stack(hp_l),
            jnp.stack(ks_l), jnp.stack(vs_l), jnp.stack(cs_l), jnp.stack(hs_l))
```

```python
import functools

import jax
import jax.numpy as jnp
from jax import lax
from jax.experimental import pallas as pl
from jax.experimental.pallas import tpu as pltpu

F32 = jnp.float32
BF16 = jnp.bfloat16

D_MODEL = 1024
N_HEADS = 16
HEAD_DIM = D_MODEL // N_HEADS
MOBA_BLOCK = 256
MOBA_TOPK = 3
PAGE_SIZE = 128
PAGES_PER_BLOCK = MOBA_BLOCK // PAGE_SIZE
D_RNN = D_MODEL
LRU_BLOCK_WIDTH = 256
N_LRU_BLOCKS = D_RNN // LRU_BLOCK_WIDTH
CONV_W = 4
LRU_C = 8.0
N_EXPERTS = 32
TOP_K = 4
D_EXPERT = D_MODEL
SWIGLU_LIMIT = 7.0
SWIGLU_ALPHA = 1.702
DEPTH = 2
DEEPNORM_ALPHA = (2 * DEPTH) ** 0.25
LN_EPS = 1e-5
NEG_INF = -1e30

V7X_VMEM_LIMIT_BYTES = 56 * 1024 * 1024
SUBLANES = 8
LANES = 128
CONV_HALO = SUBLANES
EXPERT_TILE_ROWS = 512
ROUTER_TILE = 256


def _params(*sem):
    return pltpu.CompilerParams(dimension_semantics=sem, vmem_limit_bytes=V7X_VMEM_LIMIT_BYTES)


def _row_tile(m, pref):
    t = min(pref, m)
    while m % t or t % SUBLANES:
        t -= SUBLANES
    return t


def _div_pow2(x, n):
    assert n & (n - 1) == 0, n
    return lax.shift_right_logical(x, n.bit_length() - 1)


def _layer_norm(y, g, b):
    mu = jnp.mean(y, axis=-1, keepdims=True)
    yc = y - mu
    var = jnp.mean(yc * yc, axis=-1, keepdims=True)
    return yc * lax.rsqrt(var + LN_EPS) * g + b


def _qkv_kernel(x_ref, w_ref, q_ref, k_ref, v_ref):
    x = x_ref[...].astype(BF16)
    for c, o_ref in enumerate((q_ref, k_ref, v_ref)):
        o_ref[...] = jnp.dot(x, w_ref[:, c * D_MODEL:(c + 1) * D_MODEL], preferred_element_type=F32)


def _qkv(x, w_bf):
    m = x.shape[0]
    tm = _row_tile(m, 512)
    out = jax.ShapeDtypeStruct((m, D_MODEL), F32)
    row = pl.BlockSpec((tm, D_MODEL), lambda i: (i, 0))
    return pl.pallas_call(
        _qkv_kernel,
        out_shape=(out, out, out),
        grid=(m // tm,),
        in_specs=[row, pl.BlockSpec((D_MODEL, 3 * D_MODEL), lambda i: (0, 0))],
        out_specs=(row, row, row),
        compiler_params=_params("arbitrary"),
        name="qkv_proj",
    )(x, w_bf)


def _linear_kernel(x_ref, w_ref, b_ref, o_ref):
    x = x_ref[...].astype(BF16)
    o_ref[...] = jnp.dot(x, w_ref[...], preferred_element_type=F32) + b_ref[...]


def _linear(x, w_bf, b):
    m, k = x.shape
    n = w_bf.shape[1]
    tm = _row_tile(m, 512)
    tn = min(n, 1024)
    return pl.pallas_call(
        _linear_kernel,
        out_shape=jax.ShapeDtypeStruct((m, n), F32),
        grid=(m // tm, n // tn),
        in_specs=[pl.BlockSpec((tm, k), lambda i, j: (i, 0)),
                  pl.BlockSpec((k, tn), lambda i, j: (0, j)),
                  pl.BlockSpec((1, tn), lambda i, j: (0, j))],
        out_specs=pl.BlockSpec((tm, tn), lambda i, j: (i, j)),
        compiler_params=_params("arbitrary", "arbitrary"),
        name="linear",
    )(x, w_bf, b.reshape(1, n))


def _linear_res_ln_kernel(a_ref, w_ref, b_ref, res_ref, g_ref, beta_ref, o_ref):
    m = jnp.dot(a_ref[...].astype(BF16), w_ref[...], preferred_element_type=F32) + b_ref[...]
    o_ref[...] = _layer_norm(DEEPNORM_ALPHA * res_ref[...] + m, g_ref[...], beta_ref[...])


def _linear_res_ln(a, w_bf, b, res, g, beta):
    m, k = a.shape
    n = w_bf.shape[1]
    tm = _row_tile(m, 512)
    row_in = pl.BlockSpec((tm, k), lambda i: (i, 0))
    row = pl.BlockSpec((tm, n), lambda i: (i, 0))
    vec = pl.BlockSpec((1, n), lambda i: (0, 0))
    return pl.pallas_call(
        _linear_res_ln_kernel,
        out_shape=jax.ShapeDtypeStruct((m, n), F32),
        grid=(m // tm,),
        in_specs=[row_in, pl.BlockSpec((k, n), lambda i: (0, 0)), vec, row, vec, vec],
        out_specs=row,
        compiler_params=_params("arbitrary"),
        name="linear_res_ln",
    )(a, w_bf, b.reshape(1, n), res, g.reshape(1, n), beta.reshape(1, n))


def _res_ln_kernel(x_ref, y_ref, g_ref, beta_ref, o_ref):
    o_ref[...] = _layer_norm(DEEPNORM_ALPHA * x_ref[...] + y_ref[...], g_ref[...], beta_ref[...])


def _res_ln(x, y, g, beta):
    m, n = x.shape
    tm = _row_tile(m, 1024)
    row = pl.BlockSpec((tm, n), lambda i: (i, 0))
    vec = pl.BlockSpec((1, n), lambda i: (0, 0))
    return pl.pallas_call(
        _res_ln_kernel,
        out_shape=jax.ShapeDtypeStruct((m, n), F32),
        grid=(m // tm,),
        in_specs=[row, row, vec, vec],
        out_specs=row,
        compiler_params=_params("arbitrary"),
        name="res_ln",
    )(x, y, g.reshape(1, n), beta.reshape(1, n))


def _slot_rank(gm, n_blocks, axis):
    blk = lax.broadcasted_iota(jnp.int32, gm.shape, axis)
    rank = jnp.zeros(gm.shape, jnp.int32)
    for m in range(n_blocks):
        g_m = jnp.sum(jnp.where(blk == m, gm, 0.0), axis=axis, keepdims=True)
        beats = (g_m > gm) | ((g_m == gm) & (m < blk))
        rank = rank + jnp.where(beats, 1, 0)
    return rank


def _moba_prompt_kernel(slope_ref, q_ref, k_ref, v_ref, o_ref, *, t):
    nb = t // MOBA_BLOCK
    n_sel = min(MOBA_TOPK, nb)
    bs = MOBA_BLOCK
    hp = pl.program_id(1)
    heads_per_step = LANES // HEAD_DIM

    r_io = lax.broadcasted_iota(jnp.int32, (bs, t), 0)
    c_io = lax.broadcasted_iota(jnp.int32, (bs, t), 1)
    rc = (r_io - c_io).astype(F32)
    causal_bias = jnp.where(lax.broadcasted_iota(jnp.int32, (bs, bs), 0)
                            >= lax.broadcasted_iota(jnp.int32, (bs, bs), 1), 0.0, NEG_INF)

    blk = lax.broadcasted_iota(jnp.int32, (nb, t), 0)
    own = _div_pow2(lax.broadcasted_iota(jnp.int32, (nb, t), 1), MOBA_BLOCK)
    pad_rows = jnp.zeros((LANES - 2 * nb, t), F32)

    outs = []
    for j in range(heads_per_step):
        slope = slope_ref[hp * heads_per_step + j]
        lo, hi = j * HEAD_DIM, (j + 1) * HEAD_DIM
        qh = q_ref[:, lo:hi]
        kh = k_ref[:, lo:hi]
        vh = v_ref[:, lo:hi]

        kmean = jnp.sum(kh.reshape(nb, bs, HEAD_DIM), axis=1) * (1.0 / bs)
        g_t = lax.dot_general(kmean.astype(BF16), qh.astype(BF16), (((1,), (1,)), ((), ())),
                              preferred_element_type=F32)
        gm = jnp.where(blk < own, g_t, NEG_INF)
        rank = _slot_rank(gm, nb, 0)
        chosen = (rank < n_sel) & (rank < own)
        sel_bias_t = jnp.where(chosen, 0.0, NEG_INF)
        own_mult_t = jnp.where(chosen, 2.0, 1.0)
        sel = jnp.transpose(jnp.concatenate([sel_bias_t, own_mult_t, pad_rows], axis=0))

        qs = (qh * (HEAD_DIM ** -0.5)).astype(BF16)
        kb = kh.astype(BF16)
        vb = vh.astype(BF16)
        o_blocks = []
        for i in range(nb):
            w = (i + 1) * bs
            rows = slice(i * bs, (i + 1) * bs)
            s = lax.dot_general(qs[rows], kb[:w], (((1,), (1,)), ((), ())), preferred_element_type=F32)
            s = s - slope * (rc[:, :w] + float(i * bs))
            sel_i = sel[rows]
            bias = [jnp.broadcast_to(sel_i[:, jb:jb + 1], (bs, bs)) for jb in range(i)] + [causal_bias]
            s = s + (jnp.concatenate(bias, axis=1) if i else causal_bias)
            m = jnp.max(s, axis=1, keepdims=True)
            p = jnp.exp(s - m)
            p_own = p[:, i * bs:] * sel_i[:, nb + i:nb + i + 1]
            p = jnp.concatenate([p[:, :i * bs], p_own], axis=1) if i else p_own
            p = p / jnp.sum(p, axis=1, keepdims=True)
            o_blocks.append(jnp.dot(p.astype(BF16), vb[:w], preferred_element_type=F32))
        outs.append(jnp.concatenate(o_blocks, axis=0))
    o_ref[...] = jnp.concatenate(outs, axis=1)


def _moba_prompt(q, k, v, slopes, b, t):
    blk = pl.BlockSpec((t, LANES), lambda bi, hp: (bi, hp))
    return pl.pallas_call(
        functools.partial(_moba_prompt_kernel, t=t),
        out_shape=jax.ShapeDtypeStruct((b * t, D_MODEL), F32),
        grid=(b, D_MODEL // LANES),
        in_specs=[pl.BlockSpec(memory_space=pltpu.SMEM), blk, blk, blk],
        out_specs=blk,
        compiler_params=_params("arbitrary", "arbitrary"),
        name="moba_prompt",
    )(slopes, q, k, v)


def _moba_sample_kernel(pt_ref, slope_ref, q_ref, kn_ref, vn_ref, ka_ref, kb_ref, va_ref, vb_ref,
                        o_ref, qbd_ref, kmean_ref, s_ref, pown_ref, l_ref, acc_ref,
                        *, t, nbp):
    del pt_ref
    step = pl.program_id(1)
    bs = MOBA_BLOCK
    hq = N_HEADS * t
    past_len = nbp * bs
    own_blk = past_len // bs
    nb = nbp + 1
    n_sel = min(MOBA_TOPK, nb)

    row = lax.broadcasted_iota(jnp.int32, (hq, bs), 0)
    lane = lax.broadcasted_iota(jnp.int32, (hq, bs), 1)
    assert t & (t - 1) == 0, t
    qpos = past_len + (row & (t - 1))
    slope = slope_ref[...]

    def head_diag(x):
        r = _div_pow2(lax.broadcasted_iota(jnp.int32, x.shape, 0), t)
        c = _div_pow2(lax.broadcasted_iota(jnp.int32, x.shape, 1), HEAD_DIM)
        return jnp.where(r == c, x, 0.0)

    @pl.when(step == 0)
    def _():
        q_rep = jnp.concatenate([q_ref[...]] * N_HEADS, axis=0)
        qbd_ref[...] = head_diag(q_rep)
        kmean_ref[...] = jnp.zeros_like(kmean_ref)
        acc_ref[...] = jnp.zeros_like(acc_ref)

    @pl.when(step < nbp)
    def _():
        kblk = jnp.concatenate([ka_ref[...], kb_ref[...]], axis=0)
        kmean_ref[pl.ds(step, 1), :] = jnp.sum(kblk, axis=0, keepdims=True) * (1.0 / bs)
        qbd = (qbd_ref[...] * (HEAD_DIM ** -0.5)).astype(BF16)
        s = lax.dot_general(qbd, kblk.astype(BF16), (((1,), (1,)), ((), ())), preferred_element_type=F32)
        dist = qpos - (step * bs + lane)
        s_ref[step] = s - slope * dist.astype(F32)

    @pl.when(step == nbp - 1)
    def _():
        kn = jnp.concatenate([kn_ref[...], jnp.zeros((LANES - t, D_MODEL), F32)], axis=0)
        kmean_ref[pl.ds(own_blk, 1), :] = jnp.sum(kn, axis=0, keepdims=True) * (1.0 / bs)
        qbd = qbd_ref[...]
        g = lax.dot_general(qbd.astype(BF16), kmean_ref[...].astype(BF16), (((1,), (1,)), ((), ())),
                            preferred_element_type=F32)
        col = lax.broadcasted_iota(jnp.int32, g.shape, 1)
        gm = jnp.where(col < own_blk, g, NEG_INF)
        rank = _slot_rank(gm, nb, 1)
        chosen = (rank < n_sel) & (rank < own_blk) & (col < nb)

        s_own = lax.dot_general((qbd * (HEAD_DIM ** -0.5)).astype(BF16), kn.astype(BF16),
                                (((1,), (1,)), ((), ())), preferred_element_type=F32)
        dist_own = (qpos - (past_len + lane))[:, :LANES]
        s_own = s_own - slope[:, :LANES] * dist_own.astype(F32)
        s_own = jnp.where(dist_own >= 0, s_own, NEG_INF)

        def sel_col(n):
            return jnp.sum(jnp.where((col == n) & chosen, 1.0, 0.0), axis=1, keepdims=True)

        def max_body(n, m):
            sn = jnp.where(sel_col(n) > 0.0, s_ref[n], NEG_INF)
            return jnp.maximum(m, jnp.max(sn, axis=1, keepdims=True))

        m = lax.fori_loop(0, nbp, max_body, jnp.max(s_own, axis=1, keepdims=True))

        def e_body(n, l):
            e = jnp.where(sel_col(n) > 0.0, jnp.exp(s_ref[n] - m), 0.0)
            s_ref[n] = e
            return l + jnp.sum(e, axis=1, keepdims=True)

        own_mult = 1.0 + sel_col(own_blk)
        e_own = jnp.exp(s_own - m) * own_mult
        l = lax.fori_loop(0, nbp, e_body, jnp.sum(e_own, axis=1, keepdims=True))
        pown_ref[...] = (e_own / l).astype(BF16)
        l_ref[...] = jnp.broadcast_to(l, l_ref.shape)

    @pl.when(step >= nbp)
    def _():
        vblk = jnp.concatenate([va_ref[...], vb_ref[...]], axis=0).astype(BF16)
        p = (s_ref[step - nbp] / l_ref[:, :1]).astype(BF16)
        acc_ref[...] += jnp.dot(p, vblk, preferred_element_type=F32)

    @pl.when(step == 2 * nbp - 1)
    def _():
        vn = jnp.concatenate([vn_ref[...], jnp.zeros((LANES - t, D_MODEL), F32)], axis=0).astype(BF16)
        acc = acc_ref[...] + jnp.dot(pown_ref[...], vn, preferred_element_type=F32)
        acc = head_diag(acc)
        o_ref[...] = jnp.sum(acc.reshape(N_HEADS, t, D_MODEL), axis=0)


def _moba_sample(q, k_new, v_new, k_pool, v_pool, page_table, slopes, db, t):
    n_pages = page_table.shape[1]
    nbp = n_pages // PAGES_PER_BLOCK
    hq = N_HEADS * t
    nb_pad = -(-(nbp + 1) // LANES) * LANES
    slope_rows = jnp.broadcast_to(jnp.repeat(slopes, t)[:, None], (hq, MOBA_BLOCK))
    new_rows = pl.BlockSpec((t, D_MODEL), lambda b, s, pt: (b, 0))

    def page(first, half):
        def index(b, s, pt):
            blk = jnp.clip(s - first, 0, nbp - 1)
            return (pt[b, blk * PAGES_PER_BLOCK + half], 0, 0)
        return pl.BlockSpec((None, PAGE_SIZE, D_MODEL), index)

    grid_spec = pltpu.PrefetchScalarGridSpec(
        num_scalar_prefetch=1,
        grid=(db, 2 * nbp),
        in_specs=[pl.BlockSpec((hq, MOBA_BLOCK), lambda b, s, pt: (0, 0)),
                  new_rows, new_rows, new_rows,
                  page(0, 0), page(0, 1), page(nbp, 0), page(nbp, 1)],
        out_specs=new_rows,
        scratch_shapes=[pltpu.VMEM((hq, D_MODEL), F32),
                        pltpu.VMEM((nb_pad, D_MODEL), F32),
                        pltpu.VMEM((nbp, hq, MOBA_BLOCK), F32),
                        pltpu.VMEM((hq, LANES), BF16),
                        pltpu.VMEM((hq, LANES), F32),
                        pltpu.VMEM((hq, D_MODEL), F32)])
    return pl.pallas_call(
        functools.partial(_moba_sample_kernel, t=t, nbp=nbp),
        out_shape=jax.ShapeDtypeStruct((db * t, D_MODEL), F32),
        grid_spec=grid_spec,
        compiler_params=_params("arbitrary", "arbitrary"),
        name="moba_sample",
    )(page_table, slope_rows, q, k_new, v_new, k_pool, k_pool, v_pool, v_pool)


def _rglru_kernel(u_ref, conv0_ref, h0_ref, cw_ref, cb_ref, wa_ref, ba_ref, wx_ref, bx_ref, lam_ref,
                  hy_ref, conv_ref, hlast_ref, halo_ref, h_ref, a_ref, b_ref, hs_ref, *, tt):
    i = pl.program_id(1)
    keep = CONV_W - 1
    first = CONV_HALO - keep

    @pl.when(i == 0)
    def _():
        halo_ref[first:CONV_HALO, :] = conv0_ref[...]
        h_ref[...] = h0_ref[...]

    u = u_ref[...]
    y_in = u[:, :D_RNN]
    y_branch = 0.5 * y_in * (1.0 + jnp.tanh(0.7978845608028654 * (y_in + 0.044715 * (y_in * y_in * y_in))))
    halo_ref[CONV_HALO:CONV_HALO + tt, :] = u[:, D_RNN:]
    xc = cb_ref[...]
    acc = None
    for j in range(CONV_W):
        term = halo_ref[first + j:first + j + tt, :] * cw_ref[j:j + 1, :]
        acc = term if acc is None else acc + term
    xc = xc + acc

    r_parts, i_parts = [], []
    for blk in range(N_LRU_BLOCKS):
        xb = xc[:, blk * LRU_BLOCK_WIDTH:(blk + 1) * LRU_BLOCK_WIDTH].astype(BF16)
        r_parts.append(jnp.dot(xb, wa_ref[blk], preferred_element_type=F32))
        i_parts.append(jnp.dot(xb, wx_ref[blk], preferred_element_type=F32))
    r = jax.nn.sigmoid(jnp.concatenate(r_parts, axis=1) + ba_ref[...])
    ig = jax.nn.sigmoid(jnp.concatenate(i_parts, axis=1) + bx_ref[...])
    log_a = LRU_C * r * jax.nn.log_sigmoid(lam_ref[...])
    a_ref[...] = jnp.exp(log_a)
    b_ref[...] = jnp.sqrt(1.0 - jnp.exp(2.0 * log_a)) * (ig * xc)

    def step(s, h):
        h = a_ref[pl.ds(s, 1), :] * h + b_ref[pl.ds(s, 1), :]
        hs_ref[pl.ds(s, 1), :] = h
        return h

    h = lax.fori_loop(0, tt, step, h_ref[...], unroll=8)
    h_ref[...] = h
    hy_ref[...] = hs_ref[...] * y_branch
    tail = halo_ref[first + tt:CONV_HALO + tt, :]
    halo_ref[first:CONV_HALO, :] = tail

    @pl.when(i == pl.num_programs(1) - 1)
    def _():
        conv_ref[...] = tail
        hlast_ref[...] = h


def _rglru(u, row0, conv0, h0, cw, cb, wa_bf, ba, wx_bf, bx, lam, b, t):
    tt = _row_tile(t, 256)
    nt = t // tt
    blk0 = row0 // tt
    keep = CONV_W - 1
    vec = pl.BlockSpec((1, D_RNN), lambda bi, i: (0, 0))
    gate_w = pl.BlockSpec((N_LRU_BLOCKS, LRU_BLOCK_WIDTH, LRU_BLOCK_WIDTH), lambda bi, i: (0, 0, 0))
    state3 = pl.BlockSpec((None, keep, D_RNN), lambda bi, i: (bi, 0, 0))
    state1 = pl.BlockSpec((None, 1, D_RNN), lambda bi, i: (bi, 0, 0))
    return pl.pallas_call(
        functools.partial(_rglru_kernel, tt=tt),
        out_shape=(jax.ShapeDtypeStruct((b * t, D_RNN), F32),
                   jax.ShapeDtypeStruct((b, keep, D_RNN), F32),
                   jax.ShapeDtypeStruct((b, 1, D_RNN), F32)),
        grid=(b, nt),
        in_specs=[pl.BlockSpec((tt, 2 * D_RNN), lambda bi, i: (blk0 + bi * nt + i, 0)),
                  state3, state1,
                  pl.BlockSpec((CONV_W, D_RNN), lambda bi, i: (0, 0)), vec,
                  gate_w, vec, gate_w, vec, vec],
        out_specs=(pl.BlockSpec((tt, D_RNN), lambda bi, i: (bi * nt + i, 0)), state3, state1),
        scratch_shapes=[pltpu.VMEM((CONV_HALO + tt, D_RNN), F32),
                        pltpu.VMEM((1, D_RNN), F32),
                        pltpu.VMEM((tt, D_RNN), F32),
                        pltpu.VMEM((tt, D_RNN), F32),
                        pltpu.VMEM((tt, D_RNN), F32)],
        compiler_params=_params("arbitrary", "arbitrary"),
        name="rglru",
    )(u, conv0, h0.reshape(b, 1, D_RNN), cw, cb.reshape(1, D_RNN), wa_bf, ba.reshape(1, D_RNN),
      wx_bf, bx.reshape(1, D_RNN), lam.reshape(1, D_RNN))


def _router_kernel(x_ref, wt_ref, b_ref, ids_ref, gates_ref, rank_ref, counts_ref, carry_ref):
    i = pl.program_id(0)
    tm = x_ref.shape[0]

    @pl.when(i == 0)
    def _():
        carry_ref[...] = jnp.zeros_like(carry_ref)

    logits = lax.dot_general(wt_ref[...].astype(BF16), x_ref[...].astype(BF16), (((1,), (1,)), ((), ())),
                             preferred_element_type=F32) + b_ref[:, :1]
    e_io = lax.broadcasted_iota(jnp.int32, logits.shape, 0)
    lt = logits
    vals, ids = [], []
    for _ in range(TOP_K):
        mx = jnp.max(lt, axis=0, keepdims=True)
        idx = jnp.min(jnp.where(lt == mx, e_io, N_EXPERTS), axis=0, keepdims=True)
        vals.append(mx)
        ids.append(idx)
        lt = jnp.where(e_io == idx, -jnp.inf, lt)
    ex = [jnp.exp(v - vals[0]) for v in vals]
    den = ex[0] + ex[1] + ex[2] + ex[3]
    gates_ref[...] = jnp.concatenate([e / den for e in ex], axis=0)
    ids_ref[...] = jnp.concatenate(ids, axis=0)

    onehot = jnp.zeros(logits.shape, F32)
    for idx in ids:
        onehot = onehot + jnp.where(e_io == idx, 1.0, 0.0)
    s_io = lax.broadcasted_iota(jnp.int32, (tm, tm), 0)
    t_io = lax.broadcasted_iota(jnp.int32, (tm, tm), 1)
    before = jnp.where(s_io < t_io, 1.0, 0.0).astype(BF16)
    prefix = jnp.dot(onehot.astype(BF16), before, preferred_element_type=F32) + carry_ref[:, :1]
    ranks = [jnp.sum(jnp.where(e_io == idx, prefix, 0.0), axis=0, keepdims=True) for idx in ids]
    rank_ref[...] = jnp.concatenate(ranks, axis=0).astype(jnp.int32)
    carry_ref[...] += jnp.sum(onehot, axis=1, keepdims=True)
    counts_ref[...] = carry_ref[...].astype(jnp.int32)


def _router(x, w_router, b_router):
    n = x.shape[0]
    tm = _row_tile(n, ROUTER_TILE)
    slot = pl.BlockSpec((TOP_K, tm), lambda i: (0, i))
    per_expert = pl.BlockSpec((N_EXPERTS, LANES), lambda i: (0, 0))
    return pl.pallas_call(
        _router_kernel,
        out_shape=(jax.ShapeDtypeStruct((TOP_K, n), jnp.int32),
                   jax.ShapeDtypeStruct((TOP_K, n), F32),
                   jax.ShapeDtypeStruct((TOP_K, n), jnp.int32),
                   jax.ShapeDtypeStruct((N_EXPERTS, LANES), jnp.int32)),
        grid=(n // tm,),
        in_specs=[pl.BlockSpec((tm, D_MODEL), lambda i: (i, 0)),
                  pl.BlockSpec((N_EXPERTS, D_MODEL), lambda i: (0, 0)),
                  per_expert],
        out_specs=(slot, slot, slot, per_expert),
        scratch_shapes=[pltpu.VMEM((N_EXPERTS, LANES), F32)],
        compiler_params=_params("arbitrary"),
        name="moe_router",
    )(x, w_router.T, jnp.broadcast_to(b_router[:, None], (N_EXPERTS, LANES)))


def _expert_kernel(te_ref, tv_ref, xg_ref, gate_ref, wu_ref, bu_ref, wd_ref, bd_ref, y_ref,
                   wu_bf_ref, wd_bf_ref):
    g = pl.program_id(0)
    e = te_ref[g]
    prev = te_ref[jnp.maximum(g - 1, 0)]

    @pl.when((g == 0) | (e != prev))
    def _():
        wu_bf_ref[...] = wu_ref[...].astype(BF16)
        wd_bf_ref[...] = wd_ref[...].astype(BF16)

    @pl.when(tv_ref[g] > 0)
    def _():
        gu = jnp.dot(xg_ref[...], wu_bf_ref[...], preferred_element_type=F32) + bu_ref[...]
        gate = jnp.minimum(gu[:, :D_EXPERT], SWIGLU_LIMIT)
        up = jnp.clip(gu[:, D_EXPERT:], -SWIGLU_LIMIT, SWIGLU_LIMIT)
        glu = gate * jax.nn.sigmoid(SWIGLU_ALPHA * gate)
        y = jnp.dot(((up + 1.0) * glu).astype(BF16), wd_bf_ref[...], preferred_element_type=F32) + bd_ref[...]
        y_ref[...] = y * gate_ref[...]

    @pl.when(tv_ref[g] == 0)
    def _():
        y_ref[...] = jnp.zeros_like(y_ref)


def _experts(tile_e, tile_valid, xg, gate_rows, w_up, b_up, w_down, b_down):
    n_rows = xg.shape[0]
    tm = EXPERT_TILE_ROWS
    grid_spec = pltpu.PrefetchScalarGridSpec(
        num_scalar_prefetch=2,
        grid=(n_rows // tm,),
        in_specs=[pl.BlockSpec((tm, D_MODEL), lambda g, te, tv: (g, 0)),
                  pl.BlockSpec((tm, 1), lambda g, te, tv: (g, 0)),
                  pl.BlockSpec((None, D_MODEL, 2 * D_EXPERT), lambda g, te, tv: (te[g], 0, 0)),
                  pl.BlockSpec((None, 1, 2 * D_EXPERT), lambda g, te, tv: (te[g], 0, 0)),
                  pl.BlockSpec((None, D_EXPERT, D_MODEL), lambda g, te, tv: (te[g], 0, 0)),
                  pl.BlockSpec((None, 1, D_MODEL), lambda g, te, tv: (te[g], 0, 0))],
        out_specs=pl.BlockSpec((tm, D_MODEL), lambda g, te, tv: (g, 0)),
        scratch_shapes=[pltpu.VMEM((D_MODEL, 2 * D_EXPERT), BF16),
                        pltpu.VMEM((D_EXPERT, D_MODEL), BF16)])
    return pl.pallas_call(
        _expert_kernel,
        out_shape=jax.ShapeDtypeStruct((n_rows, D_MODEL), F32),
        grid_spec=grid_spec,
        compiler_params=_params("arbitrary"),
        name="moe_experts",
    )(tile_e, tile_valid, xg, gate_rows, w_up, b_up.reshape(N_EXPERTS, 1, 2 * D_EXPERT),
      w_down, b_down.reshape(N_EXPERTS, 1, D_MODEL))


def _moe(x, w_router, b_router, w_up, b_up, w_down, b_down):
    n = x.shape[0]
    tm = EXPERT_TILE_ROWS
    ids, gates, rank, counts = _router(x, w_router, b_router)
    counts = counts[:, 0]
    padded = (counts + tm - 1) // tm * tm
    pend = jnp.cumsum(padded)
    pstart = pend - padded
    dest = pstart[ids] + rank
    n_tiles = -(-(n * TOP_K) // tm) + N_EXPERTS
    n_rows = n_tiles * tm
    tile_row0 = jnp.arange(n_tiles, dtype=jnp.int32) * tm
    tile_e = jnp.minimum(jnp.searchsorted(pend, tile_row0, side="right"), N_EXPERTS - 1).astype(jnp.int32)
    tile_valid = (tile_row0 < pend[-1]).astype(jnp.int32)
    tok = jnp.broadcast_to(jnp.arange(n, dtype=jnp.int32)[None, :], (TOP_K, n))
    tok_buf = jnp.full((n_rows,), n, jnp.int32).at[dest.reshape(-1)].set(tok.reshape(-1))
    gate_buf = jnp.zeros((n_rows,), F32).at[dest.reshape(-1)].set(gates.reshape(-1))
    x_pad = jnp.concatenate([x.astype(BF16), jnp.zeros((1, D_MODEL), BF16)], axis=0)
    xg = x_pad[tok_buf]
    yg = _experts(tile_e, tile_valid, xg, gate_buf[:, None], w_up, b_up, w_down, b_down)
    return jnp.sum(yg[dest], axis=0)


def kernel(x_prompt, x_sample, cache_k, cache_v, state_conv, state_h, page_table, ln_g, ln_b, w_qkv, w_o, w_in, b_in, conv_w, conv_b, w_gate_a, b_gate_a, w_gate_x, b_gate_x, lru_lambda, w_out, b_out, w_router, b_router, w_up, b_up, w_down, b_down):
    bp, tp, d = x_prompt.shape
    db, ts, _ = x_sample.shape
    n_p, n_s = bp * tp, db * ts
    xp = x_prompt.reshape(n_p, d)
    xs = x_sample.reshape(n_s, d)
    slopes = 2.0 ** (-8.0 * jnp.arange(1, N_HEADS + 1, dtype=F32) / N_HEADS)
    zeros_d = jnp.zeros((d,), F32)

    def moe_block(xp, xs, layer):
        x = jnp.concatenate([xp, xs], axis=0)
        y = _moe(x, w_router[layer], b_router[layer], w_up[layer], b_up[layer], w_down[layer], b_down[layer])
        x = _res_ln(x, y, ln_g[layer, 1], ln_b[layer, 1])
        return x[:n_p], x[n_p:]

    w_qkv_bf = w_qkv[0].astype(BF16)
    w_o_bf = w_o[0].astype(BF16)
    qp, kp, vp = _qkv(xp, w_qkv_bf)
    qs, ks, vs = _qkv(xs, w_qkv_bf)
    op = _moba_prompt(qp, kp, vp, slopes, bp, tp)
    n_phys = cache_k.shape[1]
    os_ = _moba_sample(qs, ks, vs, cache_k[0].reshape(n_phys, PAGE_SIZE, d),
                       cache_v[0].reshape(n_phys, PAGE_SIZE, d), page_table, slopes, db, ts)
    xp = _linear_res_ln(op, w_o_bf, zeros_d, xp, ln_g[0, 0], ln_b[0, 0])
    xs = _linear_res_ln(os_, w_o_bf, zeros_d, xs, ln_g[0, 0], ln_b[0, 0])
    xp, xs = moe_block(xp, xs, 0)

    w_in_bf = w_in[0].astype(BF16)
    wa_bf = w_gate_a[0].astype(BF16)
    wx_bf = w_gate_x[0].astype(BF16)
    rec = (conv_w[0], conv_b[0], wa_bf, b_gate_a[0].reshape(-1), wx_bf, b_gate_x[0].reshape(-1), lru_lambda[0])
    up = _linear(xp, w_in_bf, b_in[0])
    us = _linear(xs, w_in_bf, b_in[0])
    conv0 = jnp.zeros((bp, CONV_W - 1, D_RNN), F32)
    h0 = jnp.zeros((bp, D_RNN), F32)
    hyp, conv_p, h_p = _rglru(up, 0, conv0, h0, *rec, bp, tp)
    hys, conv_s, h_s = _rglru(us, 0, state_conv[0], state_h[0], *rec, db, ts)
    w_out_bf = w_out[0].astype(BF16)
    xp = _linear_res_ln(hyp, w_out_bf, b_out[0], xp, ln_g[1, 0], ln_b[1, 0])
    xs = _linear_res_ln(hys, w_out_bf, b_out[0], xs, ln_g[1, 0], ln_b[1, 0])
    xp, xs = moe_block(xp, xs, 1)

    kv_p = (1, bp, tp, N_HEADS, HEAD_DIM)
    kv_s = (1, db, ts, N_HEADS, HEAD_DIM)
    return (xp.reshape(bp, tp, d), xs.reshape(db, ts, d),
            kp.reshape(kv_p), vp.reshape(kv_p), conv_p[None], h_p.reshape(1, bp, D_RNN),
            ks.reshape(kv_s), vs.reshape(kv_s), conv_s[None], h_s.reshape(1, db, D_RNN))
```

```python
import functools

import jax
import jax.numpy as jnp
from jax import lax
from jax.experimental import pallas as pl
from jax.experimental.pallas import tpu as pltpu

F32 = jnp.float32
BF16 = jnp.bfloat16

D_MODEL = 1024
N_HEADS = 16
HEAD_DIM = D_MODEL // N_HEADS
MOBA_BLOCK = 256
MOBA_TOPK = 3
PAGE_SIZE = 128
PAGES_PER_BLOCK = MOBA_BLOCK // PAGE_SIZE
D_RNN = D_MODEL
LRU_BLOCK_WIDTH = 256
N_LRU_BLOCKS = D_RNN // LRU_BLOCK_WIDTH
CONV_W = 4
LRU_C = 8.0
N_EXPERTS = 32
TOP_K = 4
D_EXPERT = D_MODEL
SWIGLU_LIMIT = 7.0
SWIGLU_ALPHA = 1.702
DEPTH = 2
DEEPNORM_ALPHA = (2 * DEPTH) ** 0.25
LN_EPS = 1e-5
NEG_INF = -1e30

V7X_VMEM_LIMIT_BYTES = 56 * 1024 * 1024
SUBLANES = 8
LANES = 128
CONV_HALO = SUBLANES
EXPERT_TILE_ROWS = 512
ROUTER_TILE = 256
DISPATCH_TILE = 1280
COMBINE_TILE = 256


def _params(*sem):
    return pltpu.CompilerParams(dimension_semantics=sem, vmem_limit_bytes=V7X_VMEM_LIMIT_BYTES)


def _row_tile(m, pref):
    t = min(pref, m)
    while m % t or t % SUBLANES:
        t -= SUBLANES
    return t


def _div_pow2(x, n):
    assert n & (n - 1) == 0, n
    return lax.shift_right_logical(x, n.bit_length() - 1)


def _layer_norm(y, g, b):
    mu = jnp.mean(y, axis=-1, keepdims=True)
    yc = y - mu
    var = jnp.mean(yc * yc, axis=-1, keepdims=True)
    return yc * lax.rsqrt(var + LN_EPS) * g + b


def _qkv_kernel(x_ref, w_ref, q_ref, k_ref, v_ref):
    x = x_ref[...].astype(BF16)
    for c, o_ref in enumerate((q_ref, k_ref, v_ref)):
        o_ref[...] = jnp.dot(x, w_ref[:, c * D_MODEL:(c + 1) * D_MODEL], preferred_element_type=F32)


def _qkv(x, w_bf):
    m = x.shape[0]
    tm = _row_tile(m, 512)
    out = jax.ShapeDtypeStruct((m, D_MODEL), F32)
    row = pl.BlockSpec((tm, D_MODEL), lambda i: (i, 0))
    return pl.pallas_call(
        _qkv_kernel,
        out_shape=(out, out, out),
        grid=(m // tm,),
        in_specs=[row, pl.BlockSpec((D_MODEL, 3 * D_MODEL), lambda i: (0, 0))],
        out_specs=(row, row, row),
        compiler_params=_params("arbitrary"),
        name="qkv_proj",
    )(x, w_bf)


def _qkv_prompt_kernel(x_ref, wq_ref, wkt_ref, wvt_ref, q_ref, kt_ref, vt_ref):
    x = x_ref[...].astype(BF16)
    nt = (((1,), (1,)), ((), ()))
    q_ref[...] = jnp.dot(x, wq_ref[...], preferred_element_type=F32)
    kt_ref[...] = lax.dot_general(wkt_ref[...], x, nt, preferred_element_type=F32)
    vt_ref[...] = lax.dot_general(wvt_ref[...], x, nt, preferred_element_type=F32)


def _qkv_prompt(x, wq_bf, wkt_bf, wvt_bf, b, t):
    tm = _row_tile(t, 512)
    nt = t // tm
    w = pl.BlockSpec((D_MODEL, D_MODEL), lambda bi, i: (0, 0))
    row = pl.BlockSpec((tm, D_MODEL), lambda bi, i: (bi * nt + i, 0))
    col = pl.BlockSpec((None, D_MODEL, tm), lambda bi, i: (bi, 0, i))
    t_out = jax.ShapeDtypeStruct((b, D_MODEL, t), F32)
    return pl.pallas_call(
        _qkv_prompt_kernel,
        out_shape=(jax.ShapeDtypeStruct((b * t, D_MODEL), F32), t_out, t_out),
        grid=(b, nt),
        in_specs=[row, w, w, w],
        out_specs=(row, col, col),
        compiler_params=_params("arbitrary", "arbitrary"),
        name="qkv_proj_prompt",
    )(x, wq_bf, wkt_bf, wvt_bf)


def _linear_kernel(x_ref, w_ref, b_ref, o_ref):
    x = x_ref[...].astype(BF16)
    o_ref[...] = jnp.dot(x, w_ref[...], preferred_element_type=F32) + b_ref[...]


def _linear(x, w_bf, b):
    m, k = x.shape
    n = w_bf.shape[1]
    tm = _row_tile(m, 512)
    tn = min(n, 1024)
    return pl.pallas_call(
        _linear_kernel,
        out_shape=jax.ShapeDtypeStruct((m, n), F32),
        grid=(m // tm, n // tn),
        in_specs=[pl.BlockSpec((tm, k), lambda i, j: (i, 0)),
                  pl.BlockSpec((k, tn), lambda i, j: (0, j)),
                  pl.BlockSpec((1, tn), lambda i, j: (0, j))],
        out_specs=pl.BlockSpec((tm, tn), lambda i, j: (i, j)),
        compiler_params=_params("arbitrary", "arbitrary"),
        name="linear",
    )(x, w_bf, b.reshape(1, n))


def _linear_res_ln_kernel(a_ref, w_ref, b_ref, res_ref, g_ref, beta_ref, o_ref):
    m = jnp.dot(a_ref[...].astype(BF16), w_ref[...], preferred_element_type=F32) + b_ref[...]
    o_ref[...] = _layer_norm(DEEPNORM_ALPHA * res_ref[...] + m, g_ref[...], beta_ref[...])


def _linear_res_ln(a, w_bf, b, res, g, beta):
    m, k = a.shape
    n = w_bf.shape[1]
    tm = _row_tile(m, 512)
    row_in = pl.BlockSpec((tm, k), lambda i: (i, 0))
    row = pl.BlockSpec((tm, n), lambda i: (i, 0))
    vec = pl.BlockSpec((1, n), lambda i: (0, 0))
    return pl.pallas_call(
        _linear_res_ln_kernel,
        out_shape=jax.ShapeDtypeStruct((m, n), F32),
        grid=(m // tm,),
        in_specs=[row_in, pl.BlockSpec((k, n), lambda i: (0, 0)), vec, row, vec, vec],
        out_specs=row,
        compiler_params=_params("arbitrary"),
        name="linear_res_ln",
    )(a, w_bf, b.reshape(1, n), res, g.reshape(1, n), beta.reshape(1, n))


def _slot_rank(gm, n_blocks, axis):
    blk = lax.broadcasted_iota(jnp.int32, gm.shape, axis)
    rank = jnp.zeros(gm.shape, jnp.int32)
    for m in range(n_blocks):
        g_m = jnp.sum(jnp.where(blk == m, gm, 0.0), axis=axis, keepdims=True)
        beats = (g_m > gm) | ((g_m == gm) & (m < blk))
        rank = rank + jnp.where(beats, 1, 0)
    return rank


def _moba_prompt_kernel(slope_ref, q_ref, kt_ref, vt_ref, o_ref, *, t):
    nb = t // MOBA_BLOCK
    n_sel = min(MOBA_TOPK, nb)
    bs = MOBA_BLOCK
    hp = pl.program_id(1)
    heads_per_step = LANES // HEAD_DIM
    assert 2 * nb <= HEAD_DIM

    rc = (lax.broadcasted_iota(jnp.int32, (bs, t), 0)
          - lax.broadcasted_iota(jnp.int32, (bs, t), 1)).astype(F32)
    causal_bias = jnp.where(lax.broadcasted_iota(jnp.int32, (bs, bs), 0)
                            >= lax.broadcasted_iota(jnp.int32, (bs, bs), 1), 0.0, NEG_INF)

    blk = lax.broadcasted_iota(jnp.int32, (nb, t), 0)
    own = _div_pow2(lax.broadcasted_iota(jnp.int32, (nb, t), 1), MOBA_BLOCK)
    pad_rows = jnp.zeros((LANES - 2 * nb, t), F32)
    block_mean = jnp.where(blk == own, 1.0 / bs, 0.0).astype(BF16)
    blk64 = lax.broadcasted_iota(jnp.int32, (HEAD_DIM, t), 0)
    own64 = _div_pow2(lax.broadcasted_iota(jnp.int32, (HEAD_DIM, t), 1), MOBA_BLOCK)
    block_flag = jnp.where(blk64 == own64, 1.0, 0.0).astype(BF16)
    nt = (((1,), (1,)), ((), ()))

    outs = []
    for j in range(heads_per_step):
        slope = slope_ref[hp * heads_per_step + j]
        lo, hi = j * HEAD_DIM, (j + 1) * HEAD_DIM
        qh = q_ref[:, lo:hi]
        kt = kt_ref[lo:hi, :]
        vt = vt_ref[lo:hi, :]

        kb = kt.astype(BF16)
        k_lo = (kt - kb.astype(F32)).astype(BF16)
        kmean = (lax.dot_general(block_mean, kb, nt, preferred_element_type=F32)
                 + lax.dot_general(block_mean, k_lo, nt, preferred_element_type=F32))
        g_t = lax.dot_general(kmean.astype(BF16), qh.astype(BF16), nt, preferred_element_type=F32)
        gm = jnp.where(blk < own, g_t, NEG_INF)
        rank = _slot_rank(gm, nb, 0)
        chosen = (rank < n_sel) & (rank < own)
        sel_bias_t = jnp.where(chosen | (blk >= own), 0.0, NEG_INF)
        own_mult_t = jnp.where(chosen, 2.0, 1.0)
        sel = jnp.transpose(jnp.concatenate([sel_bias_t, own_mult_t, pad_rows], axis=0))

        q_aug = jnp.concatenate([(qh * (HEAD_DIM ** -0.5)).astype(BF16),
                                 sel[:, :HEAD_DIM].astype(BF16)], axis=1)
        k_aug = jnp.concatenate([kb, block_flag], axis=0)
        vb = vt.astype(BF16)
        alibi = slope * rc
        o_blocks = []
        for i in range(nb):
            w = (i + 1) * bs
            rows = slice(i * bs, (i + 1) * bs)
            s = jnp.dot(q_aug[rows], k_aug[:, :w], preferred_element_type=F32) - alibi[:, :w]
            s_own = s[:, i * bs:] + causal_bias
            s = jnp.concatenate([s[:, :i * bs], s_own], axis=1) if i else s_own
            m = jnp.max(s, axis=1, keepdims=True)
            p = jnp.exp(s - m)
            p_own = p[:, i * bs:] * sel[rows, nb + i:nb + i + 1]
            p = jnp.concatenate([p[:, :i * bs], p_own], axis=1) if i else p_own
            p = p * (1.0 / jnp.sum(p, axis=1, keepdims=True))
            o_blocks.append(lax.dot_general(p.astype(BF16), vb[:, :w], nt, preferred_element_type=F32))
        outs.append(jnp.concatenate(o_blocks, axis=0))
    o_ref[...] = jnp.concatenate(outs, axis=1)


def _moba_prompt(q, kt, vt, slopes, b, t):
    blk = pl.BlockSpec((t, LANES), lambda bi, hp: (bi, hp))
    blk_t = pl.BlockSpec((None, LANES, t), lambda bi, hp: (bi, hp, 0))
    return pl.pallas_call(
        functools.partial(_moba_prompt_kernel, t=t),
        out_shape=jax.ShapeDtypeStruct((b * t, D_MODEL), F32),
        grid=(b, D_MODEL // LANES),
        in_specs=[pl.BlockSpec(memory_space=pltpu.SMEM), blk, blk_t, blk_t],
        out_specs=blk,
        compiler_params=_params("arbitrary", "arbitrary"),
        name="moba_prompt",
    )(slopes, q, kt, vt)


def _moba_sample_kernel(pt_ref, slope_ref, q_ref, kn_ref, vn_ref, ka_ref, kb_ref, va_ref, vb_ref,
                        o_ref, qbd_ref, kmean_ref, s_ref, pown_ref, l_ref, acc_ref,
                        *, t, nbp):
    del pt_ref
    step = pl.program_id(1)
    bs = MOBA_BLOCK
    hq = N_HEADS * t
    past_len = nbp * bs
    own_blk = nbp
    nb = nbp + 1
    n_sel = min(MOBA_TOPK, nb)

    row = lax.broadcasted_iota(jnp.int32, (hq, bs), 0)
    lane = lax.broadcasted_iota(jnp.int32, (hq, bs), 1)
    assert t & (t - 1) == 0, t
    qpos = past_len + (row & (t - 1))
    slope = slope_ref[...]

    def head_diag(x):
        r = _div_pow2(lax.broadcasted_iota(jnp.int32, x.shape, 0), t)
        c = _div_pow2(lax.broadcasted_iota(jnp.int32, x.shape, 1), HEAD_DIM)
        return jnp.where(r == c, x, 0.0)

    @pl.when(step == 0)
    def _():
        q_rep = jnp.concatenate([q_ref[...]] * N_HEADS, axis=0)
        qbd_ref[...] = head_diag(q_rep)
        kmean_ref[...] = jnp.zeros_like(kmean_ref)
        acc_ref[...] = jnp.zeros_like(acc_ref)

    @pl.when(step < nbp)
    def _():
        ka, kb = ka_ref[...], kb_ref[...]
        ksum = jnp.sum(ka + kb, axis=1, keepdims=True) * (1.0 / bs)
        blk_lane = lax.broadcasted_iota(jnp.int32, kmean_ref.shape, 1)
        kmean_ref[...] += jnp.where(blk_lane == step, ksum, 0.0)
        qbd = (qbd_ref[...] * (HEAD_DIM ** -0.5)).astype(BF16)
        s = jnp.concatenate([jnp.dot(qbd, ka.astype(BF16), preferred_element_type=F32),
                             jnp.dot(qbd, kb.astype(BF16), preferred_element_type=F32)], axis=1)
        dist = qpos - (step * bs + lane)
        s_ref[step] = s - slope * dist.astype(F32)

    @pl.when(step == nbp - 1)
    def _():
        qbd = qbd_ref[...]
        g = jnp.dot(qbd.astype(BF16), kmean_ref[...].astype(BF16), preferred_element_type=F32)
        col = lax.broadcasted_iota(jnp.int32, g.shape, 1)
        gm = jnp.where(col < own_blk, g, NEG_INF)
        rank = _slot_rank(gm, nb, 1)
        chosen = (rank < n_sel) & (rank < own_blk) & (col < nb)

        kn = jnp.concatenate([kn_ref[...], jnp.zeros((LANES - t, D_MODEL), F32)], axis=0)
        s_own = lax.dot_general((qbd * (HEAD_DIM ** -0.5)).astype(BF16), kn.astype(BF16),
                                (((1,), (1,)), ((), ())), preferred_element_type=F32)
        dist_own = (qpos - (past_len + lane))[:, :LANES]
        s_own = s_own - slope[:, :LANES] * dist_own.astype(F32)
        s_own = jnp.where(dist_own >= 0, s_own, NEG_INF)

        def sel_col(n):
            return jnp.sum(jnp.where((col == n) & chosen, 1.0, 0.0), axis=1, keepdims=True)

        def max_body(n, m):
            sn = jnp.where(sel_col(n) > 0.0, s_ref[n], NEG_INF)
            return jnp.maximum(m, jnp.max(sn, axis=1, keepdims=True))

        m = lax.fori_loop(0, nbp, max_body, jnp.max(s_own, axis=1, keepdims=True))

        def e_body(n, l):
            e = jnp.where(sel_col(n) > 0.0, jnp.exp(s_ref[n] - m), 0.0)
            s_ref[n] = e
            return l + jnp.sum(e, axis=1, keepdims=True)

        own_mult = 1.0 + sel_col(own_blk)
        e_own = jnp.exp(s_own - m) * own_mult
        l = lax.fori_loop(0, nbp, e_body, jnp.sum(e_own, axis=1, keepdims=True))
        pown_ref[...] = (e_own / l).astype(BF16)
        l_ref[...] = jnp.broadcast_to(l, l_ref.shape)

    @pl.when(step >= nbp)
    def _():
        p = (s_ref[step - nbp] / l_ref[:, :1]).astype(BF16)
        nt = (((1,), (1,)), ((), ()))
        acc_ref[...] += (
            lax.dot_general(p[:, :PAGE_SIZE], va_ref[...].astype(BF16), nt, preferred_element_type=F32)
            + lax.dot_general(p[:, PAGE_SIZE:], vb_ref[...].astype(BF16), nt, preferred_element_type=F32))

    @pl.when(step == 2 * nbp - 1)
    def _():
        vn = jnp.concatenate([vn_ref[...], jnp.zeros((LANES - t, D_MODEL), F32)], axis=0).astype(BF16)
        acc = acc_ref[...] + jnp.dot(pown_ref[...], vn, preferred_element_type=F32)
        acc = head_diag(acc)
        o_ref[...] = jnp.sum(acc.reshape(N_HEADS, t, D_MODEL), axis=0)


def _moba_sample(q, k_new, v_new, kt_pool, vt_pool, page_table, slopes, db, t):
    n_pages = page_table.shape[1]
    assert n_pages % PAGES_PER_BLOCK == 0 and PAGES_PER_BLOCK == 2 and t <= LANES
    nbp = n_pages // PAGES_PER_BLOCK
    hq = N_HEADS * t
    nb_pad = -(-(nbp + 1) // LANES) * LANES
    slope_rows = jnp.broadcast_to(jnp.repeat(slopes, t)[:, None], (hq, MOBA_BLOCK))
    new_rows = pl.BlockSpec((t, D_MODEL), lambda b, s, pt: (b, 0))

    def page(first, half):
        def index(b, s, pt):
            blk = jnp.clip(s - first, 0, nbp - 1)
            return (pt[b, blk * PAGES_PER_BLOCK + half], 0, 0)
        return pl.BlockSpec((None, D_MODEL, PAGE_SIZE), index)

    grid_spec = pltpu.PrefetchScalarGridSpec(
        num_scalar_prefetch=1,
        grid=(db, 2 * nbp),
        in_specs=[pl.BlockSpec((hq, MOBA_BLOCK), lambda b, s, pt: (0, 0)),
                  new_rows, new_rows, new_rows,
                  page(0, 0), page(0, 1), page(nbp, 0), page(nbp, 1)],
        out_specs=new_rows,
        scratch_shapes=[pltpu.VMEM((hq, D_MODEL), F32),
                        pltpu.VMEM((D_MODEL, nb_pad), F32),
                        pltpu.VMEM((nbp, hq, MOBA_BLOCK), F32),
                        pltpu.VMEM((hq, LANES), BF16),
                        pltpu.VMEM((hq, LANES), F32),
                        pltpu.VMEM((hq, D_MODEL), F32)])
    return pl.pallas_call(
        functools.partial(_moba_sample_kernel, t=t, nbp=nbp),
        out_shape=jax.ShapeDtypeStruct((db * t, D_MODEL), F32),
        grid_spec=grid_spec,
        compiler_params=_params("arbitrary", "arbitrary"),
        name="moba_sample",
    )(page_table, slope_rows, q, k_new, v_new, kt_pool, kt_pool, vt_pool, vt_pool)


def _rglru_kernel(u_ref, conv0_ref, h0_ref, cw_ref, cb_ref, wa_ref, ba_ref, wx_ref, bx_ref, lam_ref,
                  hy_ref, conv_ref, hlast_ref, halo_ref, h_ref, a_ref, b_ref, hs_ref, *, tt):
    i = pl.program_id(1)
    keep = CONV_W - 1
    first = CONV_HALO - keep

    @pl.when(i == 0)
    def _():
        halo_ref[first:CONV_HALO, :] = conv0_ref[...]
        h_ref[...] = h0_ref[...]

    u = u_ref[...]
    y_in = u[:, :D_RNN]
    y_branch = 0.5 * y_in * (1.0 + jnp.tanh(0.7978845608028654 * (y_in + 0.044715 * (y_in * y_in * y_in))))
    halo_ref[CONV_HALO:CONV_HALO + tt, :] = u[:, D_RNN:]
    xc = cb_ref[...]
    acc = None
    for j in range(CONV_W):
        term = halo_ref[first + j:first + j + tt, :] * cw_ref[j:j + 1, :]
        acc = term if acc is None else acc + term
    xc = xc + acc

    r_parts, i_parts = [], []
    for blk in range(N_LRU_BLOCKS):
        xb = xc[:, blk * LRU_BLOCK_WIDTH:(blk + 1) * LRU_BLOCK_WIDTH].astype(BF16)
        r_parts.append(jnp.dot(xb, wa_ref[blk], preferred_element_type=F32))
        i_parts.append(jnp.dot(xb, wx_ref[blk], preferred_element_type=F32))
    r = jax.nn.sigmoid(jnp.concatenate(r_parts, axis=1) + ba_ref[...])
    ig = jax.nn.sigmoid(jnp.concatenate(i_parts, axis=1) + bx_ref[...])
    log_a = LRU_C * r * jax.nn.log_sigmoid(lam_ref[...])
    a_ref[...] = jnp.exp(log_a)
    b_ref[...] = jnp.sqrt(1.0 - jnp.exp(2.0 * log_a)) * (ig * xc)

    def step(s, h):
        h = a_ref[pl.ds(s, 1), :] * h + b_ref[pl.ds(s, 1), :]
        hs_ref[pl.ds(s, 1), :] = h
        return h

    h = lax.fori_loop(0, tt, step, h_ref[...], unroll=8)
    h_ref[...] = h
    hy_ref[...] = hs_ref[...] * y_branch
    tail = halo_ref[first + tt:CONV_HALO + tt, :]
    halo_ref[first:CONV_HALO, :] = tail

    @pl.when(i == pl.num_programs(1) - 1)
    def _():
        conv_ref[...] = tail
        hlast_ref[...] = h


def _rglru(u, row0, conv0, h0, cw, cb, wa_bf, ba, wx_bf, bx, lam, b, t):
    tt = _row_tile(t, 256)
    nt = t // tt
    blk0 = row0 // tt
    keep = CONV_W - 1
    vec = pl.BlockSpec((1, D_RNN), lambda bi, i: (0, 0))
    gate_w = pl.BlockSpec((N_LRU_BLOCKS, LRU_BLOCK_WIDTH, LRU_BLOCK_WIDTH), lambda bi, i: (0, 0, 0))
    state3 = pl.BlockSpec((None, keep, D_RNN), lambda bi, i: (bi, 0, 0))
    state1 = pl.BlockSpec((None, 1, D_RNN), lambda bi, i: (bi, 0, 0))
    return pl.pallas_call(
        functools.partial(_rglru_kernel, tt=tt),
        out_shape=(jax.ShapeDtypeStruct((b * t, D_RNN), F32),
                   jax.ShapeDtypeStruct((b, keep, D_RNN), F32),
                   jax.ShapeDtypeStruct((b, 1, D_RNN), F32)),
        grid=(b, nt),
        in_specs=[pl.BlockSpec((tt, 2 * D_RNN), lambda bi, i: (blk0 + bi * nt + i, 0)),
                  state3, state1,
                  pl.BlockSpec((CONV_W, D_RNN), lambda bi, i: (0, 0)), vec,
                  gate_w, vec, gate_w, vec, vec],
        out_specs=(pl.BlockSpec((tt, D_RNN), lambda bi, i: (bi * nt + i, 0)), state3, state1),
        scratch_shapes=[pltpu.VMEM((CONV_HALO + tt, D_RNN), F32),
                        pltpu.VMEM((1, D_RNN), F32),
                        pltpu.VMEM((tt, D_RNN), F32),
                        pltpu.VMEM((tt, D_RNN), F32),
                        pltpu.VMEM((tt, D_RNN), F32)],
        compiler_params=_params("arbitrary", "arbitrary"),
        name="rglru",
    )(u, conv0, h0.reshape(b, 1, D_RNN), cw, cb.reshape(1, D_RNN), wa_bf, ba.reshape(1, D_RNN),
      wx_bf, bx.reshape(1, D_RNN), lam.reshape(1, D_RNN))


def _router_kernel(x_ref, wt_ref, b_ref, ids_ref, gates_ref, rank_ref, counts_ref, carry_ref):
    i = pl.program_id(0)
    tm = x_ref.shape[0]

    @pl.when(i == 0)
    def _():
        carry_ref[...] = jnp.zeros_like(carry_ref)

    logits = lax.dot_general(wt_ref[...].astype(BF16), x_ref[...].astype(BF16), (((1,), (1,)), ((), ())),
                             preferred_element_type=F32) + b_ref[:, :1]
    e_io = lax.broadcasted_iota(jnp.int32, logits.shape, 0)
    lt = logits
    vals, ids = [], []
    for _ in range(TOP_K):
        mx = jnp.max(lt, axis=0, keepdims=True)
        idx = jnp.min(jnp.where(lt == mx, e_io, N_EXPERTS), axis=0, keepdims=True)
        vals.append(mx)
        ids.append(idx)
        lt = jnp.where(e_io == idx, -jnp.inf, lt)
    ex = [jnp.exp(v - vals[0]) for v in vals]
    den = ex[0] + ex[1] + ex[2] + ex[3]
    gates_ref[...] = jnp.concatenate([e / den for e in ex], axis=0)
    ids_ref[...] = jnp.concatenate(ids, axis=0)

    onehot = jnp.zeros(logits.shape, F32)
    for idx in ids:
        onehot = onehot + jnp.where(e_io == idx, 1.0, 0.0)
    s_io = lax.broadcasted_iota(jnp.int32, (tm, tm), 0)
    t_io = lax.broadcasted_iota(jnp.int32, (tm, tm), 1)
    before = jnp.where(s_io < t_io, 1.0, 0.0).astype(BF16)
    prefix = jnp.dot(onehot.astype(BF16), before, preferred_element_type=F32) + carry_ref[:, :1]
    ranks = [jnp.sum(jnp.where(e_io == idx, prefix, 0.0), axis=0, keepdims=True) for idx in ids]
    rank_ref[...] = jnp.concatenate(ranks, axis=0).astype(jnp.int32)
    carry_ref[...] += jnp.sum(onehot, axis=1, keepdims=True)
    counts_ref[...] = carry_ref[...].astype(jnp.int32)


def _router(x, w_router, b_router):
    n = x.shape[0]
    tm = _row_tile(n, ROUTER_TILE)
    slot = pl.BlockSpec((TOP_K, tm), lambda i: (0, i))
    per_expert = pl.BlockSpec((N_EXPERTS, LANES), lambda i: (0, 0))
    return pl.pallas_call(
        _router_kernel,
        out_shape=(jax.ShapeDtypeStruct((TOP_K, n), jnp.int32),
                   jax.ShapeDtypeStruct((TOP_K, n), F32),
                   jax.ShapeDtypeStruct((TOP_K, n), jnp.int32),
                   jax.ShapeDtypeStruct((N_EXPERTS, LANES), jnp.int32)),
        grid=(n // tm,),
        in_specs=[pl.BlockSpec((tm, D_MODEL), lambda i: (i, 0)),
                  pl.BlockSpec((N_EXPERTS, D_MODEL), lambda i: (0, 0)),
                  per_expert],
        out_specs=(slot, slot, slot, per_expert),
        scratch_shapes=[pltpu.VMEM((N_EXPERTS, LANES), F32)],
        compiler_params=_params("arbitrary"),
        name="moe_router",
    )(x, w_router.T, jnp.broadcast_to(b_router[:, None], (N_EXPERTS, LANES)))


def _dispatch_kernel(dest_ref, x_hbm, xg_in_hbm, xg_hbm, sem):
    del xg_in_hbm
    i = pl.program_id(0)
    tm = dest_ref.shape[1]

    def row_copy(r, k):
        return pltpu.make_async_copy(x_hbm.at[pl.ds(i * tm + r, 1)],
                                     xg_hbm.at[pl.ds(dest_ref[k, r], 1)], sem)

    def issue(r, c):
        for k in range(TOP_K):
            row_copy(r, k).start()
        return c

    def drain(r, c):
        for k in range(TOP_K):
            row_copy(r, k).wait()
        return c

    lax.fori_loop(0, tm, issue, 0, unroll=8)
    lax.fori_loop(0, tm, drain, 0, unroll=8)


def _dispatch(x, dest, n_rows):
    n = x.shape[0]
    tm = _row_tile(n, DISPATCH_TILE)
    return pl.pallas_call(
        _dispatch_kernel,
        out_shape=jax.ShapeDtypeStruct((n_rows, D_MODEL), F32),
        grid=(n // tm,),
        in_specs=[pl.BlockSpec((TOP_K, tm), lambda i: (0, i), memory_space=pltpu.SMEM),
                  pl.BlockSpec(memory_space=pl.ANY),
                  pl.BlockSpec(memory_space=pl.ANY)],
        out_specs=pl.BlockSpec(memory_space=pl.ANY),
        scratch_shapes=[pltpu.SemaphoreType.DMA],
        input_output_aliases={2: 0},
        compiler_params=_params("arbitrary"),
        name="moe_dispatch",
    )(dest, x, jnp.zeros((n_rows, D_MODEL), F32))


def _expert_kernel(te_ref, tv_ref, xg_ref, wu_ref, bu_ref, wd_ref, bd_ref, y_ref, wu_bf_ref, wd_bf_ref):
    g = pl.program_id(0)
    e = te_ref[g]
    prev = te_ref[jnp.maximum(g - 1, 0)]

    @pl.when((g == 0) | (e != prev))
    def _():
        wu_bf_ref[...] = wu_ref[...].astype(BF16)
        wd_bf_ref[...] = wd_ref[...].astype(BF16)

    @pl.when(tv_ref[g] > 0)
    def _():
        gu = jnp.dot(xg_ref[...].astype(BF16), wu_bf_ref[...], preferred_element_type=F32) + bu_ref[...]
        gate = jnp.minimum(gu[:, :D_EXPERT], SWIGLU_LIMIT)
        up = jnp.clip(gu[:, D_EXPERT:], -SWIGLU_LIMIT, SWIGLU_LIMIT)
        glu = gate * jax.nn.sigmoid(SWIGLU_ALPHA * gate)
        y_ref[...] = jnp.dot(((up + 1.0) * glu).astype(BF16), wd_bf_ref[...],
                             preferred_element_type=F32) + bd_ref[...]

    @pl.when(tv_ref[g] == 0)
    def _():
        y_ref[...] = jnp.zeros_like(y_ref)


def _experts(tile_e, tile_valid, xg, w_up, b_up, w_down, b_down):
    n_rows = xg.shape[0]
    tm = EXPERT_TILE_ROWS
    grid_spec = pltpu.PrefetchScalarGridSpec(
        num_scalar_prefetch=2,
        grid=(n_rows // tm,),
        in_specs=[pl.BlockSpec((tm, D_MODEL), lambda g, te, tv: (g, 0)),
                  pl.BlockSpec((None, D_MODEL, 2 * D_EXPERT), lambda g, te, tv: (te[g], 0, 0)),
                  pl.BlockSpec((None, 1, 2 * D_EXPERT), lambda g, te, tv: (te[g], 0, 0)),
                  pl.BlockSpec((None, D_EXPERT, D_MODEL), lambda g, te, tv: (te[g], 0, 0)),
                  pl.BlockSpec((None, 1, D_MODEL), lambda g, te, tv: (te[g], 0, 0))],
        out_specs=pl.BlockSpec((tm, D_MODEL), lambda g, te, tv: (g, 0)),
        scratch_shapes=[pltpu.VMEM((D_MODEL, 2 * D_EXPERT), BF16),
                        pltpu.VMEM((D_EXPERT, D_MODEL), BF16)])
    return pl.pallas_call(
        _expert_kernel,
        out_shape=jax.ShapeDtypeStruct((n_rows, D_MODEL), F32),
        grid_spec=grid_spec,
        compiler_params=_params("arbitrary"),
        name="moe_experts",
    )(tile_e, tile_valid, xg, w_up, b_up.reshape(N_EXPERTS, 1, 2 * D_EXPERT),
      w_down, b_down.reshape(N_EXPERTS, 1, D_MODEL))


def _combine_ln_kernel(dest_ref, dest_next_ref, x_ref, gate_ref, g_ref, beta_ref, yg_hbm, o_ref, buf_ref, sem):
    i = pl.program_id(0)
    n_steps = pl.num_programs(0)
    tm = x_ref.shape[0]

    def row_copy(d_ref, slot, r, k):
        return pltpu.make_async_copy(yg_hbm.at[pl.ds(d_ref[k, r], 1)],
                                     buf_ref.at[slot, k, pl.ds(r, 1)], sem.at[slot])

    def issue(d_ref, slot):
        def body(r, c):
            for k in range(TOP_K):
                row_copy(d_ref, slot, r, k).start()
            return c
        lax.fori_loop(0, tm, body, 0, unroll=8)

    @pl.when(i == 0)
    def _():
        issue(dest_ref, 0)

    @pl.when(i + 1 < n_steps)
    def _():
        issue(dest_next_ref, (i + 1) % 2)

    slot = i % 2

    def drain(r, c):
        for k in range(TOP_K):
            row_copy(dest_ref, slot, r, k).wait()
        return c

    lax.fori_loop(0, tm, drain, 0, unroll=8)
    gates = gate_ref[...]
    y = gates[:, 0:1] * buf_ref[slot, 0]
    for k in range(1, TOP_K):
        y = y + gates[:, k:k + 1] * buf_ref[slot, k]
    o_ref[...] = _layer_norm(DEEPNORM_ALPHA * x_ref[...] + y, g_ref[...], beta_ref[...])


def _combine_ln(x, yg, dest, gates_t, g, beta):
    n = x.shape[0]
    tm = _row_tile(n, COMBINE_TILE)
    n_steps = n // tm
    row = pl.BlockSpec((tm, D_MODEL), lambda i: (i, 0))
    vec = pl.BlockSpec((1, D_MODEL), lambda i: (0, 0))
    return pl.pallas_call(
        _combine_ln_kernel,
        out_shape=jax.ShapeDtypeStruct((n, D_MODEL), F32),
        grid=(n_steps,),
        in_specs=[pl.BlockSpec((TOP_K, tm), lambda i: (0, i), memory_space=pltpu.SMEM),
                  pl.BlockSpec((TOP_K, tm), lambda i: (0, jnp.minimum(i + 1, n_steps - 1)),
                               memory_space=pltpu.SMEM),
                  row, pl.BlockSpec((tm, TOP_K), lambda i: (i, 0)), vec, vec,
                  pl.BlockSpec(memory_space=pl.ANY)],
        out_specs=row,
        scratch_shapes=[pltpu.VMEM((2, TOP_K, tm, D_MODEL), F32), pltpu.SemaphoreType.DMA((2,))],
        compiler_params=_params("arbitrary"),
        name="moe_combine_ln",
    )(dest, dest, x, gates_t, g.reshape(1, D_MODEL), beta.reshape(1, D_MODEL), yg)


def _moe_ln(x, w_router, b_router, w_up, b_up, w_down, b_down, g, beta):
    n = x.shape[0]
    tm = EXPERT_TILE_ROWS
    ids, gates, rank, counts = _router(x, w_router, b_router)
    counts = counts[:, 0]
    padded = (counts + tm - 1) // tm * tm
    pend = jnp.cumsum(padded)
    pstart = pend - padded
    dest = pstart[ids] + rank
    n_tiles = -(-(n * TOP_K) // tm) + N_EXPERTS
    tile_row0 = jnp.arange(n_tiles, dtype=jnp.int32) * tm
    tile_e = jnp.sum((pend[None, :] <= tile_row0[:, None]).astype(jnp.int32), axis=1)
    tile_e = jnp.minimum(tile_e, N_EXPERTS - 1)
    tile_valid = (tile_row0 < pend[-1]).astype(jnp.int32)
    xg = _dispatch(x, dest, n_tiles * tm)
    yg = _experts(tile_e, tile_valid, xg, w_up, b_up, w_down, b_down)
    return _combine_ln(x, yg, dest, gates.T, g, beta)


def kernel(x_prompt, x_sample, cache_k, cache_v, state_conv, state_h, page_table, ln_g, ln_b, w_qkv, w_o, w_in, b_in, conv_w, conv_b, w_gate_a, b_gate_a, w_gate_x, b_gate_x, lru_lambda, w_out, b_out, w_router, b_router, w_up, b_up, w_down, b_down):
    bp, tp, d = x_prompt.shape
    db, ts, _ = x_sample.shape
    n_p, n_s = bp * tp, db * ts
    xp = x_prompt.reshape(n_p, d)
    xs = x_sample.reshape(n_s, d)
    slopes = 2.0 ** (-8.0 * jnp.arange(1, N_HEADS + 1, dtype=F32) / N_HEADS)
    zeros_d = jnp.zeros((d,), F32)

    def moe_block(xp, xs, layer):
        x = jnp.concatenate([xp, xs], axis=0)
        x = _moe_ln(x, w_router[layer], b_router[layer], w_up[layer], b_up[layer], w_down[layer], b_down[layer],
                    ln_g[layer, 1], ln_b[layer, 1])
        return x[:n_p], x[n_p:]

    w_qkv_bf = w_qkv[0].astype(BF16)
    w_o_bf = w_o[0].astype(BF16)
    wkt_bf = w_qkv[0][:, d:2 * d].T.astype(BF16)
    wvt_bf = w_qkv[0][:, 2 * d:].T.astype(BF16)
    qp, ktp, vtp = _qkv_prompt(xp, w_qkv_bf[:, :d], wkt_bf, wvt_bf, bp, tp)
    qs, ks, vs = _qkv(xs, w_qkv_bf)
    op = _moba_prompt(qp, ktp, vtp, slopes, bp, tp)
    n_phys = cache_k.shape[1]
    kt_pool = jnp.transpose(cache_k[0], (0, 2, 3, 1)).reshape(n_phys, d, PAGE_SIZE)
    vt_pool = jnp.transpose(cache_v[0], (0, 2, 3, 1)).reshape(n_phys, d, PAGE_SIZE)
    os_ = _moba_sample(qs, ks, vs, kt_pool, vt_pool, page_table, slopes, db, ts)
    xp = _linear_res_ln(op, w_o_bf, zeros_d, xp, ln_g[0, 0], ln_b[0, 0])
    xs = _linear_res_ln(os_, w_o_bf, zeros_d, xs, ln_g[0, 0], ln_b[0, 0])
    xp, xs = moe_block(xp, xs, 0)

    w_in_bf = w_in[0].astype(BF16)
    wa_bf = w_gate_a[0].astype(BF16)
    wx_bf = w_gate_x[0].astype(BF16)
    rec = (conv_w[0], conv_b[0], wa_bf, b_gate_a[0].reshape(-1), wx_bf, b_gate_x[0].reshape(-1), lru_lambda[0])
    up = _linear(xp, w_in_bf, b_in[0])
    us = _linear(xs, w_in_bf, b_in[0])
    conv0 = jnp.zeros((bp, CONV_W - 1, D_RNN), F32)
    h0 = jnp.zeros((bp, D_RNN), F32)
    hyp, conv_p, h_p = _rglru(up, 0, conv0, h0, *rec, bp, tp)
    hys, conv_s, h_s = _rglru(us, 0, state_conv[0], state_h[0], *rec, db, ts)
    w_out_bf = w_out[0].astype(BF16)
    xp = _linear_res_ln(hyp, w_out_bf, b_out[0], xp, ln_g[1, 0], ln_b[1, 0])
    xs = _linear_res_ln(hys, w_out_bf, b_out[0], xs, ln_g[1, 0], ln_b[1, 0])
    xp, xs = moe_block(xp, xs, 1)

    def kv_rows(a_t):
        return jnp.transpose(a_t.reshape(1, bp, N_HEADS, HEAD_DIM, tp), (0, 1, 4, 2, 3))

    kv_s = (1, db, ts, N_HEADS, HEAD_DIM)
    return (xp.reshape(bp, tp, d), xs.reshape(db, ts, d),
            kv_rows(ktp), kv_rows(vtp), conv_p[None], h_p.reshape(1, bp, D_RNN),
            ks.reshape(kv_s), vs.reshape(kv_s), conv_s[None], h_s.reshape(1, db, D_RNN))
```

```python
import functools

import jax
import jax.numpy as jnp
from jax import lax
from jax.experimental import pallas as pl
from jax.experimental.pallas import tpu as pltpu

F32 = jnp.float32
BF16 = jnp.bfloat16

D_MODEL = 1024
N_HEADS = 16
HEAD_DIM = D_MODEL // N_HEADS
MOBA_BLOCK = 256
MOBA_TOPK = 3
PAGE_SIZE = 128
PAGES_PER_BLOCK = MOBA_BLOCK // PAGE_SIZE
D_RNN = D_MODEL
LRU_BLOCK_WIDTH = 256
N_LRU_BLOCKS = D_RNN // LRU_BLOCK_WIDTH
CONV_W = 4
LRU_C = 8.0
N_EXPERTS = 32
TOP_K = 4
D_EXPERT = D_MODEL
SWIGLU_LIMIT = 7.0
SWIGLU_ALPHA = 1.702
DEPTH = 2
DEEPNORM_ALPHA = (2 * DEPTH) ** 0.25
LN_EPS = 1e-5
NEG_INF = -1e30

V7X_VMEM_LIMIT_BYTES = 56 * 1024 * 1024
SUBLANES = 8
LANES = 128
CONV_HALO = SUBLANES
EXPERT_TILE_ROWS = 512
ROUTER_TILE = 256
SAMPLE_BLOCKS_PER_STEP = 4
DISPATCH_TILE = 640
COMBINE_TILE = 256


def _params(*sem):
    return pltpu.CompilerParams(dimension_semantics=sem, vmem_limit_bytes=V7X_VMEM_LIMIT_BYTES)


def _row_tile(m, pref):
    t = min(pref, m)
    while m % t or t % SUBLANES:
        t -= SUBLANES
    return t


def _div_pow2(x, n):
    assert n & (n - 1) == 0, n
    return lax.shift_right_logical(x, n.bit_length() - 1)


def _layer_norm(y, g, b):
    mu = jnp.mean(y, axis=-1, keepdims=True)
    yc = y - mu
    var = jnp.mean(yc * yc, axis=-1, keepdims=True)
    return yc * lax.rsqrt(var + LN_EPS) * g + b


def _qkv_kernel(x_ref, w_ref, q_ref, k_ref, v_ref):
    x = x_ref[...].astype(BF16)
    for c, o_ref in enumerate((q_ref, k_ref, v_ref)):
        o_ref[...] = jnp.dot(x, w_ref[:, c * D_MODEL:(c + 1) * D_MODEL], preferred_element_type=F32)


def _qkv(x, w_bf):
    m = x.shape[0]
    tm = _row_tile(m, 512)
    out = jax.ShapeDtypeStruct((m, D_MODEL), F32)
    row = pl.BlockSpec((tm, D_MODEL), lambda i: (i, 0))
    return pl.pallas_call(
        _qkv_kernel,
        out_shape=(out, out, out),
        grid=(m // tm,),
        in_specs=[row, pl.BlockSpec((D_MODEL, 3 * D_MODEL), lambda i: (0, 0))],
        out_specs=(row, row, row),
        compiler_params=_params("arbitrary"),
        name="qkv_proj",
    )(x, w_bf)


def _qkv_prompt_kernel(x_ref, wq_ref, wkt_ref, wvt_ref, q_ref, kt_ref, vt_ref):
    x = x_ref[...].astype(BF16)
    nt = (((1,), (1,)), ((), ()))
    q_ref[...] = jnp.dot(x, wq_ref[...], preferred_element_type=F32)
    kt_ref[...] = lax.dot_general(wkt_ref[...], x, nt, preferred_element_type=F32)
    vt_ref[...] = lax.dot_general(wvt_ref[...], x, nt, preferred_element_type=F32)


def _qkv_prompt(x, wq_bf, wkt_bf, wvt_bf, b, t):
    tm = _row_tile(t, 512)
    nt = t // tm
    w = pl.BlockSpec((D_MODEL, D_MODEL), lambda bi, i: (0, 0))
    row = pl.BlockSpec((tm, D_MODEL), lambda bi, i: (bi * nt + i, 0))
    col = pl.BlockSpec((None, D_MODEL, tm), lambda bi, i: (bi, 0, i))
    t_out = jax.ShapeDtypeStruct((b, D_MODEL, t), F32)
    return pl.pallas_call(
        _qkv_prompt_kernel,
        out_shape=(jax.ShapeDtypeStruct((b * t, D_MODEL), F32), t_out, t_out),
        grid=(b, nt),
        in_specs=[row, w, w, w],
        out_specs=(row, col, col),
        compiler_params=_params("arbitrary", "arbitrary"),
        name="qkv_proj_prompt",
    )(x, wq_bf, wkt_bf, wvt_bf)


def _linear_kernel(x_ref, w_ref, b_ref, o_ref):
    x = x_ref[...].astype(BF16)
    o_ref[...] = jnp.dot(x, w_ref[...], preferred_element_type=F32) + b_ref[...]


def _linear(x, w_bf, b):
    m, k = x.shape
    n = w_bf.shape[1]
    tm = _row_tile(m, 512)
    tn = min(n, 1024)
    return pl.pallas_call(
        _linear_kernel,
        out_shape=jax.ShapeDtypeStruct((m, n), F32),
        grid=(m // tm, n // tn),
        in_specs=[pl.BlockSpec((tm, k), lambda i, j: (i, 0)),
                  pl.BlockSpec((k, tn), lambda i, j: (0, j)),
                  pl.BlockSpec((1, tn), lambda i, j: (0, j))],
        out_specs=pl.BlockSpec((tm, tn), lambda i, j: (i, j)),
        compiler_params=_params("arbitrary", "arbitrary"),
        name="linear",
    )(x, w_bf, b.reshape(1, n))


def _linear_res_ln_kernel(a_ref, w_ref, b_ref, res_ref, g_ref, beta_ref, o_ref):
    m = jnp.dot(a_ref[...].astype(BF16), w_ref[...], preferred_element_type=F32) + b_ref[...]
    o_ref[...] = _layer_norm(DEEPNORM_ALPHA * res_ref[...] + m, g_ref[...], beta_ref[...])


def _linear_res_ln(a, w_bf, b, res, g, beta):
    m, k = a.shape
    n = w_bf.shape[1]
    tm = _row_tile(m, 512)
    row_in = pl.BlockSpec((tm, k), lambda i: (i, 0))
    row = pl.BlockSpec((tm, n), lambda i: (i, 0))
    vec = pl.BlockSpec((1, n), lambda i: (0, 0))
    return pl.pallas_call(
        _linear_res_ln_kernel,
        out_shape=jax.ShapeDtypeStruct((m, n), F32),
        grid=(m // tm,),
        in_specs=[row_in, pl.BlockSpec((k, n), lambda i: (0, 0)), vec, row, vec, vec],
        out_specs=row,
        compiler_params=_params("arbitrary"),
        name="linear_res_ln",
    )(a, w_bf, b.reshape(1, n), res, g.reshape(1, n), beta.reshape(1, n))


def _slot_rank(gm, n_blocks, axis):
    blk = lax.broadcasted_iota(jnp.int32, gm.shape, axis)
    rank = jnp.zeros(gm.shape, jnp.int32)
    for m in range(n_blocks):
        g_m = jnp.sum(jnp.where(blk == m, gm, 0.0), axis=axis, keepdims=True)
        beats = (g_m > gm) | ((g_m == gm) & (m < blk))
        rank = rank + jnp.where(beats, 1, 0)
    return rank


def _moba_prompt_kernel(slope_ref, q_ref, kt_ref, vt_ref, o_ref, *, t):
    nb = t // MOBA_BLOCK
    n_sel = min(MOBA_TOPK, nb)
    bs = MOBA_BLOCK
    hp = pl.program_id(1)
    heads_per_step = LANES // HEAD_DIM
    assert 2 * nb <= HEAD_DIM

    rc = (lax.broadcasted_iota(jnp.int32, (bs, t), 0)
          - lax.broadcasted_iota(jnp.int32, (bs, t), 1)).astype(F32)
    causal_bias = jnp.where(lax.broadcasted_iota(jnp.int32, (bs, bs), 0)
                            >= lax.broadcasted_iota(jnp.int32, (bs, bs), 1), 0.0, NEG_INF)

    blk = lax.broadcasted_iota(jnp.int32, (nb, t), 0)
    own = _div_pow2(lax.broadcasted_iota(jnp.int32, (nb, t), 1), MOBA_BLOCK)
    pad_rows = jnp.zeros((LANES - 2 * nb, t), F32)
    block_mean = jnp.where(blk == own, 1.0 / bs, 0.0).astype(BF16)
    blk64 = lax.broadcasted_iota(jnp.int32, (HEAD_DIM, t), 0)
    own64 = _div_pow2(lax.broadcasted_iota(jnp.int32, (HEAD_DIM, t), 1), MOBA_BLOCK)
    block_flag = jnp.where(blk64 == own64, 1.0, 0.0).astype(BF16)
    nt = (((1,), (1,)), ((), ()))

    outs = []
    for j in range(heads_per_step):
        slope = slope_ref[hp * heads_per_step + j]
        lo, hi = j * HEAD_DIM, (j + 1) * HEAD_DIM
        qh = q_ref[:, lo:hi]
        kt = kt_ref[lo:hi, :]
        vt = vt_ref[lo:hi, :]

        kb = kt.astype(BF16)
        k_lo = (kt - kb.astype(F32)).astype(BF16)
        kmean = (lax.dot_general(block_mean, kb, nt, preferred_element_type=F32)
                 + lax.dot_general(block_mean, k_lo, nt, preferred_element_type=F32))
        g_t = lax.dot_general(kmean.astype(BF16), qh.astype(BF16), nt, preferred_element_type=F32)
        gm = jnp.where(blk < own, g_t, NEG_INF)
        rank = _slot_rank(gm, nb, 0)
        chosen = (rank < n_sel) & (rank < own)
        sel_bias_t = jnp.where(chosen | (blk >= own), 0.0, NEG_INF)
        own_mult_t = jnp.where(chosen, 2.0, 1.0)
        sel = jnp.transpose(jnp.concatenate([sel_bias_t, own_mult_t, pad_rows], axis=0))

        q_aug = jnp.concatenate([(qh * (HEAD_DIM ** -0.5)).astype(BF16),
                                 sel[:, :HEAD_DIM].astype(BF16)], axis=1)
        k_aug = jnp.concatenate([kb, block_flag], axis=0)
        vb = vt.astype(BF16)
        alibi = slope * rc
        o_blocks = []
        for i in range(nb):
            w = (i + 1) * bs
            rows = slice(i * bs, (i + 1) * bs)
            s = jnp.dot(q_aug[rows], k_aug[:, :w], preferred_element_type=F32) - alibi[:, :w]
            s_own = s[:, i * bs:] + causal_bias
            s = jnp.concatenate([s[:, :i * bs], s_own], axis=1) if i else s_own
            m = jnp.max(s, axis=1, keepdims=True)
            p = jnp.exp(s - m)
            p_own = p[:, i * bs:] * sel[rows, nb + i:nb + i + 1]
            p = jnp.concatenate([p[:, :i * bs], p_own], axis=1) if i else p_own
            p = p * (1.0 / jnp.sum(p, axis=1, keepdims=True))
            o_blocks.append(lax.dot_general(p.astype(BF16), vb[:, :w], nt, preferred_element_type=F32))
        outs.append(jnp.concatenate(o_blocks, axis=0))
    o_ref[...] = jnp.concatenate(outs, axis=1)


def _moba_prompt(q, kt, vt, slopes, b, t):
    blk = pl.BlockSpec((t, LANES), lambda bi, hp: (bi, hp))
    blk_t = pl.BlockSpec((None, LANES, t), lambda bi, hp: (bi, hp, 0))
    return pl.pallas_call(
        functools.partial(_moba_prompt_kernel, t=t),
        out_shape=jax.ShapeDtypeStruct((b * t, D_MODEL), F32),
        grid=(b, D_MODEL // LANES),
        in_specs=[pl.BlockSpec(memory_space=pltpu.SMEM), blk, blk_t, blk_t],
        out_specs=blk,
        compiler_params=_params("arbitrary", "arbitrary"),
        name="moba_prompt",
    )(slopes, q, kt, vt)


def _moba_sample_kernel(pt_ref, slope_ref, q_ref, kn_ref, vn_ref, *refs, t, nbp, bps):
    del pt_ref
    n_pg = bps * PAGES_PER_BLOCK
    k_refs, v_refs = refs[:n_pg], refs[n_pg:2 * n_pg]
    o_ref, qbd_ref, kmean_ref, s_ref, pown_ref, l_ref, acc_ref = refs[2 * n_pg:]
    k_steps = nbp // bps
    step = pl.program_id(1)
    bs = MOBA_BLOCK
    hq = N_HEADS * t
    past_len = nbp * bs
    own_blk = nbp
    nb = nbp + 1
    n_sel = min(MOBA_TOPK, nb)

    row = lax.broadcasted_iota(jnp.int32, (hq, bs), 0)
    lane = lax.broadcasted_iota(jnp.int32, (hq, bs), 1)
    assert t & (t - 1) == 0, t
    qpos = past_len + (row & (t - 1))
    slope = slope_ref[...]

    def head_diag(x):
        r = _div_pow2(lax.broadcasted_iota(jnp.int32, x.shape, 0), t)
        c = _div_pow2(lax.broadcasted_iota(jnp.int32, x.shape, 1), HEAD_DIM)
        return jnp.where(r == c, x, 0.0)

    @pl.when(step == 0)
    def _():
        q_rep = jnp.concatenate([q_ref[...]] * N_HEADS, axis=0)
        qbd_ref[...] = head_diag(q_rep)
        kmean_ref[...] = jnp.zeros_like(kmean_ref)
        acc_ref[...] = jnp.zeros_like(acc_ref)

    @pl.when(step < k_steps)
    def _():
        qbd = (qbd_ref[...] * (HEAD_DIM ** -0.5)).astype(BF16)
        blk_lane = lax.broadcasted_iota(jnp.int32, kmean_ref.shape, 1)
        kmean_new = kmean_ref[...]
        for jb in range(bps):
            n = step * bps + jb
            ka, kb = k_refs[2 * jb][...], k_refs[2 * jb + 1][...]
            ksum = jnp.sum(ka + kb, axis=1, keepdims=True) * (1.0 / bs)
            kmean_new = kmean_new + jnp.where(blk_lane == n, ksum, 0.0)
            s = jnp.concatenate([jnp.dot(qbd, ka.astype(BF16), preferred_element_type=F32),
                                 jnp.dot(qbd, kb.astype(BF16), preferred_element_type=F32)], axis=1)
            dist = qpos - (n * bs + lane)
            s_ref[n] = s - slope * dist.astype(F32)
        kmean_ref[...] = kmean_new

    @pl.when(step == k_steps - 1)
    def _():
        qbd = qbd_ref[...]
        g = jnp.dot(qbd.astype(BF16), kmean_ref[...].astype(BF16), preferred_element_type=F32)
        col = lax.broadcasted_iota(jnp.int32, g.shape, 1)
        gm = jnp.where(col < own_blk, g, NEG_INF)
        rank = _slot_rank(gm, nb, 1)
        chosen = (rank < n_sel) & (rank < own_blk) & (col < nb)

        kn = jnp.concatenate([kn_ref[...], jnp.zeros((LANES - t, D_MODEL), F32)], axis=0)
        s_own = lax.dot_general((qbd * (HEAD_DIM ** -0.5)).astype(BF16), kn.astype(BF16),
                                (((1,), (1,)), ((), ())), preferred_element_type=F32)
        dist_own = (qpos - (past_len + lane))[:, :LANES]
        s_own = s_own - slope[:, :LANES] * dist_own.astype(F32)
        s_own = jnp.where(dist_own >= 0, s_own, NEG_INF)

        def sel_col(n):
            return jnp.sum(jnp.where((col == n) & chosen, 1.0, 0.0), axis=1, keepdims=True)

        def max_body(n, m):
            sn = jnp.where(sel_col(n) > 0.0, s_ref[n], NEG_INF)
            return jnp.maximum(m, jnp.max(sn, axis=1, keepdims=True))

        m = lax.fori_loop(0, nbp, max_body, jnp.max(s_own, axis=1, keepdims=True))

        def e_body(n, l):
            e = jnp.where(sel_col(n) > 0.0, jnp.exp(s_ref[n] - m), 0.0)
            s_ref[n] = e
            return l + jnp.sum(e, axis=1, keepdims=True)

        own_mult = 1.0 + sel_col(own_blk)
        e_own = jnp.exp(s_own - m) * own_mult
        l = lax.fori_loop(0, nbp, e_body, jnp.sum(e_own, axis=1, keepdims=True))
        pown_ref[...] = (e_own / l).astype(BF16)
        l_ref[...] = jnp.broadcast_to(l, l_ref.shape)

    @pl.when(step >= k_steps)
    def _():
        nt = (((1,), (1,)), ((), ()))
        acc = acc_ref[...]
        for jb in range(bps):
            p = (s_ref[(step - k_steps) * bps + jb] / l_ref[:, :1]).astype(BF16)
            for half in range(PAGES_PER_BLOCK):
                acc = acc + lax.dot_general(p[:, half * PAGE_SIZE:(half + 1) * PAGE_SIZE],
                                            v_refs[2 * jb + half][...].astype(BF16), nt,
                                            preferred_element_type=F32)
        acc_ref[...] = acc

    @pl.when(step == 2 * k_steps - 1)
    def _():
        vn = jnp.concatenate([vn_ref[...], jnp.zeros((LANES - t, D_MODEL), F32)], axis=0).astype(BF16)
        acc = acc_ref[...] + jnp.dot(pown_ref[...], vn, preferred_element_type=F32)
        acc = head_diag(acc)
        o_ref[...] = jnp.sum(acc.reshape(N_HEADS, t, D_MODEL), axis=0)


def _moba_sample(q, k_new, v_new, kt_pool, vt_pool, page_table, slopes, db, t):
    n_pages = page_table.shape[1]
    assert n_pages % PAGES_PER_BLOCK == 0 and PAGES_PER_BLOCK == 2 and t <= LANES
    nbp = n_pages // PAGES_PER_BLOCK
    bps = SAMPLE_BLOCKS_PER_STEP
    assert nbp % bps == 0
    k_steps = nbp // bps
    n_pg = bps * PAGES_PER_BLOCK
    hq = N_HEADS * t
    nb_pad = -(-(nbp + 1) // LANES) * LANES
    slope_rows = jnp.broadcast_to(jnp.repeat(slopes, t)[:, None], (hq, MOBA_BLOCK))
    new_rows = pl.BlockSpec((t, D_MODEL), lambda b, s, pt: (b, 0))

    def page(first, j):
        def index(b, s, pt):
            return (pt[b, jnp.clip(s - first, 0, k_steps - 1) * n_pg + j], 0, 0)
        return pl.BlockSpec((None, D_MODEL, PAGE_SIZE), index)

    grid_spec = pltpu.PrefetchScalarGridSpec(
        num_scalar_prefetch=1,
        grid=(db, 2 * k_steps),
        in_specs=[pl.BlockSpec((hq, MOBA_BLOCK), lambda b, s, pt: (0, 0)),
                  new_rows, new_rows, new_rows,
                  *[page(0, j) for j in range(n_pg)], *[page(k_steps, j) for j in range(n_pg)]],
        out_specs=new_rows,
        scratch_shapes=[pltpu.VMEM((hq, D_MODEL), F32),
                        pltpu.VMEM((D_MODEL, nb_pad), F32),
                        pltpu.VMEM((nbp, hq, MOBA_BLOCK), F32),
                        pltpu.VMEM((hq, LANES), BF16),
                        pltpu.VMEM((hq, LANES), F32),
                        pltpu.VMEM((hq, D_MODEL), F32)])
    return pl.pallas_call(
        functools.partial(_moba_sample_kernel, t=t, nbp=nbp, bps=bps),
        out_shape=jax.ShapeDtypeStruct((db * t, D_MODEL), F32),
        grid_spec=grid_spec,
        compiler_params=_params("arbitrary", "arbitrary"),
        name="moba_sample",
    )(page_table, slope_rows, q, k_new, v_new, *([kt_pool] * n_pg), *([vt_pool] * n_pg))


def _rglru_kernel(u_ref, conv0_ref, h0_ref, cw_ref, cb_ref, wa_ref, ba_ref, wx_ref, bx_ref, lam_ref,
                  hy_ref, conv_ref, hlast_ref, halo_ref, h_ref, a_ref, b_ref, hs_ref, *, tt):
    i = pl.program_id(1)
    keep = CONV_W - 1
    first = CONV_HALO - keep

    @pl.when(i == 0)
    def _():
        halo_ref[first:CONV_HALO, :] = conv0_ref[...]
        h_ref[...] = h0_ref[...]

    u = u_ref[...]
    y_in = u[:, :D_RNN]
    y_branch = 0.5 * y_in * (1.0 + jnp.tanh(0.7978845608028654 * (y_in + 0.044715 * (y_in * y_in * y_in))))
    halo_ref[CONV_HALO:CONV_HALO + tt, :] = u[:, D_RNN:]
    xc = cb_ref[...]
    acc = None
    for j in range(CONV_W):
        term = halo_ref[first + j:first + j + tt, :] * cw_ref[j:j + 1, :]
        acc = term if acc is None else acc + term
    xc = xc + acc

    r_parts, i_parts = [], []
    for blk in range(N_LRU_BLOCKS):
        xb = xc[:, blk * LRU_BLOCK_WIDTH:(blk + 1) * LRU_BLOCK_WIDTH].astype(BF16)
        r_parts.append(jnp.dot(xb, wa_ref[blk], preferred_element_type=F32))
        i_parts.append(jnp.dot(xb, wx_ref[blk], preferred_element_type=F32))
    r = jax.nn.sigmoid(jnp.concatenate(r_parts, axis=1) + ba_ref[...])
    ig = jax.nn.sigmoid(jnp.concatenate(i_parts, axis=1) + bx_ref[...])
    log_a = LRU_C * r * jax.nn.log_sigmoid(lam_ref[...])
    a_ref[...] = jnp.exp(log_a)
    b_ref[...] = jnp.sqrt(1.0 - jnp.exp(2.0 * log_a)) * (ig * xc)

    def step(s, h):
        h = a_ref[pl.ds(s, 1), :] * h + b_ref[pl.ds(s, 1), :]
        hs_ref[pl.ds(s, 1), :] = h
        return h

    h = lax.fori_loop(0, tt, step, h_ref[...], unroll=8)
    h_ref[...] = h
    hy_ref[...] = hs_ref[...] * y_branch
    tail = halo_ref[first + tt:CONV_HALO + tt, :]
    halo_ref[first:CONV_HALO, :] = tail

    @pl.when(i == pl.num_programs(1) - 1)
    def _():
        conv_ref[...] = tail
        hlast_ref[...] = h


def _rglru(u, row0, conv0, h0, cw, cb, wa_bf, ba, wx_bf, bx, lam, b, t):
    tt = _row_tile(t, 256)
    nt = t // tt
    blk0 = row0 // tt
    keep = CONV_W - 1
    vec = pl.BlockSpec((1, D_RNN), lambda bi, i: (0, 0))
    gate_w = pl.BlockSpec((N_LRU_BLOCKS, LRU_BLOCK_WIDTH, LRU_BLOCK_WIDTH), lambda bi, i: (0, 0, 0))
    state3 = pl.BlockSpec((None, keep, D_RNN), lambda bi, i: (bi, 0, 0))
    state1 = pl.BlockSpec((None, 1, D_RNN), lambda bi, i: (bi, 0, 0))
    return pl.pallas_call(
        functools.partial(_rglru_kernel, tt=tt),
        out_shape=(jax.ShapeDtypeStruct((b * t, D_RNN), F32),
                   jax.ShapeDtypeStruct((b, keep, D_RNN), F32),
                   jax.ShapeDtypeStruct((b, 1, D_RNN), F32)),
        grid=(b, nt),
        in_specs=[pl.BlockSpec((tt, 2 * D_RNN), lambda bi, i: (blk0 + bi * nt + i, 0)),
                  state3, state1,
                  pl.BlockSpec((CONV_W, D_RNN), lambda bi, i: (0, 0)), vec,
                  gate_w, vec, gate_w, vec, vec],
        out_specs=(pl.BlockSpec((tt, D_RNN), lambda bi, i: (bi * nt + i, 0)), state3, state1),
        scratch_shapes=[pltpu.VMEM((CONV_HALO + tt, D_RNN), F32),
                        pltpu.VMEM((1, D_RNN), F32),
                        pltpu.VMEM((tt, D_RNN), F32),
                        pltpu.VMEM((tt, D_RNN), F32),
                        pltpu.VMEM((tt, D_RNN), F32)],
        compiler_params=_params("arbitrary", "arbitrary"),
        name="rglru",
    )(u, conv0, h0.reshape(b, 1, D_RNN), cw, cb.reshape(1, D_RNN), wa_bf, ba.reshape(1, D_RNN),
      wx_bf, bx.reshape(1, D_RNN), lam.reshape(1, D_RNN))


def _router_kernel(x_ref, wt_ref, b_ref, ids_ref, gates_ref, rank_ref, counts_ref, carry_ref):
    i = pl.program_id(0)
    tm = x_ref.shape[0]

    @pl.when(i == 0)
    def _():
        carry_ref[...] = jnp.zeros_like(carry_ref)

    logits = lax.dot_general(wt_ref[...].astype(BF16), x_ref[...].astype(BF16), (((1,), (1,)), ((), ())),
                             preferred_element_type=F32) + b_ref[:, :1]
    e_io = lax.broadcasted_iota(jnp.int32, logits.shape, 0)
    lt = logits
    vals, ids = [], []
    for _ in range(TOP_K):
        mx = jnp.max(lt, axis=0, keepdims=True)
        idx = jnp.min(jnp.where(lt == mx, e_io, N_EXPERTS), axis=0, keepdims=True)
        vals.append(mx)
        ids.append(idx)
        lt = jnp.where(e_io == idx, -jnp.inf, lt)
    ex = [jnp.exp(v - vals[0]) for v in vals]
    den = ex[0] + ex[1] + ex[2] + ex[3]
    gates_ref[...] = jnp.concatenate([e / den for e in ex], axis=0)
    ids_ref[...] = jnp.concatenate(ids, axis=0)

    onehot = jnp.zeros(logits.shape, F32)
    for idx in ids:
        onehot = onehot + jnp.where(e_io == idx, 1.0, 0.0)
    s_io = lax.broadcasted_iota(jnp.int32, (tm, tm), 0)
    t_io = lax.broadcasted_iota(jnp.int32, (tm, tm), 1)
    before = jnp.where(s_io < t_io, 1.0, 0.0).astype(BF16)
    prefix = jnp.dot(onehot.astype(BF16), before, preferred_element_type=F32) + carry_ref[:, :1]
    ranks = [jnp.sum(jnp.where(e_io == idx, prefix, 0.0), axis=0, keepdims=True) for idx in ids]
    rank_ref[...] = jnp.concatenate(ranks, axis=0).astype(jnp.int32)
    carry_ref[...] += jnp.sum(onehot, axis=1, keepdims=True)
    counts_ref[...] = carry_ref[...].astype(jnp.int32)


def _router(x, w_router, b_router):
    n = x.shape[0]
    tm = _row_tile(n, ROUTER_TILE)
    slot = pl.BlockSpec((TOP_K, tm), lambda i: (0, i))
    per_expert = pl.BlockSpec((N_EXPERTS, LANES), lambda i: (0, 0))
    return pl.pallas_call(
        _router_kernel,
        out_shape=(jax.ShapeDtypeStruct((TOP_K, n), jnp.int32),
                   jax.ShapeDtypeStruct((TOP_K, n), F32),
                   jax.ShapeDtypeStruct((TOP_K, n), jnp.int32),
                   jax.ShapeDtypeStruct((N_EXPERTS, LANES), jnp.int32)),
        grid=(n // tm,),
        in_specs=[pl.BlockSpec((tm, D_MODEL), lambda i: (i, 0)),
                  pl.BlockSpec((N_EXPERTS, D_MODEL), lambda i: (0, 0)),
                  per_expert],
        out_specs=(slot, slot, slot, per_expert),
        scratch_shapes=[pltpu.VMEM((N_EXPERTS, LANES), F32)],
        compiler_params=_params("arbitrary"),
        name="moe_router",
    )(x, w_router.T, jnp.broadcast_to(b_router[:, None], (N_EXPERTS, LANES)))


def _dispatch_kernel(dest_ref, x_ref, xg_in_hbm, xg_hbm, sem):
    del xg_in_hbm
    tm = dest_ref.shape[1]

    def row_copy(r, k):
        return pltpu.make_async_copy(x_ref.at[pl.ds(r, 1)], xg_hbm.at[pl.ds(dest_ref[k, r], 1)], sem)

    def issue(r, c):
        for k in range(TOP_K):
            row_copy(r, k).start()
        return c

    def drain(r, c):
        for k in range(TOP_K):
            row_copy(r, k).wait()
        return c

    lax.fori_loop(0, tm, issue, 0, unroll=8)
    lax.fori_loop(0, tm, drain, 0, unroll=8)


def _dispatch(x, dest, n_rows):
    n = x.shape[0]
    tm = _row_tile(n, DISPATCH_TILE)
    return pl.pallas_call(
        _dispatch_kernel,
        out_shape=jax.ShapeDtypeStruct((n_rows, D_MODEL), F32),
        grid=(n // tm,),
        in_specs=[pl.BlockSpec((TOP_K, tm), lambda i: (0, i), memory_space=pltpu.SMEM),
                  pl.BlockSpec((tm, D_MODEL), lambda i: (i, 0)),
                  pl.BlockSpec(memory_space=pl.ANY)],
        out_specs=pl.BlockSpec(memory_space=pl.ANY),
        scratch_shapes=[pltpu.SemaphoreType.DMA],
        input_output_aliases={2: 0},
        compiler_params=_params("arbitrary"),
        name="moe_dispatch",
    )(dest, x, jnp.zeros((n_rows, D_MODEL), F32))


def _expert_kernel(te_ref, tv_ref, xg_ref, wu_ref, bu_ref, wd_ref, bd_ref, y_ref, wu_bf_ref, wd_bf_ref):
    g = pl.program_id(0)
    e = te_ref[g]
    prev = te_ref[jnp.maximum(g - 1, 0)]

    @pl.when((g == 0) | (e != prev))
    def _():
        wu_bf_ref[...] = wu_ref[...].astype(BF16)
        wd_bf_ref[...] = wd_ref[...].astype(BF16)

    @pl.when(tv_ref[g] > 0)
    def _():
        gu = jnp.dot(xg_ref[...].astype(BF16), wu_bf_ref[...], preferred_element_type=F32) + bu_ref[...]
        gate = jnp.minimum(gu[:, :D_EXPERT], SWIGLU_LIMIT)
        up = jnp.clip(gu[:, D_EXPERT:], -SWIGLU_LIMIT, SWIGLU_LIMIT)
        glu = gate * jax.nn.sigmoid(SWIGLU_ALPHA * gate)
        y_ref[...] = jnp.dot(((up + 1.0) * glu).astype(BF16), wd_bf_ref[...],
                             preferred_element_type=F32) + bd_ref[...]

    @pl.when(tv_ref[g] == 0)
    def _():
        y_ref[...] = jnp.zeros_like(y_ref)


def _experts(tile_e, tile_valid, xg, layer, w_up, b_up, w_down, b_down):
    n_rows = xg.shape[0]
    tm = EXPERT_TILE_ROWS
    n_layers = w_up.shape[0]
    grid_spec = pltpu.PrefetchScalarGridSpec(
        num_scalar_prefetch=2,
        grid=(n_rows // tm,),
        in_specs=[pl.BlockSpec((tm, D_MODEL), lambda g, te, tv: (g, 0)),
                  pl.BlockSpec((None, None, D_MODEL, 2 * D_EXPERT), lambda g, te, tv: (layer, te[g], 0, 0)),
                  pl.BlockSpec((None, None, 1, 2 * D_EXPERT), lambda g, te, tv: (layer, te[g], 0, 0)),
                  pl.BlockSpec((None, None, D_EXPERT, D_MODEL), lambda g, te, tv: (layer, te[g], 0, 0)),
                  pl.BlockSpec((None, None, 1, D_MODEL), lambda g, te, tv: (layer, te[g], 0, 0))],
        out_specs=pl.BlockSpec((tm, D_MODEL), lambda g, te, tv: (g, 0)),
        scratch_shapes=[pltpu.VMEM((D_MODEL, 2 * D_EXPERT), BF16),
                        pltpu.VMEM((D_EXPERT, D_MODEL), BF16)])
    return pl.pallas_call(
        _expert_kernel,
        out_shape=jax.ShapeDtypeStruct((n_rows, D_MODEL), F32),
        grid_spec=grid_spec,
        compiler_params=_params("arbitrary"),
        name="moe_experts",
    )(tile_e, tile_valid, xg, w_up, b_up.reshape(n_layers, N_EXPERTS, 1, 2 * D_EXPERT),
      w_down, b_down.reshape(n_layers, N_EXPERTS, 1, D_MODEL))


def _combine_ln_kernel(dest_ref, dest_next_ref, x_ref, gate_ref, g_ref, beta_ref, yg_hbm, o_ref, buf_ref, sem):
    i = pl.program_id(0)
    n_steps = pl.num_programs(0)
    tm = x_ref.shape[0]

    def row_copy(d_ref, slot, r, k):
        return pltpu.make_async_copy(yg_hbm.at[pl.ds(d_ref[k, r], 1)],
                                     buf_ref.at[slot, k, pl.ds(r, 1)], sem.at[slot])

    def issue(d_ref, slot):
        def body(r, c):
            for k in range(TOP_K):
                row_copy(d_ref, slot, r, k).start()
            return c
        lax.fori_loop(0, tm, body, 0, unroll=8)

    @pl.when(i == 0)
    def _():
        issue(dest_ref, 0)

    @pl.when(i + 1 < n_steps)
    def _():
        issue(dest_next_ref, (i + 1) % 2)

    slot = i % 2

    def drain(r, c):
        for k in range(TOP_K):
            row_copy(dest_ref, slot, r, k).wait()
        return c

    lax.fori_loop(0, tm, drain, 0, unroll=8)
    gates = gate_ref[...]
    y = gates[:, 0:1] * buf_ref[slot, 0]
    for k in range(1, TOP_K):
        y = y + gates[:, k:k + 1] * buf_ref[slot, k]
    o_ref[...] = _layer_norm(DEEPNORM_ALPHA * x_ref[...] + y, g_ref[...], beta_ref[...])


def _combine_ln(x, yg, dest, gates_t, g, beta):
    n = x.shape[0]
    tm = _row_tile(n, COMBINE_TILE)
    n_steps = n // tm
    row = pl.BlockSpec((tm, D_MODEL), lambda i: (i, 0))
    vec = pl.BlockSpec((1, D_MODEL), lambda i: (0, 0))
    return pl.pallas_call(
        _combine_ln_kernel,
        out_shape=jax.ShapeDtypeStruct((n, D_MODEL), F32),
        grid=(n_steps,),
        in_specs=[pl.BlockSpec((TOP_K, tm), lambda i: (0, i), memory_space=pltpu.SMEM),
                  pl.BlockSpec((TOP_K, tm), lambda i: (0, jnp.minimum(i + 1, n_steps - 1)),
                               memory_space=pltpu.SMEM),
                  row, pl.BlockSpec((tm, TOP_K), lambda i: (i, 0)), vec, vec,
                  pl.BlockSpec(memory_space=pl.ANY)],
        out_specs=row,
        scratch_shapes=[pltpu.VMEM((2, TOP_K, tm, D_MODEL), F32), pltpu.SemaphoreType.DMA((2,))],
        compiler_params=_params("arbitrary"),
        name="moe_combine_ln",
    )(dest, dest, x, gates_t, g.reshape(1, D_MODEL), beta.reshape(1, D_MODEL), yg)


def _moe_ln(x, layer, w_router, b_router, w_up, b_up, w_down, b_down, g, beta):
    n = x.shape[0]
    tm = EXPERT_TILE_ROWS
    ids, gates, rank, counts = _router(x, w_router, b_router)
    counts = counts[:, 0]
    padded = (counts + tm - 1) // tm * tm
    pend = jnp.cumsum(padded)
    pstart = pend - padded
    expert_io = jnp.arange(N_EXPERTS, dtype=jnp.int32)[:, None, None]
    dest = jnp.sum(jnp.where(ids[None] == expert_io, pstart[:, None, None], 0), axis=0) + rank
    n_tiles = -(-(n * TOP_K) // tm) + N_EXPERTS
    tile_row0 = jnp.arange(n_tiles, dtype=jnp.int32) * tm
    tile_e = jnp.sum((pend[None, :] <= tile_row0[:, None]).astype(jnp.int32), axis=1)
    tile_e = jnp.minimum(tile_e, N_EXPERTS - 1)
    tile_valid = (tile_row0 < pend[-1]).astype(jnp.int32)
    xg = _dispatch(x, dest, n_tiles * tm)
    yg = _experts(tile_e, tile_valid, xg, layer, w_up, b_up, w_down, b_down)
    return _combine_ln(x, yg, dest, gates.T, g, beta)


def kernel(x_prompt, x_sample, cache_k, cache_v, state_conv, state_h, page_table, ln_g, ln_b, w_qkv, w_o, w_in, b_in, conv_w, conv_b, w_gate_a, b_gate_a, w_gate_x, b_gate_x, lru_lambda, w_out, b_out, w_router, b_router, w_up, b_up, w_down, b_down):
    bp, tp, d = x_prompt.shape
    db, ts, _ = x_sample.shape
    n_p, n_s = bp * tp, db * ts
    xp = x_prompt.reshape(n_p, d)
    xs = x_sample.reshape(n_s, d)
    slopes = 2.0 ** (-8.0 * jnp.arange(1, N_HEADS + 1, dtype=F32) / N_HEADS)
    zeros_d = jnp.zeros((d,), F32)

    def moe_block(xp, xs, layer):
        x = jnp.concatenate([xp, xs], axis=0)
        x = _moe_ln(x, layer, w_router[layer], b_router[layer], w_up, b_up, w_down, b_down,
                    ln_g[layer, 1], ln_b[layer, 1])
        return x[:n_p], x[n_p:]

    w_qkv_bf = w_qkv[0].astype(BF16)
    w_o_bf = w_o[0].astype(BF16)
    wkt_bf = w_qkv[0][:, d:2 * d].T.astype(BF16)
    wvt_bf = w_qkv[0][:, 2 * d:].T.astype(BF16)
    qp, ktp, vtp = _qkv_prompt(xp, w_qkv_bf[:, :d], wkt_bf, wvt_bf, bp, tp)
    qs, ks, vs = _qkv(xs, w_qkv_bf)
    op = _moba_prompt(qp, ktp, vtp, slopes, bp, tp)
    n_phys = cache_k.shape[1]
    kt_pool = jnp.transpose(cache_k[0], (0, 2, 3, 1)).reshape(n_phys, d, PAGE_SIZE)
    vt_pool = jnp.transpose(cache_v[0], (0, 2, 3, 1)).reshape(n_phys, d, PAGE_SIZE)
    os_ = _moba_sample(qs, ks, vs, kt_pool, vt_pool, page_table, slopes, db, ts)
    xp = _linear_res_ln(op, w_o_bf, zeros_d, xp, ln_g[0, 0], ln_b[0, 0])
    xs = _linear_res_ln(os_, w_o_bf, zeros_d, xs, ln_g[0, 0], ln_b[0, 0])
    xp, xs = moe_block(xp, xs, 0)

    w_in_bf = w_in[0].astype(BF16)
    wa_bf = w_gate_a[0].astype(BF16)
    wx_bf = w_gate_x[0].astype(BF16)
    rec = (conv_w[0], conv_b[0], wa_bf, b_gate_a[0].reshape(-1), wx_bf, b_gate_x[0].reshape(-1), lru_lambda[0])
    up = _linear(xp, w_in_bf, b_in[0])
    us = _linear(xs, w_in_bf, b_in[0])
    conv0 = jnp.zeros((bp, CONV_W - 1, D_RNN), F32)
    h0 = jnp.zeros((bp, D_RNN), F32)
    hyp, conv_p, h_p = _rglru(up, 0, conv0, h0, *rec, bp, tp)
    hys, conv_s, h_s = _rglru(us, 0, state_conv[0], state_h[0], *rec, db, ts)
    w_out_bf = w_out[0].astype(BF16)
    xp = _linear_res_ln(hyp, w_out_bf, b_out[0], xp, ln_g[1, 0], ln_b[1, 0])
    xs = _linear_res_ln(hys, w_out_bf, b_out[0], xs, ln_g[1, 0], ln_b[1, 0])
    xp, xs = moe_block(xp, xs, 1)

    def kv_rows(a_t):
        return jnp.transpose(a_t.reshape(1, bp, N_HEADS, HEAD_DIM, tp), (0, 1, 4, 2, 3))

    kv_s = (1, db, ts, N_HEADS, HEAD_DIM)
    return (xp.reshape(bp, tp, d), xs.reshape(db, ts, d),
            kv_rows(ktp), kv_rows(vtp), conv_p[None], h_p.reshape(1, bp, D_RNN),
            ks.reshape(kv_s), vs.reshape(kv_s), conv_s[None], h_s.reshape(1, db, D_RNN))
```

```python
import functools

import jax
import jax.numpy as jnp
from jax import lax
from jax.experimental import pallas as pl
from jax.experimental.pallas import tpu as pltpu

F32 = jnp.float32
BF16 = jnp.bfloat16

D_MODEL = 1024
N_HEADS = 16
HEAD_DIM = D_MODEL // N_HEADS
MOBA_BLOCK = 256
MOBA_TOPK = 3
PAGE_SIZE = 128
PAGES_PER_BLOCK = MOBA_BLOCK // PAGE_SIZE
D_RNN = D_MODEL
LRU_BLOCK_WIDTH = 256
N_LRU_BLOCKS = D_RNN // LRU_BLOCK_WIDTH
CONV_W = 4
LRU_C = 8.0
N_EXPERTS = 32
TOP_K = 4
D_EXPERT = D_MODEL
SWIGLU_LIMIT = 7.0
SWIGLU_ALPHA = 1.702
DEPTH = 2
DEEPNORM_ALPHA = (2 * DEPTH) ** 0.25
LN_EPS = 1e-5
NEG_INF = -1e30

V7X_VMEM_LIMIT_BYTES = 56 * 1024 * 1024
SUBLANES = 8
LANES = 128
CONV_HALO = SUBLANES
EXPERT_TILE_ROWS = 512
ROUTER_TILE = 256
SAMPLE_BLOCKS_PER_STEP = 8
DISPATCH_TILE = 640
COMBINE_TILE = 256


def _params(*sem):
    return pltpu.CompilerParams(dimension_semantics=sem, vmem_limit_bytes=V7X_VMEM_LIMIT_BYTES)


def _row_tile(m, pref):
    t = min(pref, m)
    while m % t or t % SUBLANES:
        t -= SUBLANES
    return t


def _div_pow2(x, n):
    assert n & (n - 1) == 0, n
    return lax.shift_right_logical(x, n.bit_length() - 1)


def _layer_norm(y, g, b):
    mu = jnp.mean(y, axis=-1, keepdims=True)
    yc = y - mu
    var = jnp.mean(yc * yc, axis=-1, keepdims=True)
    return yc * lax.rsqrt(var + LN_EPS) * g + b


def _qkv_kernel(x_ref, w_ref, q_ref, k_ref, v_ref):
    x = x_ref[...].astype(BF16)
    for c, o_ref in enumerate((q_ref, k_ref, v_ref)):
        o_ref[...] = jnp.dot(x, w_ref[:, c * D_MODEL:(c + 1) * D_MODEL], preferred_element_type=F32)


def _qkv(x, w_bf):
    m = x.shape[0]
    tm = _row_tile(m, 512)
    out = jax.ShapeDtypeStruct((m, D_MODEL), F32)
    row = pl.BlockSpec((tm, D_MODEL), lambda i: (i, 0))
    return pl.pallas_call(
        _qkv_kernel,
        out_shape=(out, out, out),
        grid=(m // tm,),
        in_specs=[row, pl.BlockSpec((D_MODEL, 3 * D_MODEL), lambda i: (0, 0))],
        out_specs=(row, row, row),
        compiler_params=_params("arbitrary"),
        name="qkv_proj",
    )(x, w_bf)


def _qkv_prompt_kernel(x_ref, wq_ref, wkt_ref, wvt_ref, q_ref, kt_ref, vt_ref):
    x = x_ref[...].astype(BF16)
    nt = (((1,), (1,)), ((), ()))
    q_ref[...] = jnp.dot(x, wq_ref[...], preferred_element_type=F32)
    kt_ref[...] = lax.dot_general(wkt_ref[...], x, nt, preferred_element_type=F32)
    vt_ref[...] = lax.dot_general(wvt_ref[...], x, nt, preferred_element_type=F32)


def _qkv_prompt(x, wq_bf, wkt_bf, wvt_bf, b, t):
    tm = _row_tile(t, 512)
    nt = t // tm
    w = pl.BlockSpec((D_MODEL, D_MODEL), lambda bi, i: (0, 0))
    row = pl.BlockSpec((tm, D_MODEL), lambda bi, i: (bi * nt + i, 0))
    col = pl.BlockSpec((None, D_MODEL, tm), lambda bi, i: (bi, 0, i))
    t_out = jax.ShapeDtypeStruct((b, D_MODEL, t), F32)
    return pl.pallas_call(
        _qkv_prompt_kernel,
        out_shape=(jax.ShapeDtypeStruct((b * t, D_MODEL), F32), t_out, t_out),
        grid=(b, nt),
        in_specs=[row, w, w, w],
        out_specs=(row, col, col),
        compiler_params=_params("arbitrary", "arbitrary"),
        name="qkv_proj_prompt",
    )(x, wq_bf, wkt_bf, wvt_bf)


def _linear_kernel(x_ref, w_ref, b_ref, o_ref):
    x = x_ref[...].astype(BF16)
    o_ref[...] = jnp.dot(x, w_ref[...], preferred_element_type=F32) + b_ref[...]


def _linear(x, w_bf, b):
    m, k = x.shape
    n = w_bf.shape[1]
    tm = _row_tile(m, 512)
    tn = min(n, 1024)
    return pl.pallas_call(
        _linear_kernel,
        out_shape=jax.ShapeDtypeStruct((m, n), F32),
        grid=(m // tm, n // tn),
        in_specs=[pl.BlockSpec((tm, k), lambda i, j: (i, 0)),
                  pl.BlockSpec((k, tn), lambda i, j: (0, j)),
                  pl.BlockSpec((1, tn), lambda i, j: (0, j))],
        out_specs=pl.BlockSpec((tm, tn), lambda i, j: (i, j)),
        compiler_params=_params("arbitrary", "arbitrary"),
        name="linear",
    )(x, w_bf, b.reshape(1, n))


def _linear_res_ln_kernel(a_ref, w_ref, b_ref, res_ref, g_ref, beta_ref, *rest):
    o_ref = rest[-1]
    m = jnp.dot(a_ref[...].astype(BF16), w_ref[...], preferred_element_type=F32) + b_ref[...]
    o_ref[...] = _layer_norm(DEEPNORM_ALPHA * res_ref[...] + m, g_ref[...], beta_ref[...])


def _linear_res_ln(a, w_bf, b, res, g, beta, *, out_rows=None, row0=0, into=None):
    m, k = a.shape
    n = w_bf.shape[1]
    out_rows = m if out_rows is None else out_rows
    tm = _row_tile(m, 512)
    assert row0 % tm == 0
    blk0 = row0 // tm
    row_in = pl.BlockSpec((tm, k), lambda i: (i, 0))
    row = pl.BlockSpec((tm, n), lambda i: (i, 0))
    vec = pl.BlockSpec((1, n), lambda i: (0, 0))
    in_specs = [row_in, pl.BlockSpec((k, n), lambda i: (0, 0)), vec, row, vec, vec]
    args = [a, w_bf, b.reshape(1, n), res, g.reshape(1, n), beta.reshape(1, n)]
    aliases = {}
    if into is not None:
        in_specs.append(pl.BlockSpec(memory_space=pl.ANY))
        args.append(into)
        aliases = {len(args) - 1: 0}
    return pl.pallas_call(
        _linear_res_ln_kernel,
        out_shape=jax.ShapeDtypeStruct((out_rows, n), F32),
        grid=(m // tm,),
        in_specs=in_specs,
        out_specs=pl.BlockSpec((tm, n), lambda i: (blk0 + i, 0)),
        input_output_aliases=aliases,
        compiler_params=_params("arbitrary"),
        name="linear_res_ln",
    )(*args)


def _slot_rank(gm, n_blocks, axis):
    blk = lax.broadcasted_iota(jnp.int32, gm.shape, axis)
    rank = jnp.zeros(gm.shape, jnp.int32)
    for m in range(n_blocks):
        g_m = jnp.sum(jnp.where(blk == m, gm, 0.0), axis=axis, keepdims=True)
        beats = (g_m > gm) | ((g_m == gm) & (m < blk))
        rank = rank + jnp.where(beats, 1, 0)
    return rank


def _moba_prompt_kernel(slope_ref, q_ref, kt_ref, vt_ref, o_ref, *, t):
    nb = t // MOBA_BLOCK
    n_sel = min(MOBA_TOPK, nb)
    bs = MOBA_BLOCK
    hp = pl.program_id(1)
    heads_per_step = LANES // HEAD_DIM
    assert 2 * nb <= HEAD_DIM

    rc = (lax.broadcasted_iota(jnp.int32, (bs, t), 0)
          - lax.broadcasted_iota(jnp.int32, (bs, t), 1)).astype(F32)
    causal_bias = jnp.where(lax.broadcasted_iota(jnp.int32, (bs, bs), 0)
                            >= lax.broadcasted_iota(jnp.int32, (bs, bs), 1), 0.0, NEG_INF)

    blk = lax.broadcasted_iota(jnp.int32, (nb, t), 0)
    own = _div_pow2(lax.broadcasted_iota(jnp.int32, (nb, t), 1), MOBA_BLOCK)
    pad_rows = jnp.zeros((LANES - 2 * nb, t), F32)
    block_mean = jnp.where(blk == own, 1.0 / bs, 0.0).astype(BF16)
    blk64 = lax.broadcasted_iota(jnp.int32, (HEAD_DIM, t), 0)
    own64 = _div_pow2(lax.broadcasted_iota(jnp.int32, (HEAD_DIM, t), 1), MOBA_BLOCK)
    block_flag = jnp.where(blk64 == own64, 1.0, 0.0).astype(BF16)
    nt = (((1,), (1,)), ((), ()))

    outs = []
    for j in range(heads_per_step):
        slope = slope_ref[hp * heads_per_step + j]
        lo, hi = j * HEAD_DIM, (j + 1) * HEAD_DIM
        qh = q_ref[:, lo:hi]
        kt = kt_ref[lo:hi, :]
        vt = vt_ref[lo:hi, :]

        kb = kt.astype(BF16)
        k_lo = (kt - kb.astype(F32)).astype(BF16)
        kmean = (lax.dot_general(block_mean, kb, nt, preferred_element_type=F32)
                 + lax.dot_general(block_mean, k_lo, nt, preferred_element_type=F32))
        g_t = lax.dot_general(kmean.astype(BF16), qh.astype(BF16), nt, preferred_element_type=F32)
        gm = jnp.where(blk < own, g_t, NEG_INF)
        rank = _slot_rank(gm, nb, 0)
        chosen = (rank < n_sel) & (rank < own)
        sel_bias_t = jnp.where(chosen | (blk >= own), 0.0, NEG_INF)
        own_mult_t = jnp.where(chosen, 2.0, 1.0)
        sel = jnp.transpose(jnp.concatenate([sel_bias_t, own_mult_t, pad_rows], axis=0))

        q_aug = jnp.concatenate([(qh * (HEAD_DIM ** -0.5)).astype(BF16),
                                 sel[:, :HEAD_DIM].astype(BF16)], axis=1)
        k_aug = jnp.concatenate([kb, block_flag], axis=0)
        vb = vt.astype(BF16)
        alibi = slope * rc
        o_blocks = []
        for i in range(nb):
            w = (i + 1) * bs
            rows = slice(i * bs, (i + 1) * bs)
            s = jnp.dot(q_aug[rows], k_aug[:, :w], preferred_element_type=F32) - alibi[:, :w]
            s_own = s[:, i * bs:] + causal_bias
            s = jnp.concatenate([s[:, :i * bs], s_own], axis=1) if i else s_own
            m = jnp.max(s, axis=1, keepdims=True)
            p = jnp.exp(s - m)
            p_own = p[:, i * bs:] * sel[rows, nb + i:nb + i + 1]
            p = jnp.concatenate([p[:, :i * bs], p_own], axis=1) if i else p_own
            p = p * (1.0 / jnp.sum(p, axis=1, keepdims=True))
            o_blocks.append(lax.dot_general(p.astype(BF16), vb[:, :w], nt, preferred_element_type=F32))
        outs.append(jnp.concatenate(o_blocks, axis=0))
    o_ref[...] = jnp.concatenate(outs, axis=1)


def _moba_prompt(q, kt, vt, slopes, b, t):
    blk = pl.BlockSpec((t, LANES), lambda bi, hp: (bi, hp))
    blk_t = pl.BlockSpec((None, LANES, t), lambda bi, hp: (bi, hp, 0))
    return pl.pallas_call(
        functools.partial(_moba_prompt_kernel, t=t),
        out_shape=jax.ShapeDtypeStruct((b * t, D_MODEL), F32),
        grid=(b, D_MODEL // LANES),
        in_specs=[pl.BlockSpec(memory_space=pltpu.SMEM), blk, blk_t, blk_t],
        out_specs=blk,
        compiler_params=_params("arbitrary", "arbitrary"),
        name="moba_prompt",
    )(slopes, q, kt, vt)


def _moba_sample_kernel(pt_ref, slope_ref, q_ref, kn_ref, vn_ref, *refs, t, nbp, bps):
    del pt_ref
    n_pg = bps * PAGES_PER_BLOCK
    k_refs, v_refs = refs[:n_pg], refs[n_pg:2 * n_pg]
    o_ref, qbd_ref, kmean_ref, s_ref, pown_ref, l_ref, acc_ref = refs[2 * n_pg:]
    k_steps = nbp // bps
    step = pl.program_id(1)
    bs = MOBA_BLOCK
    hq = N_HEADS * t
    past_len = nbp * bs
    own_blk = nbp
    nb = nbp + 1
    n_sel = min(MOBA_TOPK, nb)

    row = lax.broadcasted_iota(jnp.int32, (hq, bs), 0)
    lane = lax.broadcasted_iota(jnp.int32, (hq, bs), 1)
    assert t & (t - 1) == 0, t
    qpos = past_len + (row & (t - 1))
    slope = slope_ref[...]

    def head_diag(x):
        r = _div_pow2(lax.broadcasted_iota(jnp.int32, x.shape, 0), t)
        c = _div_pow2(lax.broadcasted_iota(jnp.int32, x.shape, 1), HEAD_DIM)
        return jnp.where(r == c, x, 0.0)

    @pl.when(step == 0)
    def _():
        q_rep = jnp.concatenate([q_ref[...]] * N_HEADS, axis=0)
        qbd_ref[...] = head_diag(q_rep)
        kmean_ref[...] = jnp.zeros_like(kmean_ref)
        acc_ref[...] = jnp.zeros_like(acc_ref)

    @pl.when(step < k_steps)
    def _():
        qbd = (qbd_ref[...] * (HEAD_DIM ** -0.5)).astype(BF16)
        blk_lane = lax.broadcasted_iota(jnp.int32, kmean_ref.shape, 1)
        kmean_new = kmean_ref[...]
        for jb in range(bps):
            n = step * bps + jb
            ka, kb = k_refs[2 * jb][...], k_refs[2 * jb + 1][...]
            ksum = jnp.sum(ka + kb, axis=1, keepdims=True) * (1.0 / bs)
            kmean_new = kmean_new + jnp.where(blk_lane == n, ksum, 0.0)
            s = jnp.concatenate([jnp.dot(qbd, ka.astype(BF16), preferred_element_type=F32),
                                 jnp.dot(qbd, kb.astype(BF16), preferred_element_type=F32)], axis=1)
            dist = qpos - (n * bs + lane)
            s_ref[n] = s - slope * dist.astype(F32)
        kmean_ref[...] = kmean_new

    @pl.when(step == k_steps - 1)
    def _():
        qbd = qbd_ref[...]
        g = jnp.dot(qbd.astype(BF16), kmean_ref[...].astype(BF16), preferred_element_type=F32)
        col = lax.broadcasted_iota(jnp.int32, g.shape, 1)
        gm = jnp.where(col < own_blk, g, NEG_INF)
        rank = _slot_rank(gm, nb, 1)
        chosen = (rank < n_sel) & (rank < own_blk) & (col < nb)

        kn = jnp.concatenate([kn_ref[...], jnp.zeros((LANES - t, D_MODEL), F32)], axis=0)
        s_own = lax.dot_general((qbd * (HEAD_DIM ** -0.5)).astype(BF16), kn.astype(BF16),
                                (((1,), (1,)), ((), ())), preferred_element_type=F32)
        dist_own = (qpos - (past_len + lane))[:, :LANES]
        s_own = s_own - slope[:, :LANES] * dist_own.astype(F32)
        s_own = jnp.where(dist_own >= 0, s_own, NEG_INF)

        def sel_col(n):
            return jnp.sum(jnp.where((col == n) & chosen, 1.0, 0.0), axis=1, keepdims=True)

        def max_body(n, m):
            sn = jnp.where(sel_col(n) > 0.0, s_ref[n], NEG_INF)
            return jnp.maximum(m, jnp.max(sn, axis=1, keepdims=True))

        m = lax.fori_loop(0, nbp, max_body, jnp.max(s_own, axis=1, keepdims=True))

        def e_body(n, l):
            e = jnp.where(sel_col(n) > 0.0, jnp.exp(s_ref[n] - m), 0.0)
            s_ref[n] = e
            return l + jnp.sum(e, axis=1, keepdims=True)

        own_mult = 1.0 + sel_col(own_blk)
        e_own = jnp.exp(s_own - m) * own_mult
        l = lax.fori_loop(0, nbp, e_body, jnp.sum(e_own, axis=1, keepdims=True))
        pown_ref[...] = (e_own / l).astype(BF16)
        l_ref[...] = jnp.broadcast_to(l, l_ref.shape)

    @pl.when(step >= k_steps)
    def _():
        nt = (((1,), (1,)), ((), ()))
        acc = acc_ref[...]
        for jb in range(bps):
            p = (s_ref[(step - k_steps) * bps + jb] / l_ref[:, :1]).astype(BF16)
            for half in range(PAGES_PER_BLOCK):
                acc = acc + lax.dot_general(p[:, half * PAGE_SIZE:(half + 1) * PAGE_SIZE],
                                            v_refs[2 * jb + half][...].astype(BF16), nt,
                                            preferred_element_type=F32)
        acc_ref[...] = acc

    @pl.when(step == 2 * k_steps - 1)
    def _():
        vn = jnp.concatenate([vn_ref[...], jnp.zeros((LANES - t, D_MODEL), F32)], axis=0).astype(BF16)
        acc = acc_ref[...] + jnp.dot(pown_ref[...], vn, preferred_element_type=F32)
        acc = head_diag(acc)
        o_ref[...] = jnp.sum(acc.reshape(N_HEADS, t, D_MODEL), axis=0)


def _moba_sample(q, k_new, v_new, kt_pool, vt_pool, page_table, slopes, db, t):
    n_pages = page_table.shape[1]
    assert n_pages % PAGES_PER_BLOCK == 0 and PAGES_PER_BLOCK == 2 and t <= LANES
    nbp = n_pages // PAGES_PER_BLOCK
    bps = min(SAMPLE_BLOCKS_PER_STEP, nbp)
    assert nbp % bps == 0
    k_steps = nbp // bps
    n_pg = bps * PAGES_PER_BLOCK
    hq = N_HEADS * t
    nb_pad = -(-(nbp + 1) // LANES) * LANES
    slope_rows = jnp.broadcast_to(jnp.repeat(slopes, t)[:, None], (hq, MOBA_BLOCK))
    new_rows = pl.BlockSpec((t, D_MODEL), lambda b, s, pt: (b, 0))

    def page(first, j):
        def index(b, s, pt):
            return (pt[b, jnp.clip(s - first, 0, k_steps - 1) * n_pg + j], 0, 0)
        return pl.BlockSpec((None, D_MODEL, PAGE_SIZE), index)

    grid_spec = pltpu.PrefetchScalarGridSpec(
        num_scalar_prefetch=1,
        grid=(db, 2 * k_steps),
        in_specs=[pl.BlockSpec((hq, MOBA_BLOCK), lambda b, s, pt: (0, 0)),
                  new_rows, new_rows, new_rows,
                  *[page(0, j) for j in range(n_pg)], *[page(k_steps, j) for j in range(n_pg)]],
        out_specs=new_rows,
        scratch_shapes=[pltpu.VMEM((hq, D_MODEL), F32),
                        pltpu.VMEM((D_MODEL, nb_pad), F32),
                        pltpu.VMEM((nbp, hq, MOBA_BLOCK), F32),
                        pltpu.VMEM((hq, LANES), BF16),
                        pltpu.VMEM((hq, LANES), F32),
                        pltpu.VMEM((hq, D_MODEL), F32)])
    return pl.pallas_call(
        functools.partial(_moba_sample_kernel, t=t, nbp=nbp, bps=bps),
        out_shape=jax.ShapeDtypeStruct((db * t, D_MODEL), F32),
        grid_spec=grid_spec,
        compiler_params=_params("arbitrary", "arbitrary"),
        name="moba_sample",
    )(page_table, slope_rows, q, k_new, v_new, *([kt_pool] * n_pg), *([vt_pool] * n_pg))


def _rglru_kernel(u_ref, conv0_ref, h0_ref, cw_ref, cb_ref, wa_ref, ba_ref, wx_ref, bx_ref, lam_ref,
                  *rest, tt):
    hy_ref, conv_ref, hlast_ref, halo_ref, h_ref, a_ref, b_ref, hs_ref = rest[-8:]
    i = pl.program_id(1)
    keep = CONV_W - 1
    first = CONV_HALO - keep

    @pl.when(i == 0)
    def _():
        halo_ref[first:CONV_HALO, :] = conv0_ref[...]
        h_ref[...] = h0_ref[...]

    u = u_ref[...]
    y_in = u[:, :D_RNN]
    y_branch = 0.5 * y_in * (1.0 + jnp.tanh(0.7978845608028654 * (y_in + 0.044715 * (y_in * y_in * y_in))))
    halo_ref[CONV_HALO:CONV_HALO + tt, :] = u[:, D_RNN:]
    xc = cb_ref[...]
    acc = None
    for j in range(CONV_W):
        term = halo_ref[first + j:first + j + tt, :] * cw_ref[j:j + 1, :]
        acc = term if acc is None else acc + term
    xc = xc + acc

    r_parts, i_parts = [], []
    for blk in range(N_LRU_BLOCKS):
        xb = xc[:, blk * LRU_BLOCK_WIDTH:(blk + 1) * LRU_BLOCK_WIDTH].astype(BF16)
        r_parts.append(jnp.dot(xb, wa_ref[blk], preferred_element_type=F32))
        i_parts.append(jnp.dot(xb, wx_ref[blk], preferred_element_type=F32))
    r = jax.nn.sigmoid(jnp.concatenate(r_parts, axis=1) + ba_ref[...])
    ig = jax.nn.sigmoid(jnp.concatenate(i_parts, axis=1) + bx_ref[...])
    log_a = LRU_C * r * jax.nn.log_sigmoid(lam_ref[...])
    a_ref[...] = jnp.exp(log_a)
    b_ref[...] = jnp.sqrt(1.0 - jnp.exp(2.0 * log_a)) * (ig * xc)

    def step(s, h):
        h = a_ref[pl.ds(s, 1), :] * h + b_ref[pl.ds(s, 1), :]
        hs_ref[pl.ds(s, 1), :] = h
        return h

    h = lax.fori_loop(0, tt, step, h_ref[...], unroll=8)
    h_ref[...] = h
    hy_ref[...] = hs_ref[...] * y_branch
    tail = halo_ref[first + tt:CONV_HALO + tt, :]
    halo_ref[first:CONV_HALO, :] = tail

    @pl.when(i == pl.num_programs(1) - 1)
    def _():
        conv_ref[...] = tail
        hlast_ref[...] = h


def _rglru(u, row0, conv0, h0, cw, cb, wa_bf, ba, wx_bf, bx, lam, b, t, *, into=None):
    tt = _row_tile(t, 256)
    nt = t // tt
    assert row0 % tt == 0
    blk0 = row0 // tt
    keep = CONV_W - 1
    vec = pl.BlockSpec((1, D_RNN), lambda bi, i: (0, 0))
    gate_w = pl.BlockSpec((N_LRU_BLOCKS, LRU_BLOCK_WIDTH, LRU_BLOCK_WIDTH), lambda bi, i: (0, 0, 0))
    state3 = pl.BlockSpec((None, keep, D_RNN), lambda bi, i: (bi, 0, 0))
    state1 = pl.BlockSpec((None, 1, D_RNN), lambda bi, i: (bi, 0, 0))
    rows = pl.BlockSpec((tt, D_RNN), lambda bi, i: (blk0 + bi * nt + i, 0))
    extra_specs, extra_args, aliases = [], [], {}
    if into is not None:
        extra_specs, extra_args, aliases = [pl.BlockSpec(memory_space=pl.ANY)], [into], {10: 0}
    return pl.pallas_call(
        functools.partial(_rglru_kernel, tt=tt),
        out_shape=(jax.ShapeDtypeStruct((u.shape[0], D_RNN), F32),
                   jax.ShapeDtypeStruct((b, keep, D_RNN), F32),
                   jax.ShapeDtypeStruct((b, 1, D_RNN), F32)),
        grid=(b, nt),
        in_specs=[pl.BlockSpec((tt, 2 * D_RNN), lambda bi, i: (blk0 + bi * nt + i, 0)),
                  state3, state1,
                  pl.BlockSpec((CONV_W, D_RNN), lambda bi, i: (0, 0)), vec,
                  gate_w, vec, gate_w, vec, vec, *extra_specs],
        out_specs=(rows, state3, state1),
        input_output_aliases=aliases,
        scratch_shapes=[pltpu.VMEM((CONV_HALO + tt, D_RNN), F32),
                        pltpu.VMEM((1, D_RNN), F32),
                        pltpu.VMEM((tt, D_RNN), F32),
                        pltpu.VMEM((tt, D_RNN), F32),
                        pltpu.VMEM((tt, D_RNN), F32)],
        compiler_params=_params("arbitrary", "arbitrary"),
        name="rglru",
    )(u, conv0, h0.reshape(b, 1, D_RNN), cw, cb.reshape(1, D_RNN), wa_bf, ba.reshape(1, D_RNN),
      wx_bf, bx.reshape(1, D_RNN), lam.reshape(1, D_RNN), *extra_args)


def _router_kernel(x_ref, wt_ref, b_ref, ids_ref, gates_ref, rank_ref, counts_ref, carry_ref):
    i = pl.program_id(0)
    tm = x_ref.shape[0]

    @pl.when(i == 0)
    def _():
        carry_ref[...] = jnp.zeros_like(carry_ref)

    logits = lax.dot_general(wt_ref[...].astype(BF16), x_ref[...].astype(BF16), (((1,), (1,)), ((), ())),
                             preferred_element_type=F32) + b_ref[:, :1]
    e_io = lax.broadcasted_iota(jnp.int32, logits.shape, 0)
    lt = logits
    vals, ids = [], []
    for _ in range(TOP_K):
        mx = jnp.max(lt, axis=0, keepdims=True)
        idx = jnp.min(jnp.where(lt == mx, e_io, N_EXPERTS), axis=0, keepdims=True)
        vals.append(mx)
        ids.append(idx)
        lt = jnp.where(e_io == idx, -jnp.inf, lt)
    ex = [jnp.exp(v - vals[0]) for v in vals]
    den = ex[0] + ex[1] + ex[2] + ex[3]
    gates_ref[...] = jnp.concatenate([e / den for e in ex], axis=0)
    ids_ref[...] = jnp.concatenate(ids, axis=0)

    onehot = jnp.zeros(logits.shape, F32)
    for idx in ids:
        onehot = onehot + jnp.where(e_io == idx, 1.0, 0.0)
    s_io = lax.broadcasted_iota(jnp.int32, (tm, tm), 0)
    t_io = lax.broadcasted_iota(jnp.int32, (tm, tm), 1)
    before = jnp.where(s_io < t_io, 1.0, 0.0).astype(BF16)
    prefix = jnp.dot(onehot.astype(BF16), before, preferred_element_type=F32) + carry_ref[:, :1]
    ranks = [jnp.sum(jnp.where(e_io == idx, prefix, 0.0), axis=0, keepdims=True) for idx in ids]
    rank_ref[...] = jnp.concatenate(ranks, axis=0).astype(jnp.int32)
    carry_ref[...] += jnp.sum(onehot, axis=1, keepdims=True)
    counts_ref[...] = carry_ref[...].astype(jnp.int32)


def _router(x, w_router, b_router):
    n = x.shape[0]
    tm = _row_tile(n, ROUTER_TILE)
    slot = pl.BlockSpec((TOP_K, tm), lambda i: (0, i))
    per_expert = pl.BlockSpec((N_EXPERTS, LANES), lambda i: (0, 0))
    return pl.pallas_call(
        _router_kernel,
        out_shape=(jax.ShapeDtypeStruct((TOP_K, n), jnp.int32),
                   jax.ShapeDtypeStruct((TOP_K, n), F32),
                   jax.ShapeDtypeStruct((TOP_K, n), jnp.int32),
                   jax.ShapeDtypeStruct((N_EXPERTS, LANES), jnp.int32)),
        grid=(n // tm,),
        in_specs=[pl.BlockSpec((tm, D_MODEL), lambda i: (i, 0)),
                  pl.BlockSpec((N_EXPERTS, D_MODEL), lambda i: (0, 0)),
                  per_expert],
        out_specs=(slot, slot, slot, per_expert),
        scratch_shapes=[pltpu.VMEM((N_EXPERTS, LANES), F32)],
        compiler_params=_params("arbitrary"),
        name="moe_router",
    )(x, w_router.T, jnp.broadcast_to(b_router[:, None], (N_EXPERTS, LANES)))


def _dispatch_kernel(last_ref, dest_ref, x_ref, xg_hbm, zero_ref, sem, zero_sem):
    tm = dest_ref.shape[1]
    rows = zero_ref.shape[0]

    def zero_copy(e):
        return pltpu.make_async_copy(zero_ref, xg_hbm.at[pl.ds(last_ref[e] * rows, rows)], zero_sem)

    @pl.when(pl.program_id(0) == 0)
    def _():
        zero_ref[...] = jnp.zeros_like(zero_ref)
        for e in range(N_EXPERTS):
            pl.when(last_ref[e] >= 0)(lambda e=e: zero_copy(e).start())
        for e in range(N_EXPERTS):
            pl.when(last_ref[e] >= 0)(lambda e=e: zero_copy(e).wait())

    def row_copy(r, k):
        return pltpu.make_async_copy(x_ref.at[pl.ds(r, 1)], xg_hbm.at[pl.ds(dest_ref[k, r], 1)], sem)

    def issue(r, c):
        for k in range(TOP_K):
            row_copy(r, k).start(priority=k % 2)
        return c

    def drain(r, c):
        for k in range(TOP_K):
            row_copy(r, k).wait()
        return c

    lax.fori_loop(0, tm, issue, 0, unroll=8)
    lax.fori_loop(0, tm, drain, 0, unroll=8)


def _dispatch(x, dest, last_tile, n_rows):
    n = x.shape[0]
    tm = _row_tile(n, DISPATCH_TILE)
    grid_spec = pltpu.PrefetchScalarGridSpec(
        num_scalar_prefetch=1,
        grid=(n // tm,),
        in_specs=[pl.BlockSpec((TOP_K, tm), lambda i, last: (0, i), memory_space=pltpu.SMEM),
                  pl.BlockSpec((tm, D_MODEL), lambda i, last: (i, 0))],
        out_specs=pl.BlockSpec(memory_space=pl.ANY),
        scratch_shapes=[pltpu.VMEM((EXPERT_TILE_ROWS, D_MODEL), F32),
                        pltpu.SemaphoreType.DMA, pltpu.SemaphoreType.DMA])
    return pl.pallas_call(
        _dispatch_kernel,
        out_shape=jax.ShapeDtypeStruct((n_rows, D_MODEL), F32),
        grid_spec=grid_spec,
        compiler_params=_params("arbitrary"),
        name="moe_dispatch",
    )(last_tile, dest, x)


def _expert_kernel(te_ref, tv_ref, xg_ref, wu_ref, bu_ref, wd_ref, bd_ref, y_ref, wu_bf_ref, wd_bf_ref):
    g = pl.program_id(0)
    e = te_ref[g]
    prev = te_ref[jnp.maximum(g - 1, 0)]

    @pl.when((g == 0) | (e != prev))
    def _():
        wu_bf_ref[...] = wu_ref[...].astype(BF16)
        wd_bf_ref[...] = wd_ref[...].astype(BF16)

    @pl.when(tv_ref[g] > 0)
    def _():
        gu = jnp.dot(xg_ref[...].astype(BF16), wu_bf_ref[...], preferred_element_type=F32) + bu_ref[...]
        gate = jnp.minimum(gu[:, :D_EXPERT], SWIGLU_LIMIT)
        up = jnp.clip(gu[:, D_EXPERT:], -SWIGLU_LIMIT, SWIGLU_LIMIT)
        glu = gate * jax.nn.sigmoid(SWIGLU_ALPHA * gate)
        y_ref[...] = jnp.dot(((up + 1.0) * glu).astype(BF16), wd_bf_ref[...],
                             preferred_element_type=F32) + bd_ref[...]

    @pl.when(tv_ref[g] == 0)
    def _():
        y_ref[...] = jnp.zeros_like(y_ref)


def _experts(tile_e, tile_valid, xg, layer, w_up, b_up, w_down, b_down):
    n_rows = xg.shape[0]
    tm = EXPERT_TILE_ROWS
    n_layers = w_up.shape[0]
    grid_spec = pltpu.PrefetchScalarGridSpec(
        num_scalar_prefetch=2,
        grid=(n_rows // tm,),
        in_specs=[pl.BlockSpec((tm, D_MODEL), lambda g, te, tv: (g * tv[g], 0)),
                  pl.BlockSpec((None, None, D_MODEL, 2 * D_EXPERT), lambda g, te, tv: (layer, te[g], 0, 0)),
                  pl.BlockSpec((None, None, 1, 2 * D_EXPERT), lambda g, te, tv: (layer, te[g], 0, 0)),
                  pl.BlockSpec((None, None, D_EXPERT, D_MODEL), lambda g, te, tv: (layer, te[g], 0, 0)),
                  pl.BlockSpec((None, None, 1, D_MODEL), lambda g, te, tv: (layer, te[g], 0, 0))],
        out_specs=pl.BlockSpec((tm, D_MODEL), lambda g, te, tv: (g, 0)),
        scratch_shapes=[pltpu.VMEM((D_MODEL, 2 * D_EXPERT), BF16),
                        pltpu.VMEM((D_EXPERT, D_MODEL), BF16)])
    return pl.pallas_call(
        _expert_kernel,
        out_shape=jax.ShapeDtypeStruct((n_rows, D_MODEL), F32),
        grid_spec=grid_spec,
        compiler_params=_params("arbitrary"),
        name="moe_experts",
    )(tile_e, tile_valid, xg, w_up, b_up.reshape(n_layers, N_EXPERTS, 1, 2 * D_EXPERT),
      w_down, b_down.reshape(n_layers, N_EXPERTS, 1, D_MODEL))


def _combine_ln_kernel(dest_ref, dest_next_ref, x_ref, gate_ref, g_ref, beta_ref, yg_hbm, o_ref, buf_ref, sem):
    i = pl.program_id(0)
    n_steps = pl.num_programs(0)
    tm = x_ref.shape[0]

    def row_copy(d_ref, slot, r, k):
        return pltpu.make_async_copy(yg_hbm.at[pl.ds(d_ref[k, r], 1)],
                                     buf_ref.at[slot, k, pl.ds(r, 1)], sem.at[slot])

    def issue(d_ref, slot):
        def body(r, c):
            for k in range(TOP_K):
                row_copy(d_ref, slot, r, k).start(priority=k % 2)
            return c
        lax.fori_loop(0, tm, body, 0, unroll=8)

    @pl.when(i == 0)
    def _():
        issue(dest_ref, 0)

    @pl.when(i + 1 < n_steps)
    def _():
        issue(dest_next_ref, (i + 1) % 2)

    slot = i % 2

    def drain(r, c):
        for k in range(TOP_K):
            row_copy(dest_ref, slot, r, k).wait()
        return c

    lax.fori_loop(0, tm, drain, 0, unroll=8)
    gates = gate_ref[...]
    y = gates[:, 0:1] * buf_ref[slot, 0]
    for k in range(1, TOP_K):
        y = y + gates[:, k:k + 1] * buf_ref[slot, k]
    o_ref[...] = _layer_norm(DEEPNORM_ALPHA * x_ref[...] + y, g_ref[...], beta_ref[...])


def _combine_ln(x, yg, dest, gates_t, g, beta):
    n = x.shape[0]
    tm = _row_tile(n, COMBINE_TILE)
    n_steps = n // tm
    row = pl.BlockSpec((tm, D_MODEL), lambda i: (i, 0))
    vec = pl.BlockSpec((1, D_MODEL), lambda i: (0, 0))
    return pl.pallas_call(
        _combine_ln_kernel,
        out_shape=jax.ShapeDtypeStruct((n, D_MODEL), F32),
        grid=(n_steps,),
        in_specs=[pl.BlockSpec((TOP_K, tm), lambda i: (0, i), memory_space=pltpu.SMEM),
                  pl.BlockSpec((TOP_K, tm), lambda i: (0, jnp.minimum(i + 1, n_steps - 1)),
                               memory_space=pltpu.SMEM),
                  row, pl.BlockSpec((tm, TOP_K), lambda i: (i, 0)), vec, vec,
                  pl.BlockSpec(memory_space=pl.ANY)],
        out_specs=row,
        scratch_shapes=[pltpu.VMEM((2, TOP_K, tm, D_MODEL), F32), pltpu.SemaphoreType.DMA((2,))],
        compiler_params=_params("arbitrary"),
        name="moe_combine_ln",
    )(dest, dest, x, gates_t, g.reshape(1, D_MODEL), beta.reshape(1, D_MODEL), yg)


def _moe_ln(x, layer, w_router, b_router, w_up, b_up, w_down, b_down, g, beta):
    n = x.shape[0]
    tm = EXPERT_TILE_ROWS
    ids, gates, rank, counts = _router(x, w_router, b_router)
    counts = counts[:, 0]
    padded = (counts + tm - 1) // tm * tm
    pend = jnp.cumsum(padded)
    pstart = pend - padded
    expert_io = jnp.arange(N_EXPERTS, dtype=jnp.int32)[:, None, None]
    dest = jnp.sum(jnp.where(ids[None] == expert_io, pstart[:, None, None], 0), axis=0) + rank
    n_tiles = -(-(n * TOP_K) // tm) + N_EXPERTS
    tile_row0 = jnp.arange(n_tiles, dtype=jnp.int32) * tm
    tile_e = jnp.sum((pend[None, :] <= tile_row0[:, None]).astype(jnp.int32), axis=1)
    tile_e = jnp.minimum(tile_e, N_EXPERTS - 1)
    tile_valid = (tile_row0 < pend[-1]).astype(jnp.int32)
    last_tile = jnp.where(padded > 0, pend // tm - 1, -1).astype(jnp.int32)
    xg = _dispatch(x, dest, last_tile, n_tiles * tm)
    yg = _experts(tile_e, tile_valid, xg, layer, w_up, b_up, w_down, b_down)
    return _combine_ln(x, yg, dest, gates.T, g, beta)


def kernel(x_prompt, x_sample, cache_k, cache_v, state_conv, state_h, page_table, ln_g, ln_b, w_qkv, w_o, w_in, b_in, conv_w, conv_b, w_gate_a, b_gate_a, w_gate_x, b_gate_x, lru_lambda, w_out, b_out, w_router, b_router, w_up, b_up, w_down, b_down):
    bp, tp, d = x_prompt.shape
    db, ts, _ = x_sample.shape
    n_p, n_s = bp * tp, db * ts
    xp = x_prompt.reshape(n_p, d)
    xs = x_sample.reshape(n_s, d)
    slopes = 2.0 ** (-8.0 * jnp.arange(1, N_HEADS + 1, dtype=F32) / N_HEADS)
    zeros_d = jnp.zeros((d,), F32)

    def moe_block(x, layer):
        return _moe_ln(x, layer, w_router[layer], b_router[layer], w_up, b_up, w_down, b_down,
                       ln_g[layer, 1], ln_b[layer, 1])

    n = n_p + n_s

    w_qkv_bf = w_qkv[0].astype(BF16)
    w_o_bf = w_o[0].astype(BF16)
    wkt_bf = w_qkv[0][:, d:2 * d].T.astype(BF16)
    wvt_bf = w_qkv[0][:, 2 * d:].T.astype(BF16)
    qp, ktp, vtp = _qkv_prompt(xp, w_qkv_bf[:, :d], wkt_bf, wvt_bf, bp, tp)
    qs, ks, vs = _qkv(xs, w_qkv_bf)
    op = _moba_prompt(qp, ktp, vtp, slopes, bp, tp)
    n_phys = cache_k.shape[1]
    kt_pool = jnp.transpose(cache_k[0], (0, 2, 3, 1)).reshape(n_phys, d, PAGE_SIZE)
    vt_pool = jnp.transpose(cache_v[0], (0, 2, 3, 1)).reshape(n_phys, d, PAGE_SIZE)
    os_ = _moba_sample(qs, ks, vs, kt_pool, vt_pool, page_table, slopes, db, ts)
    x = _linear_res_ln(op, w_o_bf, zeros_d, xp, ln_g[0, 0], ln_b[0, 0], out_rows=n)
    x = _linear_res_ln(os_, w_o_bf, zeros_d, xs, ln_g[0, 0], ln_b[0, 0], out_rows=n, row0=n_p, into=x)
    x = moe_block(x, 0)

    w_in_bf = w_in[0].astype(BF16)
    wa_bf = w_gate_a[0].astype(BF16)
    wx_bf = w_gate_x[0].astype(BF16)
    rec = (conv_w[0], conv_b[0], wa_bf, b_gate_a[0].reshape(-1), wx_bf, b_gate_x[0].reshape(-1), lru_lambda[0])
    u = _linear(x, w_in_bf, b_in[0])
    conv0 = jnp.zeros((bp, CONV_W - 1, D_RNN), F32)
    h0 = jnp.zeros((bp, D_RNN), F32)
    hy, conv_p, h_p = _rglru(u, 0, conv0, h0, *rec, bp, tp)
    hy, conv_s, h_s = _rglru(u, n_p, state_conv[0], state_h[0], *rec, db, ts, into=hy)
    x = _linear_res_ln(hy, w_out[0].astype(BF16), b_out[0], x, ln_g[1, 0], ln_b[1, 0])
    x = moe_block(x, 1)
    xp, xs = x[:n_p], x[n_p:]

    def kv_rows(a_t):
        return jnp.transpose(a_t.reshape(1, bp, N_HEADS, HEAD_DIM, tp), (0, 1, 4, 2, 3))

    kv_s = (1, db, ts, N_HEADS, HEAD_DIM)
    return (xp.reshape(bp, tp, d), xs.reshape(db, ts, d),
            kv_rows(ktp), kv_rows(vtp), conv_p[None], h_p.reshape(1, bp, D_RNN),
            ks.reshape(kv_s), vs.reshape(kv_s), conv_s[None], h_s.reshape(1, db, D_RNN))
```

```python
import functools

import jax
import jax.numpy as jnp
from jax import lax
from jax.experimental import pallas as pl
from jax.experimental.pallas import tpu as pltpu

F32 = jnp.float32
BF16 = jnp.bfloat16

D_MODEL = 1024
N_HEADS = 16
HEAD_DIM = D_MODEL // N_HEADS
MOBA_BLOCK = 256
MOBA_TOPK = 3
PAGE_SIZE = 128
PAGES_PER_BLOCK = MOBA_BLOCK // PAGE_SIZE
D_RNN = D_MODEL
LRU_BLOCK_WIDTH = 256
N_LRU_BLOCKS = D_RNN // LRU_BLOCK_WIDTH
CONV_W = 4
LRU_C = 8.0
N_EXPERTS = 32
TOP_K = 4
D_EXPERT = D_MODEL
SWIGLU_LIMIT = 7.0
SWIGLU_ALPHA = 1.702
DEPTH = 2
DEEPNORM_ALPHA = (2 * DEPTH) ** 0.25
LN_EPS = 1e-5
NEG_INF = -1e30

V7X_VMEM_LIMIT_BYTES = 56 * 1024 * 1024
SUBLANES = 8
LANES = 128
CONV_HALO = SUBLANES
EXPERT_TILE_ROWS = 512
ROUTER_TILE = 256
SAMPLE_BLOCKS_PER_STEP = 8
DISPATCH_TILE = 640
COMBINE_TILE = 256


def _params(*sem):
    return pltpu.CompilerParams(dimension_semantics=sem, vmem_limit_bytes=V7X_VMEM_LIMIT_BYTES)


def _row_tile(m, pref):
    t = min(pref, m)
    while m % t or t % SUBLANES:
        t -= SUBLANES
    return t


def _div_pow2(x, n):
    assert n & (n - 1) == 0, n
    return lax.shift_right_logical(x, n.bit_length() - 1)


def _layer_norm(y, g, b):
    mu = jnp.mean(y, axis=-1, keepdims=True)
    yc = y - mu
    var = jnp.mean(yc * yc, axis=-1, keepdims=True)
    return yc * lax.rsqrt(var + LN_EPS) * g + b


def _qkv_kernel(x_ref, w_ref, q_ref, k_ref, v_ref):
    x = x_ref[...].astype(BF16)
    for c, o_ref in enumerate((q_ref, k_ref, v_ref)):
        o_ref[...] = jnp.dot(x, w_ref[:, c * D_MODEL:(c + 1) * D_MODEL], preferred_element_type=F32)


def _qkv(x, w_bf):
    m = x.shape[0]
    tm = _row_tile(m, 512)
    out = jax.ShapeDtypeStruct((m, D_MODEL), F32)
    row = pl.BlockSpec((tm, D_MODEL), lambda i: (i, 0))
    return pl.pallas_call(
        _qkv_kernel,
        out_shape=(out, out, out),
        grid=(m // tm,),
        in_specs=[row, pl.BlockSpec((D_MODEL, 3 * D_MODEL), lambda i: (0, 0))],
        out_specs=(row, row, row),
        compiler_params=_params("arbitrary"),
        name="qkv_proj",
    )(x, w_bf)


def _qkv_prompt_kernel(x_ref, wq_ref, wkt_ref, wvt_ref, q_ref, kt_ref, vt_ref):
    x = x_ref[...].astype(BF16)
    nt = (((1,), (1,)), ((), ()))
    q_ref[...] = jnp.dot(x, wq_ref[...], preferred_element_type=F32)
    kt_ref[...] = lax.dot_general(wkt_ref[...], x, nt, preferred_element_type=F32)
    vt_ref[...] = lax.dot_general(wvt_ref[...], x, nt, preferred_element_type=F32)


def _qkv_prompt(x, wq_bf, wkt_bf, wvt_bf, b, t):
    tm = _row_tile(t, 512)
    nt = t // tm
    w = pl.BlockSpec((D_MODEL, D_MODEL), lambda bi, i: (0, 0))
    row = pl.BlockSpec((tm, D_MODEL), lambda bi, i: (bi * nt + i, 0))
    col = pl.BlockSpec((None, D_MODEL, tm), lambda bi, i: (bi, 0, i))
    t_out = jax.ShapeDtypeStruct((b, D_MODEL, t), F32)
    return pl.pallas_call(
        _qkv_prompt_kernel,
        out_shape=(jax.ShapeDtypeStruct((b * t, D_MODEL), F32), t_out, t_out),
        grid=(b, nt),
        in_specs=[row, w, w, w],
        out_specs=(row, col, col),
        compiler_params=_params("arbitrary", "arbitrary"),
        name="qkv_proj_prompt",
    )(x, wq_bf, wkt_bf, wvt_bf)


def _linear_res_ln_kernel(a_ref, w_ref, b_ref, res_ref, g_ref, beta_ref, *rest):
    o_ref = rest[-1]
    m = jnp.dot(a_ref[...].astype(BF16), w_ref[...], preferred_element_type=F32) + b_ref[...]
    o_ref[...] = _layer_norm(DEEPNORM_ALPHA * res_ref[...] + m, g_ref[...], beta_ref[...])


def _linear_res_ln(a, w_bf, b, res, g, beta, *, out_rows=None, row0=0, into=None):
    m, k = a.shape
    n = w_bf.shape[1]
    out_rows = m if out_rows is None else out_rows
    tm = _row_tile(m, 512)
    assert row0 % tm == 0
    blk0 = row0 // tm
    row_in = pl.BlockSpec((tm, k), lambda i: (i, 0))
    row = pl.BlockSpec((tm, n), lambda i: (i, 0))
    vec = pl.BlockSpec((1, n), lambda i: (0, 0))
    in_specs = [row_in, pl.BlockSpec((k, n), lambda i: (0, 0)), vec, row, vec, vec]
    args = [a, w_bf, b.reshape(1, n), res, g.reshape(1, n), beta.reshape(1, n)]
    aliases = {}
    if into is not None:
        in_specs.append(pl.BlockSpec(memory_space=pl.ANY))
        args.append(into)
        aliases = {len(args) - 1: 0}
    return pl.pallas_call(
        _linear_res_ln_kernel,
        out_shape=jax.ShapeDtypeStruct((out_rows, n), F32),
        grid=(m // tm,),
        in_specs=in_specs,
        out_specs=pl.BlockSpec((tm, n), lambda i: (blk0 + i, 0)),
        input_output_aliases=aliases,
        compiler_params=_params("arbitrary"),
        name="linear_res_ln",
    )(*args)


def _slot_rank(gm, n_blocks, axis):
    blk = lax.broadcasted_iota(jnp.int32, gm.shape, axis)
    rank = jnp.zeros(gm.shape, jnp.int32)
    for m in range(n_blocks):
        g_m = jnp.sum(jnp.where(blk == m, gm, 0.0), axis=axis, keepdims=True)
        beats = (g_m > gm) | ((g_m == gm) & (m < blk))
        rank = rank + jnp.where(beats, 1, 0)
    return rank


def _moba_prompt_kernel(slope_ref, q_ref, kt_ref, vt_ref, o_ref, *, t):
    nb = t // MOBA_BLOCK
    n_sel = min(MOBA_TOPK, nb)
    bs = MOBA_BLOCK
    hp = pl.program_id(1)
    heads_per_step = LANES // HEAD_DIM
    assert 2 * nb <= HEAD_DIM

    rc = (lax.broadcasted_iota(jnp.int32, (bs, t), 0)
          - lax.broadcasted_iota(jnp.int32, (bs, t), 1)).astype(F32)
    causal_bias = jnp.where(lax.broadcasted_iota(jnp.int32, (bs, bs), 0)
                            >= lax.broadcasted_iota(jnp.int32, (bs, bs), 1), 0.0, NEG_INF)

    blk = lax.broadcasted_iota(jnp.int32, (nb, t), 0)
    own = _div_pow2(lax.broadcasted_iota(jnp.int32, (nb, t), 1), MOBA_BLOCK)
    pad_rows = jnp.zeros((LANES - 2 * nb, t), F32)
    block_mean = jnp.where(blk == own, 1.0 / bs, 0.0).astype(BF16)
    blk64 = lax.broadcasted_iota(jnp.int32, (HEAD_DIM, t), 0)
    own64 = _div_pow2(lax.broadcasted_iota(jnp.int32, (HEAD_DIM, t), 1), MOBA_BLOCK)
    block_flag = jnp.where(blk64 == own64, 1.0, 0.0).astype(BF16)
    nt = (((1,), (1,)), ((), ()))

    outs = []
    for j in range(heads_per_step):
        slope = slope_ref[hp * heads_per_step + j]
        lo, hi = j * HEAD_DIM, (j + 1) * HEAD_DIM
        qh = q_ref[:, lo:hi]
        kt = kt_ref[lo:hi, :]
        vt = vt_ref[lo:hi, :]

        kb = kt.astype(BF16)
        k_lo = (kt - kb.astype(F32)).astype(BF16)
        kmean = (lax.dot_general(block_mean, kb, nt, preferred_element_type=F32)
                 + lax.dot_general(block_mean, k_lo, nt, preferred_element_type=F32))
        g_t = lax.dot_general(kmean.astype(BF16), qh.astype(BF16), nt, preferred_element_type=F32)
        gm = jnp.where(blk < own, g_t, NEG_INF)
        rank = _slot_rank(gm, nb, 0)
        chosen = (rank < n_sel) & (rank < own)
        sel_bias_t = jnp.where(chosen | (blk >= own), 0.0, NEG_INF)
        own_mult_t = jnp.where(chosen, 2.0, 1.0)
        sel = jnp.transpose(jnp.concatenate([sel_bias_t, own_mult_t, pad_rows], axis=0))

        q_aug = jnp.concatenate([(qh * (HEAD_DIM ** -0.5)).astype(BF16),
                                 sel[:, :HEAD_DIM].astype(BF16)], axis=1)
        k_aug = jnp.concatenate([kb, block_flag], axis=0)
        vb = vt.astype(BF16)
        alibi = slope * rc
        o_blocks = []
        for i in range(nb):
            w = (i + 1) * bs
            rows = slice(i * bs, (i + 1) * bs)
            s = jnp.dot(q_aug[rows], k_aug[:, :w], preferred_element_type=F32) - alibi[:, :w]
            s_own = s[:, i * bs:] + causal_bias
            s = jnp.concatenate([s[:, :i * bs], s_own], axis=1) if i else s_own
            m = jnp.max(s, axis=1, keepdims=True)
            p = jnp.exp(s - m)
            p_own = p[:, i * bs:] * sel[rows, nb + i:nb + i + 1]
            p = jnp.concatenate([p[:, :i * bs], p_own], axis=1) if i else p_own
            p = p * (1.0 / jnp.sum(p, axis=1, keepdims=True))
            o_blocks.append(lax.dot_general(p.astype(BF16), vb[:, :w], nt, preferred_element_type=F32))
        outs.append(jnp.concatenate(o_blocks, axis=0))
    o_ref[...] = jnp.concatenate(outs, axis=1)


def _moba_prompt(q, kt, vt, slopes, b, t):
    blk = pl.BlockSpec((t, LANES), lambda bi, hp: (bi, hp))
    blk_t = pl.BlockSpec((None, LANES, t), lambda bi, hp: (bi, hp, 0))
    return pl.pallas_call(
        functools.partial(_moba_prompt_kernel, t=t),
        out_shape=jax.ShapeDtypeStruct((b * t, D_MODEL), F32),
        grid=(b, D_MODEL // LANES),
        in_specs=[pl.BlockSpec(memory_space=pltpu.SMEM), blk, blk_t, blk_t],
        out_specs=blk,
        compiler_params=_params("arbitrary", "arbitrary"),
        name="moba_prompt",
    )(slopes, q, kt, vt)


def _moba_sample_kernel(pt_ref, slope_ref, q_ref, kn_ref, vn_ref, *refs, t, nbp, bps):
    del pt_ref
    n_pg = bps * PAGES_PER_BLOCK
    k_refs, v_refs = refs[:n_pg], refs[n_pg:2 * n_pg]
    o_ref, qbd_ref, kmean_ref, s_ref, pown_ref, l_ref, acc_ref = refs[2 * n_pg:]
    k_steps = nbp // bps
    step = pl.program_id(1)
    bs = MOBA_BLOCK
    hq = N_HEADS * t
    past_len = nbp * bs
    own_blk = nbp
    nb = nbp + 1
    n_sel = min(MOBA_TOPK, nb)

    row = lax.broadcasted_iota(jnp.int32, (hq, bs), 0)
    lane = lax.broadcasted_iota(jnp.int32, (hq, bs), 1)
    assert t & (t - 1) == 0, t
    qpos = past_len + (row & (t - 1))
    slope = slope_ref[...]

    def head_diag(x):
        r = _div_pow2(lax.broadcasted_iota(jnp.int32, x.shape, 0), t)
        c = _div_pow2(lax.broadcasted_iota(jnp.int32, x.shape, 1), HEAD_DIM)
        return jnp.where(r == c, x, 0.0)

    @pl.when(step == 0)
    def _():
        q_rep = jnp.concatenate([q_ref[...]] * N_HEADS, axis=0)
        qbd_ref[...] = head_diag(q_rep)
        kmean_ref[...] = jnp.zeros_like(kmean_ref)
        acc_ref[...] = jnp.zeros_like(acc_ref)

    @pl.when(step < k_steps)
    def _():
        qbd = (qbd_ref[...] * (HEAD_DIM ** -0.5)).astype(BF16)
        blk_lane = lax.broadcasted_iota(jnp.int32, kmean_ref.shape, 1)
        kmean_new = kmean_ref[...]
        for jb in range(bps):
            n = step * bps + jb
            ka, kb = k_refs[2 * jb][...], k_refs[2 * jb + 1][...]
            ksum = jnp.sum(ka + kb, axis=1, keepdims=True) * (1.0 / bs)
            kmean_new = kmean_new + jnp.where(blk_lane == n, ksum, 0.0)
            s = jnp.concatenate([jnp.dot(qbd, ka.astype(BF16), preferred_element_type=F32),
                                 jnp.dot(qbd, kb.astype(BF16), preferred_element_type=F32)], axis=1)
            dist = qpos - (n * bs + lane)
            s_ref[n] = s - slope * dist.astype(F32)
        kmean_ref[...] = kmean_new

    @pl.when(step == k_steps - 1)
    def _():
        qbd = qbd_ref[...]
        g = jnp.dot(qbd.astype(BF16), kmean_ref[...].astype(BF16), preferred_element_type=F32)
        col = lax.broadcasted_iota(jnp.int32, g.shape, 1)
        gm = jnp.where(col < own_blk, g, NEG_INF)
        rank = _slot_rank(gm, nb, 1)
        chosen = (rank < n_sel) & (rank < own_blk) & (col < nb)

        kn = jnp.concatenate([kn_ref[...], jnp.zeros((LANES - t, D_MODEL), F32)], axis=0)
        s_own = lax.dot_general((qbd * (HEAD_DIM ** -0.5)).astype(BF16), kn.astype(BF16),
                                (((1,), (1,)), ((), ())), preferred_element_type=F32)
        dist_own = (qpos - (past_len + lane))[:, :LANES]
        s_own = s_own - slope[:, :LANES] * dist_own.astype(F32)
        s_own = jnp.where(dist_own >= 0, s_own, NEG_INF)

        def sel_col(n):
            return jnp.sum(jnp.where((col == n) & chosen, 1.0, 0.0), axis=1, keepdims=True)

        def max_body(n, m):
            sn = jnp.where(sel_col(n) > 0.0, s_ref[n], NEG_INF)
            return jnp.maximum(m, jnp.max(sn, axis=1, keepdims=True))

        m = lax.fori_loop(0, nbp, max_body, jnp.max(s_own, axis=1, keepdims=True))

        def e_body(n, l):
            e = jnp.where(sel_col(n) > 0.0, jnp.exp(s_ref[n] - m), 0.0)
            s_ref[n] = e
            return l + jnp.sum(e, axis=1, keepdims=True)

        own_mult = 1.0 + sel_col(own_blk)
        e_own = jnp.exp(s_own - m) * own_mult
        l = lax.fori_loop(0, nbp, e_body, jnp.sum(e_own, axis=1, keepdims=True))
        pown_ref[...] = (e_own / l).astype(BF16)
        l_ref[...] = jnp.broadcast_to(l, l_ref.shape)

    @pl.when(step >= k_steps)
    def _():
        nt = (((1,), (1,)), ((), ()))
        acc = acc_ref[...]
        for jb in range(bps):
            p = (s_ref[(step - k_steps) * bps + jb] / l_ref[:, :1]).astype(BF16)
            for half in range(PAGES_PER_BLOCK):
                acc = acc + lax.dot_general(p[:, half * PAGE_SIZE:(half + 1) * PAGE_SIZE],
                                            v_refs[2 * jb + half][...].astype(BF16), nt,
                                            preferred_element_type=F32)
        acc_ref[...] = acc

    @pl.when(step == 2 * k_steps - 1)
    def _():
        vn = jnp.concatenate([vn_ref[...], jnp.zeros((LANES - t, D_MODEL), F32)], axis=0).astype(BF16)
        acc = acc_ref[...] + jnp.dot(pown_ref[...], vn, preferred_element_type=F32)
        acc = head_diag(acc)
        o_ref[...] = jnp.sum(acc.reshape(N_HEADS, t, D_MODEL), axis=0)


def _moba_sample(q, k_new, v_new, kt_pool, vt_pool, page_table, slopes, db, t):
    n_pages = page_table.shape[1]
    assert n_pages % PAGES_PER_BLOCK == 0 and PAGES_PER_BLOCK == 2 and t <= LANES
    nbp = n_pages // PAGES_PER_BLOCK
    bps = min(SAMPLE_BLOCKS_PER_STEP, nbp)
    assert nbp % bps == 0
    k_steps = nbp // bps
    n_pg = bps * PAGES_PER_BLOCK
    hq = N_HEADS * t
    nb_pad = -(-(nbp + 1) // LANES) * LANES
    slope_rows = jnp.broadcast_to(jnp.repeat(slopes, t)[:, None], (hq, MOBA_BLOCK))
    new_rows = pl.BlockSpec((t, D_MODEL), lambda b, s, pt: (b, 0))

    def page(first, j):
        def index(b, s, pt):
            return (pt[b, jnp.clip(s - first, 0, k_steps - 1) * n_pg + j], 0, 0)
        return pl.BlockSpec((None, D_MODEL, PAGE_SIZE), index)

    grid_spec = pltpu.PrefetchScalarGridSpec(
        num_scalar_prefetch=1,
        grid=(db, 2 * k_steps),
        in_specs=[pl.BlockSpec((hq, MOBA_BLOCK), lambda b, s, pt: (0, 0)),
                  new_rows, new_rows, new_rows,
                  *[page(0, j) for j in range(n_pg)], *[page(k_steps, j) for j in range(n_pg)]],
        out_specs=new_rows,
        scratch_shapes=[pltpu.VMEM((hq, D_MODEL), F32),
                        pltpu.VMEM((D_MODEL, nb_pad), F32),
                        pltpu.VMEM((nbp, hq, MOBA_BLOCK), F32),
                        pltpu.VMEM((hq, LANES), BF16),
                        pltpu.VMEM((hq, LANES), F32),
                        pltpu.VMEM((hq, D_MODEL), F32)])
    return pl.pallas_call(
        functools.partial(_moba_sample_kernel, t=t, nbp=nbp, bps=bps),
        out_shape=jax.ShapeDtypeStruct((db * t, D_MODEL), F32),
        grid_spec=grid_spec,
        compiler_params=_params("arbitrary", "arbitrary"),
        name="moba_sample",
    )(page_table, slope_rows, q, k_new, v_new, *([kt_pool] * n_pg), *([vt_pool] * n_pg))


def _rglru_kernel(x_ref, win_ref, bin_ref, conv0_ref, h0_ref, cw_ref, cb_ref, wa_ref, ba_ref, wx_ref, bx_ref,
                  lam_ref, wout_ref, bout_ref, g_ref, beta_ref, *rest, tt):
    o_ref, conv_ref, hlast_ref, halo_ref, h_ref, a_ref, b_ref, hs_ref = rest[-8:]
    i = pl.program_id(1)
    keep = CONV_W - 1
    first = CONV_HALO - keep

    @pl.when(i == 0)
    def _():
        halo_ref[first:CONV_HALO, :] = conv0_ref[...]
        h_ref[...] = h0_ref[...]

    x = x_ref[...]
    u = jnp.dot(x.astype(BF16), win_ref[...], preferred_element_type=F32) + bin_ref[...]
    y_in = u[:, :D_RNN]
    y_branch = 0.5 * y_in * (1.0 + jnp.tanh(0.7978845608028654 * (y_in + 0.044715 * (y_in * y_in * y_in))))
    halo_ref[CONV_HALO:CONV_HALO + tt, :] = u[:, D_RNN:]
    xc = cb_ref[...]
    acc = None
    for j in range(CONV_W):
        term = halo_ref[first + j:first + j + tt, :] * cw_ref[j:j + 1, :]
        acc = term if acc is None else acc + term
    xc = xc + acc

    r_parts, i_parts = [], []
    for blk in range(N_LRU_BLOCKS):
        xb = xc[:, blk * LRU_BLOCK_WIDTH:(blk + 1) * LRU_BLOCK_WIDTH].astype(BF16)
        r_parts.append(jnp.dot(xb, wa_ref[blk], preferred_element_type=F32))
        i_parts.append(jnp.dot(xb, wx_ref[blk], preferred_element_type=F32))
    r = jax.nn.sigmoid(jnp.concatenate(r_parts, axis=1) + ba_ref[...])
    ig = jax.nn.sigmoid(jnp.concatenate(i_parts, axis=1) + bx_ref[...])
    log_a = LRU_C * r * jax.nn.log_sigmoid(lam_ref[...])
    a_ref[...] = jnp.exp(log_a)
    b_ref[...] = jnp.sqrt(1.0 - jnp.exp(2.0 * log_a)) * (ig * xc)

    def step(s, h):
        h = a_ref[pl.ds(s, 1), :] * h + b_ref[pl.ds(s, 1), :]
        hs_ref[pl.ds(s, 1), :] = h
        return h

    h = lax.fori_loop(0, tt, step, h_ref[...], unroll=8)
    h_ref[...] = h
    hy = (hs_ref[...] * y_branch).astype(BF16)
    m = jnp.dot(hy, wout_ref[...], preferred_element_type=F32) + bout_ref[...]
    o_ref[...] = _layer_norm(DEEPNORM_ALPHA * x + m, g_ref[...], beta_ref[...])
    tail = halo_ref[first + tt:CONV_HALO + tt, :]
    halo_ref[first:CONV_HALO, :] = tail

    @pl.when(i == pl.num_programs(1) - 1)
    def _():
        conv_ref[...] = tail
        hlast_ref[...] = h


def _rglru(x, row0, conv0, h0, w_in_bf, b_in, cw, cb, wa_bf, ba, wx_bf, bx, lam, w_out_bf, b_out, g, beta,
           b, t, *, into=None):
    tt = _row_tile(t, 256)
    nt = t // tt
    assert row0 % tt == 0
    blk0 = row0 // tt
    keep = CONV_W - 1
    vec = pl.BlockSpec((1, D_RNN), lambda bi, i: (0, 0))
    gate_w = pl.BlockSpec((N_LRU_BLOCKS, LRU_BLOCK_WIDTH, LRU_BLOCK_WIDTH), lambda bi, i: (0, 0, 0))
    state3 = pl.BlockSpec((None, keep, D_RNN), lambda bi, i: (bi, 0, 0))
    state1 = pl.BlockSpec((None, 1, D_RNN), lambda bi, i: (bi, 0, 0))
    rows = pl.BlockSpec((tt, D_MODEL), lambda bi, i: (blk0 + bi * nt + i, 0))
    whole = lambda shape: pl.BlockSpec(shape, lambda bi, i: (0,) * len(shape))
    in_specs = [rows, whole((D_MODEL, 2 * D_RNN)), whole((1, 2 * D_RNN)),
                state3, state1, whole((CONV_W, D_RNN)), vec,
                gate_w, vec, gate_w, vec, vec,
                whole((D_RNN, D_MODEL)), whole((1, D_MODEL)), whole((1, D_MODEL)), whole((1, D_MODEL))]
    args = [x, w_in_bf, b_in.reshape(1, 2 * D_RNN), conv0, h0.reshape(b, 1, D_RNN), cw, cb.reshape(1, D_RNN),
            wa_bf, ba.reshape(1, D_RNN), wx_bf, bx.reshape(1, D_RNN), lam.reshape(1, D_RNN),
            w_out_bf, b_out.reshape(1, D_MODEL), g.reshape(1, D_MODEL), beta.reshape(1, D_MODEL)]
    aliases = {}
    if into is not None:
        in_specs.append(pl.BlockSpec(memory_space=pl.ANY))
        args.append(into)
        aliases = {len(args) - 1: 0}
    return pl.pallas_call(
        functools.partial(_rglru_kernel, tt=tt),
        out_shape=(jax.ShapeDtypeStruct((x.shape[0], D_MODEL), F32),
                   jax.ShapeDtypeStruct((b, keep, D_RNN), F32),
                   jax.ShapeDtypeStruct((b, 1, D_RNN), F32)),
        grid=(b, nt),
        in_specs=in_specs,
        out_specs=(rows, state3, state1),
        input_output_aliases=aliases,
        scratch_shapes=[pltpu.VMEM((CONV_HALO + tt, D_RNN), F32),
                        pltpu.VMEM((1, D_RNN), F32),
                        pltpu.VMEM((tt, D_RNN), F32),
                        pltpu.VMEM((tt, D_RNN), F32),
                        pltpu.VMEM((tt, D_RNN), F32)],
        compiler_params=_params("arbitrary", "arbitrary"),
        name="rglru",
    )(*args)


def _router_kernel(x_ref, wt_ref, b_ref, ids_ref, gates_ref, rank_ref, counts_ref, carry_ref):
    i = pl.program_id(0)
    tm = x_ref.shape[0]

    @pl.when(i == 0)
    def _():
        carry_ref[...] = jnp.zeros_like(carry_ref)

    logits = lax.dot_general(wt_ref[...].astype(BF16), x_ref[...].astype(BF16), (((1,), (1,)), ((), ())),
                             preferred_element_type=F32) + b_ref[:, :1]
    e_io = lax.broadcasted_iota(jnp.int32, logits.shape, 0)
    lt = logits
    vals, ids = [], []
    for _ in range(TOP_K):
        mx = jnp.max(lt, axis=0, keepdims=True)
        idx = jnp.min(jnp.where(lt == mx, e_io, N_EXPERTS), axis=0, keepdims=True)
        vals.append(mx)
        ids.append(idx)
        lt = jnp.where(e_io == idx, -jnp.inf, lt)
    ex = [jnp.exp(v - vals[0]) for v in vals]
    den = ex[0] + ex[1] + ex[2] + ex[3]
    gates_ref[...] = jnp.concatenate([e / den for e in ex], axis=0)
    ids_ref[...] = jnp.concatenate(ids, axis=0)

    onehot = jnp.zeros(logits.shape, F32)
    for idx in ids:
        onehot = onehot + jnp.where(e_io == idx, 1.0, 0.0)
    s_io = lax.broadcasted_iota(jnp.int32, (tm, tm), 0)
    t_io = lax.broadcasted_iota(jnp.int32, (tm, tm), 1)
    before = jnp.where(s_io < t_io, 1.0, 0.0).astype(BF16)
    prefix = jnp.dot(onehot.astype(BF16), before, preferred_element_type=F32) + carry_ref[:, :1]
    ranks = [jnp.sum(jnp.where(e_io == idx, prefix, 0.0), axis=0, keepdims=True) for idx in ids]
    rank_ref[...] = jnp.concatenate(ranks, axis=0).astype(jnp.int32)
    carry_ref[...] += jnp.sum(onehot, axis=1, keepdims=True)
    counts_ref[...] = carry_ref[...].astype(jnp.int32)


def _router(x, w_router, b_router):
    n = x.shape[0]
    tm = _row_tile(n, ROUTER_TILE)
    slot = pl.BlockSpec((TOP_K, tm), lambda i: (0, i))
    per_expert = pl.BlockSpec((N_EXPERTS, LANES), lambda i: (0, 0))
    return pl.pallas_call(
        _router_kernel,
        out_shape=(jax.ShapeDtypeStruct((TOP_K, n), jnp.int32),
                   jax.ShapeDtypeStruct((TOP_K, n), F32),
                   jax.ShapeDtypeStruct((TOP_K, n), jnp.int32),
                   jax.ShapeDtypeStruct((N_EXPERTS, LANES), jnp.int32)),
        grid=(n // tm,),
        in_specs=[pl.BlockSpec((tm, D_MODEL), lambda i: (i, 0)),
                  pl.BlockSpec((N_EXPERTS, D_MODEL), lambda i: (0, 0)),
                  per_expert],
        out_specs=(slot, slot, slot, per_expert),
        scratch_shapes=[pltpu.VMEM((N_EXPERTS, LANES), F32)],
        compiler_params=_params("arbitrary"),
        name="moe_router",
    )(x, w_router.T, jnp.broadcast_to(b_router[:, None], (N_EXPERTS, LANES)))


def _dispatch_kernel(last_ref, dest_ref, x_ref, xg_hbm, zero_ref, sem, zero_sem):
    tm = dest_ref.shape[1]
    rows = zero_ref.shape[0]

    def zero_copy(e):
        return pltpu.make_async_copy(zero_ref, xg_hbm.at[pl.ds(last_ref[e] * rows, rows)], zero_sem)

    @pl.when(pl.program_id(0) == 0)
    def _():
        zero_ref[...] = jnp.zeros_like(zero_ref)
        for e in range(N_EXPERTS):
            pl.when(last_ref[e] >= 0)(lambda e=e: zero_copy(e).start())
        for e in range(N_EXPERTS):
            pl.when(last_ref[e] >= 0)(lambda e=e: zero_copy(e).wait())

    def row_copy(r, k):
        return pltpu.make_async_copy(x_ref.at[pl.ds(r, 1)], xg_hbm.at[pl.ds(dest_ref[k, r], 1)], sem)

    def issue(r, c):
        for k in range(TOP_K):
            row_copy(r, k).start(priority=k % 2)
        return c

    def drain(r, c):
        for k in range(TOP_K):
            row_copy(r, k).wait()
        return c

    lax.fori_loop(0, tm, issue, 0, unroll=8)
    lax.fori_loop(0, tm, drain, 0, unroll=8)


def _dispatch(x, dest, last_tile, n_rows):
    n = x.shape[0]
    tm = _row_tile(n, DISPATCH_TILE)
    grid_spec = pltpu.PrefetchScalarGridSpec(
        num_scalar_prefetch=1,
        grid=(n // tm,),
        in_specs=[pl.BlockSpec((TOP_K, tm), lambda i, last: (0, i), memory_space=pltpu.SMEM),
                  pl.BlockSpec((tm, D_MODEL), lambda i, last: (i, 0))],
        out_specs=pl.BlockSpec(memory_space=pl.ANY),
        scratch_shapes=[pltpu.VMEM((EXPERT_TILE_ROWS, D_MODEL), F32),
                        pltpu.SemaphoreType.DMA, pltpu.SemaphoreType.DMA])
    return pl.pallas_call(
        _dispatch_kernel,
        out_shape=jax.ShapeDtypeStruct((n_rows, D_MODEL), F32),
        grid_spec=grid_spec,
        compiler_params=_params("arbitrary"),
        name="moe_dispatch",
    )(last_tile, dest, x)


def _expert_kernel(te_ref, tr_ref, xg_ref, wu_ref, bu_ref, wd_ref, bd_ref, y_ref, wu_bf_ref, wd_bf_ref):
    g = pl.program_id(0)
    e = te_ref[g]
    prev = te_ref[jnp.maximum(g - 1, 0)]
    rows = tr_ref[g]
    half = xg_ref.shape[0] // 2

    @pl.when((g == 0) | (e != prev))
    def _():
        wu_bf_ref[...] = wu_ref[...].astype(BF16)
        wd_bf_ref[...] = wd_ref[...].astype(BF16)

    def ffn(x):
        gu = jnp.dot(x.astype(BF16), wu_bf_ref[...], preferred_element_type=F32) + bu_ref[...]
        gate = jnp.minimum(gu[:, :D_EXPERT], SWIGLU_LIMIT)
        up = jnp.clip(gu[:, D_EXPERT:], -SWIGLU_LIMIT, SWIGLU_LIMIT)
        glu = gate * jax.nn.sigmoid(SWIGLU_ALPHA * gate)
        return jnp.dot(((up + 1.0) * glu).astype(BF16), wd_bf_ref[...], preferred_element_type=F32) + bd_ref[...]

    @pl.when(rows > half)
    def _():
        y_ref[...] = ffn(xg_ref[...])

    @pl.when((rows > 0) & (rows <= half))
    def _():
        y_ref[:half, :] = ffn(xg_ref[:half, :])
        y_ref[half:, :] = jnp.zeros((half, D_MODEL), F32)

    @pl.when(rows == 0)
    def _():
        y_ref[...] = jnp.zeros_like(y_ref)


def _experts(tile_e, tile_rows, xg, layer, w_up, b_up, w_down, b_down):
    n_rows = xg.shape[0]
    tm = EXPERT_TILE_ROWS
    n_layers = w_up.shape[0]
    grid_spec = pltpu.PrefetchScalarGridSpec(
        num_scalar_prefetch=2,
        grid=(n_rows // tm,),
        in_specs=[pl.BlockSpec((tm, D_MODEL), lambda g, te, tv: (g * jnp.minimum(tv[g], 1), 0)),
                  pl.BlockSpec((None, None, D_MODEL, 2 * D_EXPERT), lambda g, te, tv: (layer, te[g], 0, 0)),
                  pl.BlockSpec((None, None, 1, 2 * D_EXPERT), lambda g, te, tv: (layer, te[g], 0, 0)),
                  pl.BlockSpec((None, None, D_EXPERT, D_MODEL), lambda g, te, tv: (layer, te[g], 0, 0)),
                  pl.BlockSpec((None, None, 1, D_MODEL), lambda g, te, tv: (layer, te[g], 0, 0))],
        out_specs=pl.BlockSpec((tm, D_MODEL), lambda g, te, tv: (g, 0)),
        scratch_shapes=[pltpu.VMEM((D_MODEL, 2 * D_EXPERT), BF16),
                        pltpu.VMEM((D_EXPERT, D_MODEL), BF16)])
    return pl.pallas_call(
        _expert_kernel,
        out_shape=jax.ShapeDtypeStruct((n_rows, D_MODEL), F32),
        grid_spec=grid_spec,
        compiler_params=_params("arbitrary"),
        name="moe_experts",
    )(tile_e, tile_rows, xg, w_up, b_up.reshape(n_layers, N_EXPERTS, 1, 2 * D_EXPERT),
      w_down, b_down.reshape(n_layers, N_EXPERTS, 1, D_MODEL))


def _combine_ln_kernel(dest_ref, dest_next_ref, x_ref, gate_ref, g_ref, beta_ref, yg_hbm, o_ref, buf_ref, sem):
    i = pl.program_id(0)
    n_steps = pl.num_programs(0)
    tm = x_ref.shape[0]

    def row_copy(d_ref, slot, r, k):
        return pltpu.make_async_copy(yg_hbm.at[pl.ds(d_ref[k, r], 1)],
                                     buf_ref.at[slot, k, pl.ds(r, 1)], sem.at[slot])

    def issue(d_ref, slot):
        def body(r, c):
            for k in range(TOP_K):
                row_copy(d_ref, slot, r, k).start(priority=k % 2)
            return c
        lax.fori_loop(0, tm, body, 0, unroll=8)

    @pl.when(i == 0)
    def _():
        issue(dest_ref, 0)

    @pl.when(i + 1 < n_steps)
    def _():
        issue(dest_next_ref, (i + 1) % 2)

    slot = i % 2

    def drain(r, c):
        for k in range(TOP_K):
            row_copy(dest_ref, slot, r, k).wait()
        return c

    lax.fori_loop(0, tm, drain, 0, unroll=8)
    gates = gate_ref[...]
    y = gates[:, 0:1] * buf_ref[slot, 0]
    for k in range(1, TOP_K):
        y = y + gates[:, k:k + 1] * buf_ref[slot, k]
    o_ref[...] = _layer_norm(DEEPNORM_ALPHA * x_ref[...] + y, g_ref[...], beta_ref[...])


def _combine_ln(x, yg, dest, gates_t, g, beta):
    n = x.shape[0]
    tm = _row_tile(n, COMBINE_TILE)
    n_steps = n // tm
    row = pl.BlockSpec((tm, D_MODEL), lambda i: (i, 0))
    vec = pl.BlockSpec((1, D_MODEL), lambda i: (0, 0))
    return pl.pallas_call(
        _combine_ln_kernel,
        out_shape=jax.ShapeDtypeStruct((n, D_MODEL), F32),
        grid=(n_steps,),
        in_specs=[pl.BlockSpec((TOP_K, tm), lambda i: (0, i), memory_space=pltpu.SMEM),
                  pl.BlockSpec((TOP_K, tm), lambda i: (0, jnp.minimum(i + 1, n_steps - 1)),
                               memory_space=pltpu.SMEM),
                  row, pl.BlockSpec((tm, TOP_K), lambda i: (i, 0)), vec, vec,
                  pl.BlockSpec(memory_space=pl.ANY)],
        out_specs=row,
        scratch_shapes=[pltpu.VMEM((2, TOP_K, tm, D_MODEL), F32), pltpu.SemaphoreType.DMA((2,))],
        compiler_params=_params("arbitrary"),
        name="moe_combine_ln",
    )(dest, dest, x, gates_t, g.reshape(1, D_MODEL), beta.reshape(1, D_MODEL), yg)


def _moe_ln(x, layer, w_router, b_router, w_up, b_up, w_down, b_down, g, beta):
    n = x.shape[0]
    tm = EXPERT_TILE_ROWS
    ids, gates, rank, counts = _router(x, w_router, b_router)
    counts = counts[:, 0]
    padded = (counts + tm - 1) // tm * tm
    pend = jnp.cumsum(padded)
    pstart = pend - padded
    expert_io = jnp.arange(N_EXPERTS, dtype=jnp.int32)[:, None, None]
    dest = jnp.sum(jnp.where(ids[None] == expert_io, pstart[:, None, None], 0), axis=0) + rank
    n_tiles = -(-(n * TOP_K) // tm) + N_EXPERTS
    tile_row0 = jnp.arange(n_tiles, dtype=jnp.int32) * tm
    tile_e = jnp.sum((pend[None, :] <= tile_row0[:, None]).astype(jnp.int32), axis=1)
    tile_e = jnp.minimum(tile_e, N_EXPERTS - 1)
    seg_end = jnp.sum(jnp.where(tile_e[:, None] == expert_io[:, 0, 0][None, :], (pstart + counts)[None, :], 0), axis=1)
    tile_rows = jnp.clip(seg_end - tile_row0, 0, tm).astype(jnp.int32)
    last_tile = jnp.where(padded > 0, pend // tm - 1, -1).astype(jnp.int32)
    xg = _dispatch(x, dest, last_tile, n_tiles * tm)
    yg = _experts(tile_e, tile_rows, xg, layer, w_up, b_up, w_down, b_down)
    return _combine_ln(x, yg, dest, gates.T, g, beta)


def kernel(x_prompt, x_sample, cache_k, cache_v, state_conv, state_h, page_table, ln_g, ln_b, w_qkv, w_o, w_in, b_in, conv_w, conv_b, w_gate_a, b_gate_a, w_gate_x, b_gate_x, lru_lambda, w_out, b_out, w_router, b_router, w_up, b_up, w_down, b_down):
    bp, tp, d = x_prompt.shape
    db, ts, _ = x_sample.shape
    n_p, n_s = bp * tp, db * ts
    xp = x_prompt.reshape(n_p, d)
    xs = x_sample.reshape(n_s, d)
    slopes = 2.0 ** (-8.0 * jnp.arange(1, N_HEADS + 1, dtype=F32) / N_HEADS)
    zeros_d = jnp.zeros((d,), F32)

    def moe_block(x, layer):
        return _moe_ln(x, layer, w_router[layer], b_router[layer], w_up, b_up, w_down, b_down,
                       ln_g[layer, 1], ln_b[layer, 1])

    n = n_p + n_s

    w_qkv_bf = w_qkv[0].astype(BF16)
    w_o_bf = w_o[0].astype(BF16)
    wkt_bf = w_qkv[0][:, d:2 * d].T.astype(BF16)
    wvt_bf = w_qkv[0][:, 2 * d:].T.astype(BF16)
    qp, ktp, vtp = _qkv_prompt(xp, w_qkv_bf[:, :d], wkt_bf, wvt_bf, bp, tp)
    qs, ks, vs = _qkv(xs, w_qkv_bf)
    op = _moba_prompt(qp, ktp, vtp, slopes, bp, tp)
    n_phys = cache_k.shape[1]
    kt_pool = jnp.transpose(cache_k[0], (0, 2, 3, 1)).reshape(n_phys, d, PAGE_SIZE)
    vt_pool = jnp.transpose(cache_v[0], (0, 2, 3, 1)).reshape(n_phys, d, PAGE_SIZE)
    os_ = _moba_sample(qs, ks, vs, kt_pool, vt_pool, page_table, slopes, db, ts)
    x = _linear_res_ln(op, w_o_bf, zeros_d, xp, ln_g[0, 0], ln_b[0, 0], out_rows=n)
    x = _linear_res_ln(os_, w_o_bf, zeros_d, xs, ln_g[0, 0], ln_b[0, 0], out_rows=n, row0=n_p, into=x)
    x = moe_block(x, 0)

    w_in_bf = w_in[0].astype(BF16)
    wa_bf = w_gate_a[0].astype(BF16)
    wx_bf = w_gate_x[0].astype(BF16)
    rec = (w_in_bf, b_in[0], conv_w[0], conv_b[0], wa_bf, b_gate_a[0].reshape(-1), wx_bf, b_gate_x[0].reshape(-1),
           lru_lambda[0], w_out[0].astype(BF16), b_out[0], ln_g[1, 0], ln_b[1, 0])
    conv0 = jnp.zeros((bp, CONV_W - 1, D_RNN), F32)
    h0 = jnp.zeros((bp, D_RNN), F32)
    x1, conv_p, h_p = _rglru(x, 0, conv0, h0, *rec, bp, tp)
    x1, conv_s, h_s = _rglru(x, n_p, state_conv[0], state_h[0], *rec, db, ts, into=x1)
    x = moe_block(x1, 1)
    xp, xs = x[:n_p], x[n_p:]

    def kv_rows(a_t):
        return jnp.transpose(a_t.reshape(1, bp, N_HEADS, HEAD_DIM, tp), (0, 1, 4, 2, 3))

    kv_s = (1, db, ts, N_HEADS, HEAD_DIM)
    return (xp.reshape(bp, tp, d), xs.reshape(db, ts, d),
            kv_rows(ktp), kv_rows(vtp), conv_p[None], h_p.reshape(1, bp, D_RNN),
            ks.reshape(kv_s), vs.reshape(kv_s), conv_s[None], h_s.reshape(1, db, D_RNN))
```

```python
import functools

import jax
import jax.numpy as jnp
from jax import lax
from jax.experimental import pallas as pl
from jax.experimental.pallas import tpu as pltpu

F32 = jnp.float32
BF16 = jnp.bfloat16

D_MODEL = 1024
N_HEADS = 16
HEAD_DIM = D_MODEL // N_HEADS
MOBA_BLOCK = 256
MOBA_TOPK = 3
PAGE_SIZE = 128
PAGES_PER_BLOCK = MOBA_BLOCK // PAGE_SIZE
D_RNN = D_MODEL
LRU_BLOCK_WIDTH = 256
N_LRU_BLOCKS = D_RNN // LRU_BLOCK_WIDTH
CONV_W = 4
LRU_C = 8.0
N_EXPERTS = 32
TOP_K = 4
D_EXPERT = D_MODEL
SWIGLU_LIMIT = 7.0
SWIGLU_ALPHA = 1.702
DEPTH = 2
DEEPNORM_ALPHA = (2 * DEPTH) ** 0.25
LN_EPS = 1e-5
NEG_INF = -1e30

V7X_VMEM_LIMIT_BYTES = 56 * 1024 * 1024
SUBLANES = 8
LANES = 128
CONV_HALO = SUBLANES
EXPERT_TILE_ROWS = 512
ROUTER_TILE = 256
SAMPLE_BLOCKS_PER_STEP = 8
DISPATCH_TILE = 640
COMBINE_TILE = 256


def _params(*sem):
    return pltpu.CompilerParams(dimension_semantics=sem, vmem_limit_bytes=V7X_VMEM_LIMIT_BYTES)


def _row_tile(m, pref):
    t = min(pref, m)
    while m % t or t % SUBLANES:
        t -= SUBLANES
    return t


def _div_pow2(x, n):
    assert n & (n - 1) == 0, n
    return lax.shift_right_logical(x, n.bit_length() - 1)


def _layer_norm(y, g, b):
    mu = jnp.mean(y, axis=-1, keepdims=True)
    yc = y - mu
    var = jnp.mean(yc * yc, axis=-1, keepdims=True)
    return yc * lax.rsqrt(var + LN_EPS) * g + b


def _qkv_kernel(x_ref, w_ref, q_ref, k_ref, v_ref):
    x = x_ref[...].astype(BF16)
    for c, o_ref in enumerate((q_ref, k_ref, v_ref)):
        o_ref[...] = jnp.dot(x, w_ref[:, c * D_MODEL:(c + 1) * D_MODEL], preferred_element_type=F32)


def _qkv(x, w_bf):
    m = x.shape[0]
    tm = _row_tile(m, 512)
    out = jax.ShapeDtypeStruct((m, D_MODEL), F32)
    row = pl.BlockSpec((tm, D_MODEL), lambda i: (i, 0))
    return pl.pallas_call(
        _qkv_kernel,
        out_shape=(out, out, out),
        grid=(m // tm,),
        in_specs=[row, pl.BlockSpec((D_MODEL, 3 * D_MODEL), lambda i: (0, 0))],
        out_specs=(row, row, row),
        compiler_params=_params("arbitrary"),
        name="qkv_proj",
    )(x, w_bf)


def _qkv_prompt_kernel(x_ref, wqt_ref, wk_ref, wkt_ref, wvt_ref, qt_ref, k_ref, kt_ref, vt_ref):
    x = x_ref[...].astype(BF16)
    nt = (((1,), (1,)), ((), ()))
    qt_ref[...] = lax.dot_general(wqt_ref[...], x, nt, preferred_element_type=F32)
    k_ref[...] = jnp.dot(x, wk_ref[...], preferred_element_type=F32)
    kt_ref[...] = lax.dot_general(wkt_ref[...], x, nt, preferred_element_type=F32)
    vt_ref[...] = lax.dot_general(wvt_ref[...], x, nt, preferred_element_type=F32)


def _qkv_prompt(x, wqt_bf, wk_bf, wkt_bf, wvt_bf, b, t):
    tm = _row_tile(t, 512)
    nt = t // tm
    w = pl.BlockSpec((D_MODEL, D_MODEL), lambda bi, i: (0, 0))
    row = pl.BlockSpec((tm, D_MODEL), lambda bi, i: (bi * nt + i, 0))
    col = pl.BlockSpec((None, D_MODEL, tm), lambda bi, i: (bi, 0, i))
    t_out = jax.ShapeDtypeStruct((b, D_MODEL, t), F32)
    return pl.pallas_call(
        _qkv_prompt_kernel,
        out_shape=(t_out, jax.ShapeDtypeStruct((b * t, D_MODEL), F32), t_out, t_out),
        grid=(b, nt),
        in_specs=[row, w, w, w, w],
        out_specs=(col, row, col, col),
        compiler_params=_params("arbitrary", "arbitrary"),
        name="qkv_proj_prompt",
    )(x, wqt_bf, wk_bf, wkt_bf, wvt_bf)


def _linear_res_ln_kernel(a_ref, w_ref, b_ref, res_ref, g_ref, beta_ref, *rest):
    o_ref = rest[-1]
    m = jnp.dot(a_ref[...].astype(BF16), w_ref[...], preferred_element_type=F32) + b_ref[...]
    o_ref[...] = _layer_norm(DEEPNORM_ALPHA * res_ref[...] + m, g_ref[...], beta_ref[...])


def _linear_res_ln(a, w_bf, b, res, g, beta, *, out_rows=None, row0=0, into=None):
    m, k = a.shape
    n = w_bf.shape[1]
    out_rows = m if out_rows is None else out_rows
    tm = _row_tile(m, 512)
    assert row0 % tm == 0
    blk0 = row0 // tm
    row_in = pl.BlockSpec((tm, k), lambda i: (i, 0))
    row = pl.BlockSpec((tm, n), lambda i: (i, 0))
    vec = pl.BlockSpec((1, n), lambda i: (0, 0))
    in_specs = [row_in, pl.BlockSpec((k, n), lambda i: (0, 0)), vec, row, vec, vec]
    args = [a, w_bf, b.reshape(1, n), res, g.reshape(1, n), beta.reshape(1, n)]
    aliases = {}
    if into is not None:
        in_specs.append(pl.BlockSpec(memory_space=pl.ANY))
        args.append(into)
        aliases = {len(args) - 1: 0}
    return pl.pallas_call(
        _linear_res_ln_kernel,
        out_shape=jax.ShapeDtypeStruct((out_rows, n), F32),
        grid=(m // tm,),
        in_specs=in_specs,
        out_specs=pl.BlockSpec((tm, n), lambda i: (blk0 + i, 0)),
        input_output_aliases=aliases,
        compiler_params=_params("arbitrary"),
        name="linear_res_ln",
    )(*args)


def _slot_rank(gm, n_blocks, axis):
    blk = lax.broadcasted_iota(jnp.int32, gm.shape, axis)
    rank = jnp.zeros(gm.shape, jnp.int32)
    for m in range(n_blocks):
        g_m = jnp.sum(jnp.where(blk == m, gm, 0.0), axis=axis, keepdims=True)
        beats = (g_m > gm) | ((g_m == gm) & (m < blk))
        rank = rank + jnp.where(beats, 1, 0)
    return rank


def _moba_prompt_kernel(slope_ref, qt_ref, k_ref, vt_ref, o_ref, *, t):
    nb = t // MOBA_BLOCK
    n_sel = min(MOBA_TOPK, nb)
    bs = MOBA_BLOCK
    hp = pl.program_id(1)
    heads_per_step = LANES // HEAD_DIM

    key_io = lax.broadcasted_iota(jnp.int32, (bs, bs), 0)
    qry_io = lax.broadcasted_iota(jnp.int32, (bs, bs), 1)
    causal_bias = jnp.where(key_io <= qry_io, 0.0, NEG_INF)

    blk = lax.broadcasted_iota(jnp.int32, (nb, t), 0)
    own = _div_pow2(lax.broadcasted_iota(jnp.int32, (nb, t), 1), MOBA_BLOCK)

    def row_of(a, j):
        return jnp.sum(jnp.where(blk == j, a, 0.0), axis=0, keepdims=True)

    outs = []
    for h in range(heads_per_step):
        slope = slope_ref[hp * heads_per_step + h]
        lo, hi = h * HEAD_DIM, (h + 1) * HEAD_DIM
        qt = qt_ref[lo:hi, :]
        kr = k_ref[:, lo:hi]
        vt = vt_ref[lo:hi, :]

        kmean = jnp.sum(kr.reshape(nb, bs, HEAD_DIM), axis=1) * (1.0 / bs)
        g_t = jnp.dot(kmean.astype(BF16), qt.astype(BF16), preferred_element_type=F32)
        gm = jnp.where(blk < own, g_t, NEG_INF)
        rank = _slot_rank(gm, nb, 0)
        chosen = (rank < n_sel) & (rank < own)
        sel_bias = jnp.where(chosen, 0.0, NEG_INF)
        own_mult = jnp.where(chosen, 2.0, 1.0)
        sel_rows = [row_of(sel_bias, j) for j in range(nb)]
        mult_rows = [row_of(own_mult, j) for j in range(nb)]

        qs = (qt * (HEAD_DIM ** -0.5)).astype(BF16)
        kb = kr.astype(BF16)
        vb = vt.astype(BF16)
        alibi = [slope * (key_io + j * bs).astype(F32) for j in range(nb)]
        o_blocks = []
        for i in range(nb):
            cols = slice(i * bs, (i + 1) * bs)
            q_i = qs[:, cols]
            tiles, maxes = [], []
            for j in range(i + 1):
                s_ij = jnp.dot(kb[j * bs:(j + 1) * bs], q_i, preferred_element_type=F32) + alibi[j]
                if j == i:
                    s_ij = s_ij + causal_bias
                mx = jnp.max(s_ij, axis=0, keepdims=True)
                tiles.append(s_ij)
                maxes.append(mx + sel_rows[j][:, cols] if j < i else mx)
            m = functools.reduce(jnp.maximum, maxes)
            ps = [jnp.exp(s_ij + ((sel_rows[j][:, cols] - m) if j < i else -m)) for j, s_ij in enumerate(tiles)]
            ps[-1] = ps[-1] * mult_rows[i][:, cols]
            inv = 1.0 / functools.reduce(jnp.add, [jnp.sum(p, axis=0, keepdims=True) for p in ps])
            p_t = jnp.concatenate([(p * inv).astype(BF16) for p in ps], axis=0)
            o_blocks.append(jnp.dot(vb[:, :(i + 1) * bs], p_t, preferred_element_type=F32))
        outs.append(jnp.concatenate(o_blocks, axis=1))
    o_ref[...] = jnp.transpose(jnp.concatenate(outs, axis=0))


def _moba_prompt(qt, k, vt, slopes, b, t):
    blk = pl.BlockSpec((t, LANES), lambda bi, hp: (bi, hp))
    blk_t = pl.BlockSpec((None, LANES, t), lambda bi, hp: (bi, hp, 0))
    return pl.pallas_call(
        functools.partial(_moba_prompt_kernel, t=t),
        out_shape=jax.ShapeDtypeStruct((b * t, D_MODEL), F32),
        grid=(b, D_MODEL // LANES),
        in_specs=[pl.BlockSpec(memory_space=pltpu.SMEM), blk_t, blk, blk_t],
        out_specs=blk,
        compiler_params=_params("arbitrary", "arbitrary"),
        name="moba_prompt",
    )(slopes, qt, k, vt)


def _moba_sample_kernel(pt_ref, slope_ref, q_ref, kn_ref, vn_ref, *refs, t, nbp, bps):
    del pt_ref
    n_pg = bps * PAGES_PER_BLOCK
    k_refs, v_refs = refs[:n_pg], refs[n_pg:2 * n_pg]
    o_ref, qbd_ref, kmean_ref, s_ref, pown_ref, l_ref, acc_ref = refs[2 * n_pg:]
    k_steps = nbp // bps
    step = pl.program_id(1)
    bs = MOBA_BLOCK
    hq = N_HEADS * t
    past_len = nbp * bs
    own_blk = nbp
    nb = nbp + 1
    n_sel = min(MOBA_TOPK, nb)

    row = lax.broadcasted_iota(jnp.int32, (hq, bs), 0)
    lane = lax.broadcasted_iota(jnp.int32, (hq, bs), 1)
    assert t & (t - 1) == 0, t
    qpos = past_len + (row & (t - 1))
    slope = slope_ref[...]

    def head_diag(x):
        r = _div_pow2(lax.broadcasted_iota(jnp.int32, x.shape, 0), t)
        c = _div_pow2(lax.broadcasted_iota(jnp.int32, x.shape, 1), HEAD_DIM)
        return jnp.where(r == c, x, 0.0)

    @pl.when(step == 0)
    def _():
        q_rep = jnp.concatenate([q_ref[...]] * N_HEADS, axis=0)
        qbd_ref[...] = head_diag(q_rep)
        kmean_ref[...] = jnp.zeros_like(kmean_ref)
        acc_ref[...] = jnp.zeros_like(acc_ref)

    @pl.when(step < k_steps)
    def _():
        qbd = (qbd_ref[...] * (HEAD_DIM ** -0.5)).astype(BF16)
        blk_lane = lax.broadcasted_iota(jnp.int32, kmean_ref.shape, 1)
        kmean_new = kmean_ref[...]
        for jb in range(bps):
            n = step * bps + jb
            ka, kb = k_refs[2 * jb][...], k_refs[2 * jb + 1][...]
            ksum = jnp.sum(ka + kb, axis=1, keepdims=True) * (1.0 / bs)
            kmean_new = kmean_new + jnp.where(blk_lane == n, ksum, 0.0)
            s = jnp.concatenate([jnp.dot(qbd, ka.astype(BF16), preferred_element_type=F32),
                                 jnp.dot(qbd, kb.astype(BF16), preferred_element_type=F32)], axis=1)
            dist = qpos - (n * bs + lane)
            s_ref[n] = s - slope * dist.astype(F32)
        kmean_ref[...] = kmean_new

    @pl.when(step == k_steps - 1)
    def _():
        qbd = qbd_ref[...]
        g = jnp.dot(qbd.astype(BF16), kmean_ref[...].astype(BF16), preferred_element_type=F32)
        col = lax.broadcasted_iota(jnp.int32, g.shape, 1)
        gm = jnp.where(col < own_blk, g, NEG_INF)
        rank = _slot_rank(gm, nb, 1)
        chosen = (rank < n_sel) & (rank < own_blk) & (col < nb)

        kn = jnp.concatenate([kn_ref[...], jnp.zeros((LANES - t, D_MODEL), F32)], axis=0)
        s_own = lax.dot_general((qbd * (HEAD_DIM ** -0.5)).astype(BF16), kn.astype(BF16),
                                (((1,), (1,)), ((), ())), preferred_element_type=F32)
        dist_own = (qpos - (past_len + lane))[:, :LANES]
        s_own = s_own - slope[:, :LANES] * dist_own.astype(F32)
        s_own = jnp.where(dist_own >= 0, s_own, NEG_INF)

        def sel_col(n):
            return jnp.sum(jnp.where((col == n) & chosen, 1.0, 0.0), axis=1, keepdims=True)

        def max_body(n, m):
            sn = jnp.where(sel_col(n) > 0.0, s_ref[n], NEG_INF)
            return jnp.maximum(m, jnp.max(sn, axis=1, keepdims=True))

        m = lax.fori_loop(0, nbp, max_body, jnp.max(s_own, axis=1, keepdims=True))

        def e_body(n, l):
            e = jnp.where(sel_col(n) > 0.0, jnp.exp(s_ref[n] - m), 0.0)
            s_ref[n] = e
            return l + jnp.sum(e, axis=1, keepdims=True)

        own_mult = 1.0 + sel_col(own_blk)
        e_own = jnp.exp(s_own - m) * own_mult
        l = lax.fori_loop(0, nbp, e_body, jnp.sum(e_own, axis=1, keepdims=True))
        pown_ref[...] = (e_own / l).astype(BF16)
        l_ref[...] = jnp.broadcast_to(l, l_ref.shape)

    @pl.when(step >= k_steps)
    def _():
        nt = (((1,), (1,)), ((), ()))
        acc = acc_ref[...]
        for jb in range(bps):
            p = (s_ref[(step - k_steps) * bps + jb] / l_ref[:, :1]).astype(BF16)
            for half in range(PAGES_PER_BLOCK):
                acc = acc + lax.dot_general(p[:, half * PAGE_SIZE:(half + 1) * PAGE_SIZE],
                                            v_refs[2 * jb + half][...].astype(BF16), nt,
                                            preferred_element_type=F32)
        acc_ref[...] = acc

    @pl.when(step == 2 * k_steps - 1)
    def _():
        vn = jnp.concatenate([vn_ref[...], jnp.zeros((LANES - t, D_MODEL), F32)], axis=0).astype(BF16)
        acc = acc_ref[...] + jnp.dot(pown_ref[...], vn, preferred_element_type=F32)
        acc = head_diag(acc)
        o_ref[...] = jnp.sum(acc.reshape(N_HEADS, t, D_MODEL), axis=0)


def _moba_sample(q, k_new, v_new, kt_pool, vt_pool, page_table, slopes, db, t):
    n_pages = page_table.shape[1]
    assert n_pages % PAGES_PER_BLOCK == 0 and PAGES_PER_BLOCK == 2 and t <= LANES
    nbp = n_pages // PAGES_PER_BLOCK
    bps = min(SAMPLE_BLOCKS_PER_STEP, nbp)
    assert nbp % bps == 0
    k_steps = nbp // bps
    n_pg = bps * PAGES_PER_BLOCK
    hq = N_HEADS * t
    nb_pad = -(-(nbp + 1) // LANES) * LANES
    slope_rows = jnp.broadcast_to(jnp.repeat(slopes, t)[:, None], (hq, MOBA_BLOCK))
    new_rows = pl.BlockSpec((t, D_MODEL), lambda b, s, pt: (b, 0))

    def page(first, j):
        def index(b, s, pt):
            return (pt[b, jnp.clip(s - first, 0, k_steps - 1) * n_pg + j], 0, 0)
        return pl.BlockSpec((None, D_MODEL, PAGE_SIZE), index)

    grid_spec = pltpu.PrefetchScalarGridSpec(
        num_scalar_prefetch=1,
        grid=(db, 2 * k_steps),
        in_specs=[pl.BlockSpec((hq, MOBA_BLOCK), lambda b, s, pt: (0, 0)),
                  new_rows, new_rows, new_rows,
                  *[page(0, j) for j in range(n_pg)], *[page(k_steps, j) for j in range(n_pg)]],
        out_specs=new_rows,
        scratch_shapes=[pltpu.VMEM((hq, D_MODEL), F32),
                        pltpu.VMEM((D_MODEL, nb_pad), F32),
                        pltpu.VMEM((nbp, hq, MOBA_BLOCK), F32),
                        pltpu.VMEM((hq, LANES), BF16),
                        pltpu.VMEM((hq, LANES), F32),
                        pltpu.VMEM((hq, D_MODEL), F32)])
    return pl.pallas_call(
        functools.partial(_moba_sample_kernel, t=t, nbp=nbp, bps=bps),
        out_shape=jax.ShapeDtypeStruct((db * t, D_MODEL), F32),
        grid_spec=grid_spec,
        compiler_params=_params("arbitrary", "arbitrary"),
        name="moba_sample",
    )(page_table, slope_rows, q, k_new, v_new, *([kt_pool] * n_pg), *([vt_pool] * n_pg))


def _rglru_kernel(x_ref, win_ref, bin_ref, conv0_ref, h0_ref, cw_ref, cb_ref, wa_ref, ba_ref, wx_ref, bx_ref,
                  lam_ref, wout_ref, bout_ref, g_ref, beta_ref, *rest, tt):
    o_ref, conv_ref, hlast_ref, halo_ref, h_ref, a_ref, b_ref, hs_ref = rest[-8:]
    i = pl.program_id(1)
    keep = CONV_W - 1
    first = CONV_HALO - keep

    @pl.when(i == 0)
    def _():
        halo_ref[first:CONV_HALO, :] = conv0_ref[...]
        h_ref[...] = h0_ref[...]

    x = x_ref[...]
    u = jnp.dot(x.astype(BF16), win_ref[...], preferred_element_type=F32) + bin_ref[...]
    y_in = u[:, :D_RNN]
    y_branch = 0.5 * y_in * (1.0 + jnp.tanh(0.7978845608028654 * (y_in + 0.044715 * (y_in * y_in * y_in))))
    halo_ref[CONV_HALO:CONV_HALO + tt, :] = u[:, D_RNN:]
    xc = cb_ref[...]
    acc = None
    for j in range(CONV_W):
        term = halo_ref[first + j:first + j + tt, :] * cw_ref[j:j + 1, :]
        acc = term if acc is None else acc + term
    xc = xc + acc

    r_parts, i_parts = [], []
    for blk in range(N_LRU_BLOCKS):
        xb = xc[:, blk * LRU_BLOCK_WIDTH:(blk + 1) * LRU_BLOCK_WIDTH].astype(BF16)
        r_parts.append(jnp.dot(xb, wa_ref[blk], preferred_element_type=F32))
        i_parts.append(jnp.dot(xb, wx_ref[blk], preferred_element_type=F32))
    r = jax.nn.sigmoid(jnp.concatenate(r_parts, axis=1) + ba_ref[...])
    ig = jax.nn.sigmoid(jnp.concatenate(i_parts, axis=1) + bx_ref[...])
    log_a = LRU_C * r * jax.nn.log_sigmoid(lam_ref[...])
    a_ref[...] = jnp.exp(log_a)
    b_ref[...] = jnp.sqrt(1.0 - jnp.exp(2.0 * log_a)) * (ig * xc)

    def step(s, h):
        h = a_ref[pl.ds(s, 1), :] * h + b_ref[pl.ds(s, 1), :]
        hs_ref[pl.ds(s, 1), :] = h
        return h

    h = lax.fori_loop(0, tt, step, h_ref[...], unroll=8)
    h_ref[...] = h
    hy = (hs_ref[...] * y_branch).astype(BF16)
    m = jnp.dot(hy, wout_ref[...], preferred_element_type=F32) + bout_ref[...]
    o_ref[...] = _layer_norm(DEEPNORM_ALPHA * x + m, g_ref[...], beta_ref[...])
    tail = halo_ref[first + tt:CONV_HALO + tt, :]
    halo_ref[first:CONV_HALO, :] = tail

    @pl.when(i == pl.num_programs(1) - 1)
    def _():
        conv_ref[...] = tail
        hlast_ref[...] = h


def _rglru(x, row0, conv0, h0, w_in_bf, b_in, cw, cb, wa_bf, ba, wx_bf, bx, lam, w_out_bf, b_out, g, beta,
           b, t, *, into=None):
    tt = _row_tile(t, 256)
    nt = t // tt
    assert row0 % tt == 0
    blk0 = row0 // tt
    keep = CONV_W - 1
    vec = pl.BlockSpec((1, D_RNN), lambda bi, i: (0, 0))
    gate_w = pl.BlockSpec((N_LRU_BLOCKS, LRU_BLOCK_WIDTH, LRU_BLOCK_WIDTH), lambda bi, i: (0, 0, 0))
    state3 = pl.BlockSpec((None, keep, D_RNN), lambda bi, i: (bi, 0, 0))
    state1 = pl.BlockSpec((None, 1, D_RNN), lambda bi, i: (bi, 0, 0))
    rows = pl.BlockSpec((tt, D_MODEL), lambda bi, i: (blk0 + bi * nt + i, 0))
    whole = lambda shape: pl.BlockSpec(shape, lambda bi, i: (0,) * len(shape))
    in_specs = [rows, whole((D_MODEL, 2 * D_RNN)), whole((1, 2 * D_RNN)),
                state3, state1, whole((CONV_W, D_RNN)), vec,
                gate_w, vec, gate_w, vec, vec,
                whole((D_RNN, D_MODEL)), whole((1, D_MODEL)), whole((1, D_MODEL)), whole((1, D_MODEL))]
    args = [x, w_in_bf, b_in.reshape(1, 2 * D_RNN), conv0, h0.reshape(b, 1, D_RNN), cw, cb.reshape(1, D_RNN),
            wa_bf, ba.reshape(1, D_RNN), wx_bf, bx.reshape(1, D_RNN), lam.reshape(1, D_RNN),
            w_out_bf, b_out.reshape(1, D_MODEL), g.reshape(1, D_MODEL), beta.reshape(1, D_MODEL)]
    aliases = {}
    if into is not None:
        in_specs.append(pl.BlockSpec(memory_space=pl.ANY))
        args.append(into)
        aliases = {len(args) - 1: 0}
    return pl.pallas_call(
        functools.partial(_rglru_kernel, tt=tt),
        out_shape=(jax.ShapeDtypeStruct((x.shape[0], D_MODEL), F32),
                   jax.ShapeDtypeStruct((b, keep, D_RNN), F32),
                   jax.ShapeDtypeStruct((b, 1, D_RNN), F32)),
        grid=(b, nt),
        in_specs=in_specs,
        out_specs=(rows, state3, state1),
        input_output_aliases=aliases,
        scratch_shapes=[pltpu.VMEM((CONV_HALO + tt, D_RNN), F32),
                        pltpu.VMEM((1, D_RNN), F32),
                        pltpu.VMEM((tt, D_RNN), F32),
                        pltpu.VMEM((tt, D_RNN), F32),
                        pltpu.VMEM((tt, D_RNN), F32)],
        compiler_params=_params("arbitrary", "arbitrary"),
        name="rglru",
    )(*args)


def _router_kernel(x_ref, wt_ref, b_ref, ids_ref, gates_ref, rank_ref, counts_ref, carry_ref):
    i = pl.program_id(0)
    tm = x_ref.shape[0]

    @pl.when(i == 0)
    def _():
        carry_ref[...] = jnp.zeros_like(carry_ref)

    logits = lax.dot_general(wt_ref[...].astype(BF16), x_ref[...].astype(BF16), (((1,), (1,)), ((), ())),
                             preferred_element_type=F32) + b_ref[:, :1]
    e_io = lax.broadcasted_iota(jnp.int32, logits.shape, 0)
    lt = logits
    vals, ids = [], []
    for _ in range(TOP_K):
        mx = jnp.max(lt, axis=0, keepdims=True)
        idx = jnp.min(jnp.where(lt == mx, e_io, N_EXPERTS), axis=0, keepdims=True)
        vals.append(mx)
        ids.append(idx)
        lt = jnp.where(e_io == idx, -jnp.inf, lt)
    ex = [jnp.exp(v - vals[0]) for v in vals]
    den = ex[0] + ex[1] + ex[2] + ex[3]
    gates_ref[...] = jnp.concatenate([e / den for e in ex], axis=0)
    ids_ref[...] = jnp.concatenate(ids, axis=0)

    onehot = jnp.zeros(logits.shape, F32)
    for idx in ids:
        onehot = onehot + jnp.where(e_io == idx, 1.0, 0.0)
    s_io = lax.broadcasted_iota(jnp.int32, (tm, tm), 0)
    t_io = lax.broadcasted_iota(jnp.int32, (tm, tm), 1)
    before = jnp.where(s_io < t_io, 1.0, 0.0).astype(BF16)
    prefix = jnp.dot(onehot.astype(BF16), before, preferred_element_type=F32) + carry_ref[:, :1]
    ranks = [jnp.sum(jnp.where(e_io == idx, prefix, 0.0), axis=0, keepdims=True) for idx in ids]
    rank_ref[...] = jnp.concatenate(ranks, axis=0).astype(jnp.int32)
    carry_ref[...] += jnp.sum(onehot, axis=1, keepdims=True)
    counts_ref[...] = carry_ref[...].astype(jnp.int32)


def _router(x, w_router, b_router):
    n = x.shape[0]
    tm = _row_tile(n, ROUTER_TILE)
    slot = pl.BlockSpec((TOP_K, tm), lambda i: (0, i))
    per_expert = pl.BlockSpec((N_EXPERTS, LANES), lambda i: (0, 0))
    return pl.pallas_call(
        _router_kernel,
        out_shape=(jax.ShapeDtypeStruct((TOP_K, n), jnp.int32),
                   jax.ShapeDtypeStruct((TOP_K, n), F32),
                   jax.ShapeDtypeStruct((TOP_K, n), jnp.int32),
                   jax.ShapeDtypeStruct((N_EXPERTS, LANES), jnp.int32)),
        grid=(n // tm,),
        in_specs=[pl.BlockSpec((tm, D_MODEL), lambda i: (i, 0)),
                  pl.BlockSpec((N_EXPERTS, D_MODEL), lambda i: (0, 0)),
                  per_expert],
        out_specs=(slot, slot, slot, per_expert),
        scratch_shapes=[pltpu.VMEM((N_EXPERTS, LANES), F32)],
        compiler_params=_params("arbitrary"),
        name="moe_router",
    )(x, w_router.T, jnp.broadcast_to(b_router[:, None], (N_EXPERTS, LANES)))


def _dispatch_kernel(last_ref, dest_ref, x_ref, xg_hbm, zero_ref, sem, zero_sem):
    tm = dest_ref.shape[1]
    rows = zero_ref.shape[0]

    def zero_copy(e):
        return pltpu.make_async_copy(zero_ref, xg_hbm.at[pl.ds(last_ref[e] * rows, rows)], zero_sem)

    @pl.when(pl.program_id(0) == 0)
    def _():
        zero_ref[...] = jnp.zeros_like(zero_ref)
        for e in range(N_EXPERTS):
            pl.when(last_ref[e] >= 0)(lambda e=e: zero_copy(e).start())
        for e in range(N_EXPERTS):
            pl.when(last_ref[e] >= 0)(lambda e=e: zero_copy(e).wait())

    def row_copy(r, k):
        return pltpu.make_async_copy(x_ref.at[pl.ds(r, 1)], xg_hbm.at[pl.ds(dest_ref[k, r], 1)], sem)

    def issue(r, c):
        for k in range(TOP_K):
            row_copy(r, k).start(priority=k % 2)
        return c

    def drain(r, c):
        for k in range(TOP_K):
            row_copy(r, k).wait()
        return c

    lax.fori_loop(0, tm, issue, 0, unroll=8)
    lax.fori_loop(0, tm, drain, 0, unroll=8)


def _dispatch(x, dest, last_tile, n_rows):
    n = x.shape[0]
    tm = _row_tile(n, DISPATCH_TILE)
    grid_spec = pltpu.PrefetchScalarGridSpec(
        num_scalar_prefetch=1,
        grid=(n // tm,),
        in_specs=[pl.BlockSpec((TOP_K, tm), lambda i, last: (0, i), memory_space=pltpu.SMEM),
                  pl.BlockSpec((tm, D_MODEL), lambda i, last: (i, 0))],
        out_specs=pl.BlockSpec(memory_space=pl.ANY),
        scratch_shapes=[pltpu.VMEM((EXPERT_TILE_ROWS, D_MODEL), F32),
                        pltpu.SemaphoreType.DMA, pltpu.SemaphoreType.DMA])
    return pl.pallas_call(
        _dispatch_kernel,
        out_shape=jax.ShapeDtypeStruct((n_rows, D_MODEL), F32),
        grid_spec=grid_spec,
        compiler_params=_params("arbitrary"),
        name="moe_dispatch",
    )(last_tile, dest, x)


def _expert_kernel(te_ref, tr_ref, xg_ref, wu_ref, bu_ref, wd_ref, bd_ref, y_ref, wu_bf_ref, wd_bf_ref):
    g = pl.program_id(0)
    e = te_ref[g]
    prev = te_ref[jnp.maximum(g - 1, 0)]
    rows = tr_ref[g]
    half = xg_ref.shape[0] // 2

    @pl.when((g == 0) | (e != prev))
    def _():
        wu_bf_ref[...] = wu_ref[...].astype(BF16)
        wd_bf_ref[...] = wd_ref[...].astype(BF16)

    def ffn(x):
        gu = jnp.dot(x.astype(BF16), wu_bf_ref[...], preferred_element_type=F32) + bu_ref[...]
        gate = jnp.minimum(gu[:, :D_EXPERT], SWIGLU_LIMIT)
        up = jnp.clip(gu[:, D_EXPERT:], -SWIGLU_LIMIT, SWIGLU_LIMIT)
        glu = gate * jax.nn.sigmoid(SWIGLU_ALPHA * gate)
        return jnp.dot(((up + 1.0) * glu).astype(BF16), wd_bf_ref[...], preferred_element_type=F32) + bd_ref[...]

    @pl.when(rows > half)
    def _():
        y_ref[...] = ffn(xg_ref[...])

    @pl.when((rows > 0) & (rows <= half))
    def _():
        y_ref[:half, :] = ffn(xg_ref[:half, :])
        y_ref[half:, :] = jnp.zeros((half, D_MODEL), F32)

    @pl.when(rows == 0)
    def _():
        y_ref[...] = jnp.zeros_like(y_ref)


def _experts(tile_e, tile_rows, xg, layer, w_up, b_up, w_down, b_down):
    n_rows = xg.shape[0]
    tm = EXPERT_TILE_ROWS
    n_layers = w_up.shape[0]
    grid_spec = pltpu.PrefetchScalarGridSpec(
        num_scalar_prefetch=2,
        grid=(n_rows // tm,),
        in_specs=[pl.BlockSpec((tm, D_MODEL), lambda g, te, tv: (g * jnp.minimum(tv[g], 1), 0)),
                  pl.BlockSpec((None, None, D_MODEL, 2 * D_EXPERT), lambda g, te, tv: (layer, te[g], 0, 0)),
                  pl.BlockSpec((None, None, 1, 2 * D_EXPERT), lambda g, te, tv: (layer, te[g], 0, 0)),
                  pl.BlockSpec((None, None, D_EXPERT, D_MODEL), lambda g, te, tv: (layer, te[g], 0, 0)),
                  pl.BlockSpec((None, None, 1, D_MODEL), lambda g, te, tv: (layer, te[g], 0, 0))],
        out_specs=pl.BlockSpec((tm, D_MODEL), lambda g, te, tv: (g, 0)),
        scratch_shapes=[pltpu.VMEM((D_MODEL, 2 * D_EXPERT), BF16),
                        pltpu.VMEM((D_EXPERT, D_MODEL), BF16)])
    return pl.pallas_call(
        _expert_kernel,
        out_shape=jax.ShapeDtypeStruct((n_rows, D_MODEL), F32),
        grid_spec=grid_spec,
        compiler_params=_params("arbitrary"),
        name="moe_experts",
    )(tile_e, tile_rows, xg, w_up, b_up.reshape(n_layers, N_EXPERTS, 1, 2 * D_EXPERT),
      w_down, b_down.reshape(n_layers, N_EXPERTS, 1, D_MODEL))


def _combine_ln_kernel(dest_ref, dest_next_ref, x_ref, gate_ref, g_ref, beta_ref, yg_hbm, o_ref, buf_ref, sem):
    i = pl.program_id(0)
    n_steps = pl.num_programs(0)
    tm = x_ref.shape[0]

    def row_copy(d_ref, slot, r, k):
        return pltpu.make_async_copy(yg_hbm.at[pl.ds(d_ref[k, r], 1)],
                                     buf_ref.at[slot, k, pl.ds(r, 1)], sem.at[slot])

    def issue(d_ref, slot):
        def body(r, c):
            for k in range(TOP_K):
                row_copy(d_ref, slot, r, k).start(priority=k % 2)
            return c
        lax.fori_loop(0, tm, body, 0, unroll=8)

    @pl.when(i == 0)
    def _():
        issue(dest_ref, 0)

    @pl.when(i + 1 < n_steps)
    def _():
        issue(dest_next_ref, (i + 1) % 2)

    slot = i % 2

    def drain(r, c):
        for k in range(TOP_K):
            row_copy(dest_ref, slot, r, k).wait()
        return c

    lax.fori_loop(0, tm, drain, 0, unroll=8)
    gates = gate_ref[...]
    y = gates[:, 0:1] * buf_ref[slot, 0]
    for k in range(1, TOP_K):
        y = y + gates[:, k:k + 1] * buf_ref[slot, k]
    o_ref[...] = _layer_norm(DEEPNORM_ALPHA * x_ref[...] + y, g_ref[...], beta_ref[...])


def _combine_ln(x, yg, dest, gates_t, g, beta):
    n = x.shape[0]
    tm = _row_tile(n, COMBINE_TILE)
    n_steps = n // tm
    row = pl.BlockSpec((tm, D_MODEL), lambda i: (i, 0))
    vec = pl.BlockSpec((1, D_MODEL), lambda i: (0, 0))
    return pl.pallas_call(
        _combine_ln_kernel,
        out_shape=jax.ShapeDtypeStruct((n, D_MODEL), F32),
        grid=(n_steps,),
        in_specs=[pl.BlockSpec((TOP_K, tm), lambda i: (0, i), memory_space=pltpu.SMEM),
                  pl.BlockSpec((TOP_K, tm), lambda i: (0, jnp.minimum(i + 1, n_steps - 1)),
                               memory_space=pltpu.SMEM),
                  row, pl.BlockSpec((tm, TOP_K), lambda i: (i, 0)), vec, vec,
                  pl.BlockSpec(memory_space=pl.ANY)],
        out_specs=row,
        scratch_shapes=[pltpu.VMEM((2, TOP_K, tm, D_MODEL), F32), pltpu.SemaphoreType.DMA((2,))],
        compiler_params=_params("arbitrary"),
        name="moe_combine_ln",
    )(dest, dest, x, gates_t, g.reshape(1, D_MODEL), beta.reshape(1, D_MODEL), yg)


def _moe_ln(x, layer, w_router, b_router, w_up, b_up, w_down, b_down, g, beta):
    n = x.shape[0]
    tm = EXPERT_TILE_ROWS
    ids, gates, rank, counts = _router(x, w_router, b_router)
    counts = counts[:, 0]
    padded = (counts + tm - 1) // tm * tm
    pend = jnp.cumsum(padded)
    pstart = pend - padded
    expert_io = jnp.arange(N_EXPERTS, dtype=jnp.int32)[:, None, None]
    dest = jnp.sum(jnp.where(ids[None] == expert_io, pstart[:, None, None], 0), axis=0) + rank
    n_tiles = -(-(n * TOP_K) // tm) + N_EXPERTS
    tile_row0 = jnp.arange(n_tiles, dtype=jnp.int32) * tm
    tile_e = jnp.sum((pend[None, :] <= tile_row0[:, None]).astype(jnp.int32), axis=1)
    tile_e = jnp.minimum(tile_e, N_EXPERTS - 1)
    seg_end = jnp.sum(jnp.where(tile_e[:, None] == expert_io[:, 0, 0][None, :], (pstart + counts)[None, :], 0), axis=1)
    tile_rows = jnp.clip(seg_end - tile_row0, 0, tm).astype(jnp.int32)
    last_tile = jnp.where(padded > 0, pend // tm - 1, -1).astype(jnp.int32)
    xg = _dispatch(x, dest, last_tile, n_tiles * tm)
    yg = _experts(tile_e, tile_rows, xg, layer, w_up, b_up, w_down, b_down)
    return _combine_ln(x, yg, dest, gates.T, g, beta)


def kernel(x_prompt, x_sample, cache_k, cache_v, state_conv, state_h, page_table, ln_g, ln_b, w_qkv, w_o, w_in, b_in, conv_w, conv_b, w_gate_a, b_gate_a, w_gate_x, b_gate_x, lru_lambda, w_out, b_out, w_router, b_router, w_up, b_up, w_down, b_down):
    bp, tp, d = x_prompt.shape
    db, ts, _ = x_sample.shape
    n_p, n_s = bp * tp, db * ts
    xp = x_prompt.reshape(n_p, d)
    xs = x_sample.reshape(n_s, d)
    slopes = 2.0 ** (-8.0 * jnp.arange(1, N_HEADS + 1, dtype=F32) / N_HEADS)
    zeros_d = jnp.zeros((d,), F32)

    def moe_block(x, layer):
        return _moe_ln(x, layer, w_router[layer], b_router[layer], w_up, b_up, w_down, b_down,
                       ln_g[layer, 1], ln_b[layer, 1])

    n = n_p + n_s

    w_qkv_bf = w_qkv[0].astype(BF16)
    w_o_bf = w_o[0].astype(BF16)
    wkt_bf = w_qkv[0][:, d:2 * d].T.astype(BF16)
    wvt_bf = w_qkv[0][:, 2 * d:].T.astype(BF16)
    wqt_bf = w_qkv[0][:, :d].T.astype(BF16)
    qtp, kp, ktp, vtp = _qkv_prompt(xp, wqt_bf, w_qkv_bf[:, d:2 * d], wkt_bf, wvt_bf, bp, tp)
    qs, ks, vs = _qkv(xs, w_qkv_bf)
    op = _moba_prompt(qtp, kp, vtp, slopes, bp, tp)
    n_phys = cache_k.shape[1]
    kt_pool = jnp.transpose(cache_k[0], (0, 2, 3, 1)).reshape(n_phys, d, PAGE_SIZE)
    vt_pool = jnp.transpose(cache_v[0], (0, 2, 3, 1)).reshape(n_phys, d, PAGE_SIZE)
    os_ = _moba_sample(qs, ks, vs, kt_pool, vt_pool, page_table, slopes, db, ts)
    x = _linear_res_ln(op, w_o_bf, zeros_d, xp, ln_g[0, 0], ln_b[0, 0], out_rows=n)
    x = _linear_res_ln(os_, w_o_bf, zeros_d, xs, ln_g[0, 0], ln_b[0, 0], out_rows=n, row0=n_p, into=x)
    x = moe_block(x, 0)

    w_in_bf = w_in[0].astype(BF16)
    wa_bf = w_gate_a[0].astype(BF16)
    wx_bf = w_gate_x[0].astype(BF16)
    rec = (w_in_bf, b_in[0], conv_w[0], conv_b[0], wa_bf, b_gate_a[0].reshape(-1), wx_bf, b_gate_x[0].reshape(-1),
           lru_lambda[0], w_out[0].astype(BF16), b_out[0], ln_g[1, 0], ln_b[1, 0])
    conv0 = jnp.zeros((bp, CONV_W - 1, D_RNN), F32)
    h0 = jnp.zeros((bp, D_RNN), F32)
    x1, conv_p, h_p = _rglru(x, 0, conv0, h0, *rec, bp, tp)
    x1, conv_s, h_s = _rglru(x, n_p, state_conv[0], state_h[0], *rec, db, ts, into=x1)
    x = moe_block(x1, 1)
    xp, xs = x[:n_p], x[n_p:]

    def kv_rows(a_t):
        return jnp.transpose(a_t.reshape(1, bp, N_HEADS, HEAD_DIM, tp), (0, 1, 4, 2, 3))

    kv_s = (1, db, ts, N_HEADS, HEAD_DIM)
    return (xp.reshape(bp, tp, d), xs.reshape(db, ts, d),
            kv_rows(ktp), kv_rows(vtp), conv_p[None], h_p.reshape(1, bp, D_RNN),
            ks.reshape(kv_s), vs.reshape(kv_s), conv_s[None], h_s.reshape(1, db, D_RNN))
```

```python
import functools

import jax
import jax.numpy as jnp
from jax import lax
from jax.experimental import pallas as pl
from jax.experimental.pallas import tpu as pltpu

F32 = jnp.float32
BF16 = jnp.bfloat16

D_MODEL = 1024
N_HEADS = 16
HEAD_DIM = D_MODEL // N_HEADS
MOBA_BLOCK = 256
MOBA_TOPK = 3
PAGE_SIZE = 128
PAGES_PER_BLOCK = MOBA_BLOCK // PAGE_SIZE
D_RNN = D_MODEL
LRU_BLOCK_WIDTH = 256
N_LRU_BLOCKS = D_RNN // LRU_BLOCK_WIDTH
CONV_W = 4
LRU_C = 8.0
N_EXPERTS = 32
TOP_K = 4
D_EXPERT = D_MODEL
SWIGLU_LIMIT = 7.0
SWIGLU_ALPHA = 1.702
DEPTH = 2
DEEPNORM_ALPHA = (2 * DEPTH) ** 0.25
LN_EPS = 1e-5
NEG_INF = -1e30

V7X_VMEM_LIMIT_BYTES = 56 * 1024 * 1024
SUBLANES = 8
LANES = 128
CONV_HALO = SUBLANES
EXPERT_TILE_ROWS = 512
ROUTER_TILE = 256
SAMPLE_BLOCKS_PER_STEP = 8
MOE_TOKEN_TILE = 256


def _params(*sem):
    return pltpu.CompilerParams(dimension_semantics=sem, vmem_limit_bytes=V7X_VMEM_LIMIT_BYTES)


def _row_tile(m, pref):
    t = min(pref, m)
    while m % t or t % SUBLANES:
        t -= SUBLANES
    return t


def _div_pow2(x, n):
    assert n & (n - 1) == 0, n
    return lax.shift_right_logical(x, n.bit_length() - 1)


def _layer_norm(y, g, b):
    mu = jnp.mean(y, axis=-1, keepdims=True)
    yc = y - mu
    var = jnp.mean(yc * yc, axis=-1, keepdims=True)
    return yc * lax.rsqrt(var + LN_EPS) * g + b


def _qkv_kernel(x_ref, w_ref, q_ref, k_ref, v_ref):
    x = x_ref[...].astype(BF16)
    for c, o_ref in enumerate((q_ref, k_ref, v_ref)):
        o_ref[...] = jnp.dot(x, w_ref[:, c * D_MODEL:(c + 1) * D_MODEL], preferred_element_type=F32)


def _qkv(x, w_bf):
    m = x.shape[0]
    tm = _row_tile(m, 512)
    out = jax.ShapeDtypeStruct((m, D_MODEL), F32)
    row = pl.BlockSpec((tm, D_MODEL), lambda i: (i, 0))
    return pl.pallas_call(
        _qkv_kernel,
        out_shape=(out, out, out),
        grid=(m // tm,),
        in_specs=[row, pl.BlockSpec((D_MODEL, 3 * D_MODEL), lambda i: (0, 0))],
        out_specs=(row, row, row),
        compiler_params=_params("arbitrary"),
        name="qkv_proj",
    )(x, w_bf)


def _qkv_prompt_kernel(x_ref, wqt_ref, wk_ref, wkt_ref, wvt_ref, qt_ref, k_ref, kt_ref, vt_ref):
    x = x_ref[...].astype(BF16)
    nt = (((1,), (1,)), ((), ()))
    qt_ref[...] = lax.dot_general(wqt_ref[...], x, nt, preferred_element_type=F32)
    k_ref[...] = jnp.dot(x, wk_ref[...], preferred_element_type=F32)
    kt_ref[...] = lax.dot_general(wkt_ref[...], x, nt, preferred_element_type=F32)
    vt_ref[...] = lax.dot_general(wvt_ref[...], x, nt, preferred_element_type=F32)


def _qkv_prompt(x, wqt_bf, wk_bf, wkt_bf, wvt_bf, b, t):
    tm = _row_tile(t, 512)
    nt = t // tm
    w = pl.BlockSpec((D_MODEL, D_MODEL), lambda bi, i: (0, 0))
    row = pl.BlockSpec((tm, D_MODEL), lambda bi, i: (bi * nt + i, 0))
    col = pl.BlockSpec((None, D_MODEL, tm), lambda bi, i: (bi, 0, i))
    t_out = jax.ShapeDtypeStruct((b, D_MODEL, t), F32)
    return pl.pallas_call(
        _qkv_prompt_kernel,
        out_shape=(t_out, jax.ShapeDtypeStruct((b * t, D_MODEL), F32), t_out, t_out),
        grid=(b, nt),
        in_specs=[row, w, w, w, w],
        out_specs=(col, row, col, col),
        compiler_params=_params("arbitrary", "arbitrary"),
        name="qkv_proj_prompt",
    )(x, wqt_bf, wk_bf, wkt_bf, wvt_bf)


def _linear_res_ln_kernel(a_ref, w_ref, b_ref, res_ref, g_ref, beta_ref, *rest):
    o_ref = rest[-1]
    m = jnp.dot(a_ref[...].astype(BF16), w_ref[...], preferred_element_type=F32) + b_ref[...]
    o_ref[...] = _layer_norm(DEEPNORM_ALPHA * res_ref[...] + m, g_ref[...], beta_ref[...])


def _linear_res_ln(a, w_bf, b, res, g, beta, *, out_rows=None, row0=0, into=None):
    m, k = a.shape
    n = w_bf.shape[1]
    out_rows = m if out_rows is None else out_rows
    tm = _row_tile(m, 512)
    assert row0 % tm == 0
    blk0 = row0 // tm
    row_in = pl.BlockSpec((tm, k), lambda i: (i, 0))
    row = pl.BlockSpec((tm, n), lambda i: (i, 0))
    vec = pl.BlockSpec((1, n), lambda i: (0, 0))
    in_specs = [row_in, pl.BlockSpec((k, n), lambda i: (0, 0)), vec, row, vec, vec]
    args = [a, w_bf, b.reshape(1, n), res, g.reshape(1, n), beta.reshape(1, n)]
    aliases = {}
    if into is not None:
        in_specs.append(pl.BlockSpec(memory_space=pl.ANY))
        args.append(into)
        aliases = {len(args) - 1: 0}
    return pl.pallas_call(
        _linear_res_ln_kernel,
        out_shape=jax.ShapeDtypeStruct((out_rows, n), F32),
        grid=(m // tm,),
        in_specs=in_specs,
        out_specs=pl.BlockSpec((tm, n), lambda i: (blk0 + i, 0)),
        input_output_aliases=aliases,
        compiler_params=_params("arbitrary"),
        name="linear_res_ln",
    )(*args)


def _slot_rank(gm, n_blocks, axis):
    blk = lax.broadcasted_iota(jnp.int32, gm.shape, axis)
    rank = jnp.zeros(gm.shape, jnp.int32)
    for m in range(n_blocks):
        g_m = jnp.sum(jnp.where(blk == m, gm, 0.0), axis=axis, keepdims=True)
        beats = (g_m > gm) | ((g_m == gm) & (m < blk))
        rank = rank + jnp.where(beats, 1, 0)
    return rank


def _moba_prompt_kernel(slope_ref, qt_ref, k_ref, vt_ref, o_ref, *, t):
    nb = t // MOBA_BLOCK
    n_sel = min(MOBA_TOPK, nb)
    bs = MOBA_BLOCK
    hp = pl.program_id(1)
    heads_per_step = LANES // HEAD_DIM

    key_io = lax.broadcasted_iota(jnp.int32, (bs, bs), 0)
    qry_io = lax.broadcasted_iota(jnp.int32, (bs, bs), 1)
    causal_bias = jnp.where(key_io <= qry_io, 0.0, NEG_INF)

    blk = lax.broadcasted_iota(jnp.int32, (nb, t), 0)
    own = _div_pow2(lax.broadcasted_iota(jnp.int32, (nb, t), 1), MOBA_BLOCK)

    def row_of(a, j):
        return jnp.sum(jnp.where(blk == j, a, 0.0), axis=0, keepdims=True)

    outs = []
    for h in range(heads_per_step):
        slope = slope_ref[hp * heads_per_step + h]
        lo, hi = h * HEAD_DIM, (h + 1) * HEAD_DIM
        qt = qt_ref[lo:hi, :]
        kr = k_ref[:, lo:hi]
        vt = vt_ref[lo:hi, :]

        kmean = jnp.sum(kr.reshape(nb, bs, HEAD_DIM), axis=1) * (1.0 / bs)
        g_t = jnp.dot(kmean.astype(BF16), qt.astype(BF16), preferred_element_type=F32)
        gm = jnp.where(blk < own, g_t, NEG_INF)
        rank = _slot_rank(gm, nb, 0)
        chosen = (rank < n_sel) & (rank < own)
        sel_bias = jnp.where(chosen, 0.0, NEG_INF)
        own_mult = jnp.where(chosen, 2.0, 1.0)
        sel_rows = [row_of(sel_bias, j) for j in range(nb)]
        mult_rows = [row_of(own_mult, j) for j in range(nb)]

        qs = (qt * (HEAD_DIM ** -0.5)).astype(BF16)
        kb = kr.astype(BF16)
        vb = vt.astype(BF16)
        alibi = [slope * (key_io + j * bs).astype(F32) for j in range(nb)]
        o_blocks = []
        for i in range(nb):
            cols = slice(i * bs, (i + 1) * bs)
            q_i = qs[:, cols]
            tiles, maxes = [], []
            for j in range(i + 1):
                s_ij = jnp.dot(kb[j * bs:(j + 1) * bs], q_i, preferred_element_type=F32) + alibi[j]
                if j == i:
                    s_ij = s_ij + causal_bias
                mx = jnp.max(s_ij, axis=0, keepdims=True)
                tiles.append(s_ij)
                maxes.append(mx + sel_rows[j][:, cols] if j < i else mx)
            m = functools.reduce(jnp.maximum, maxes)
            ps = [jnp.exp(s_ij + ((sel_rows[j][:, cols] - m) if j < i else -m)) for j, s_ij in enumerate(tiles)]
            ps[-1] = ps[-1] * mult_rows[i][:, cols]
            inv = 1.0 / functools.reduce(jnp.add, [jnp.sum(p, axis=0, keepdims=True) for p in ps])
            p_t = jnp.concatenate([(p * inv).astype(BF16) for p in ps], axis=0)
            o_blocks.append(jnp.dot(vb[:, :(i + 1) * bs], p_t, preferred_element_type=F32))
        outs.append(jnp.concatenate(o_blocks, axis=1))
    o_ref[...] = jnp.transpose(jnp.concatenate(outs, axis=0))


def _moba_prompt(qt, k, vt, slopes, b, t):
    blk = pl.BlockSpec((t, LANES), lambda bi, hp: (bi, hp))
    blk_t = pl.BlockSpec((None, LANES, t), lambda bi, hp: (bi, hp, 0))
    return pl.pallas_call(
        functools.partial(_moba_prompt_kernel, t=t),
        out_shape=jax.ShapeDtypeStruct((b * t, D_MODEL), F32),
        grid=(b, D_MODEL // LANES),
        in_specs=[pl.BlockSpec(memory_space=pltpu.SMEM), blk_t, blk, blk_t],
        out_specs=blk,
        compiler_params=_params("arbitrary", "arbitrary"),
        name="moba_prompt",
    )(slopes, qt, k, vt)


def _moba_sample_kernel(pt_ref, slope_ref, q_ref, kn_ref, vn_ref, *refs, t, nbp, bps):
    del pt_ref
    n_pg = bps * PAGES_PER_BLOCK
    k_refs, v_refs = refs[:n_pg], refs[n_pg:2 * n_pg]
    o_ref, qbd_ref, kmean_ref, s_ref, pown_ref, l_ref, acc_ref = refs[2 * n_pg:]
    k_steps = nbp // bps
    step = pl.program_id(1)
    bs = MOBA_BLOCK
    hq = N_HEADS * t
    past_len = nbp * bs
    own_blk = nbp
    nb = nbp + 1
    n_sel = min(MOBA_TOPK, nb)

    row = lax.broadcasted_iota(jnp.int32, (hq, bs), 0)
    lane = lax.broadcasted_iota(jnp.int32, (hq, bs), 1)
    assert t & (t - 1) == 0, t
    qpos = past_len + (row & (t - 1))
    slope = slope_ref[...]

    def head_diag(x):
        r = _div_pow2(lax.broadcasted_iota(jnp.int32, x.shape, 0), t)
        c = _div_pow2(lax.broadcasted_iota(jnp.int32, x.shape, 1), HEAD_DIM)
        return jnp.where(r == c, x, 0.0)

    @pl.when(step == 0)
    def _():
        q_rep = jnp.concatenate([q_ref[...]] * N_HEADS, axis=0)
        qbd_ref[...] = head_diag(q_rep)
        kmean_ref[...] = jnp.zeros_like(kmean_ref)
        acc_ref[...] = jnp.zeros_like(acc_ref)

    @pl.when(step < k_steps)
    def _():
        qbd = (qbd_ref[...] * (HEAD_DIM ** -0.5)).astype(BF16)
        blk_lane = lax.broadcasted_iota(jnp.int32, kmean_ref.shape, 1)
        kmean_new = kmean_ref[...]
        for jb in range(bps):
            n = step * bps + jb
            ka, kb = k_refs[2 * jb][...], k_refs[2 * jb + 1][...]
            ksum = jnp.sum(ka + kb, axis=1, keepdims=True) * (1.0 / bs)
            kmean_new = kmean_new + jnp.where(blk_lane == n, ksum, 0.0)
            s = jnp.concatenate([jnp.dot(qbd, ka.astype(BF16), preferred_element_type=F32),
                                 jnp.dot(qbd, kb.astype(BF16), preferred_element_type=F32)], axis=1)
            dist = qpos - (n * bs + lane)
            s_ref[n] = s - slope * dist.astype(F32)
        kmean_ref[...] = kmean_new

    @pl.when(step == k_steps - 1)
    def _():
        qbd = qbd_ref[...]
        g = jnp.dot(qbd.astype(BF16), kmean_ref[...].astype(BF16), preferred_element_type=F32)
        col = lax.broadcasted_iota(jnp.int32, g.shape, 1)
        gm = jnp.where(col < own_blk, g, NEG_INF)
        rank = _slot_rank(gm, nb, 1)
        chosen = (rank < n_sel) & (rank < own_blk) & (col < nb)

        kn = jnp.concatenate([kn_ref[...], jnp.zeros((LANES - t, D_MODEL), F32)], axis=0)
        s_own = lax.dot_general((qbd * (HEAD_DIM ** -0.5)).astype(BF16), kn.astype(BF16),
                                (((1,), (1,)), ((), ())), preferred_element_type=F32)
        dist_own = (qpos - (past_len + lane))[:, :LANES]
        s_own = s_own - slope[:, :LANES] * dist_own.astype(F32)
        s_own = jnp.where(dist_own >= 0, s_own, NEG_INF)

        def sel_col(n):
            return jnp.sum(jnp.where((col == n) & chosen, 1.0, 0.0), axis=1, keepdims=True)

        def max_body(n, m):
            sn = jnp.where(sel_col(n) > 0.0, s_ref[n], NEG_INF)
            return jnp.maximum(m, jnp.max(sn, axis=1, keepdims=True))

        m = lax.fori_loop(0, nbp, max_body, jnp.max(s_own, axis=1, keepdims=True))

        def e_body(n, l):
            e = jnp.where(sel_col(n) > 0.0, jnp.exp(s_ref[n] - m), 0.0)
            s_ref[n] = e
            return l + jnp.sum(e, axis=1, keepdims=True)

        own_mult = 1.0 + sel_col(own_blk)
        e_own = jnp.exp(s_own - m) * own_mult
        l = lax.fori_loop(0, nbp, e_body, jnp.sum(e_own, axis=1, keepdims=True))
        pown_ref[...] = (e_own / l).astype(BF16)
        l_ref[...] = jnp.broadcast_to(l, l_ref.shape)

    @pl.when(step >= k_steps)
    def _():
        nt = (((1,), (1,)), ((), ()))
        acc = acc_ref[...]
        for jb in range(bps):
            p = (s_ref[(step - k_steps) * bps + jb] / l_ref[:, :1]).astype(BF16)
            for half in range(PAGES_PER_BLOCK):
                acc = acc + lax.dot_general(p[:, half * PAGE_SIZE:(half + 1) * PAGE_SIZE],
                                            v_refs[2 * jb + half][...].astype(BF16), nt,
                                            preferred_element_type=F32)
        acc_ref[...] = acc

    @pl.when(step == 2 * k_steps - 1)
    def _():
        vn = jnp.concatenate([vn_ref[...], jnp.zeros((LANES - t, D_MODEL), F32)], axis=0).astype(BF16)
        acc = acc_ref[...] + jnp.dot(pown_ref[...], vn, preferred_element_type=F32)
        acc = head_diag(acc)
        o_ref[...] = jnp.sum(acc.reshape(N_HEADS, t, D_MODEL), axis=0)


def _moba_sample(q, k_new, v_new, kt_pool, vt_pool, page_table, slopes, db, t):
    n_pages = page_table.shape[1]
    assert n_pages % PAGES_PER_BLOCK == 0 and PAGES_PER_BLOCK == 2 and t <= LANES
    nbp = n_pages // PAGES_PER_BLOCK
    bps = min(SAMPLE_BLOCKS_PER_STEP, nbp)
    assert nbp % bps == 0
    k_steps = nbp // bps
    n_pg = bps * PAGES_PER_BLOCK
    hq = N_HEADS * t
    nb_pad = -(-(nbp + 1) // LANES) * LANES
    slope_rows = jnp.broadcast_to(jnp.repeat(slopes, t)[:, None], (hq, MOBA_BLOCK))
    new_rows = pl.BlockSpec((t, D_MODEL), lambda b, s, pt: (b, 0))

    def page(first, j):
        def index(b, s, pt):
            return (pt[b, jnp.clip(s - first, 0, k_steps - 1) * n_pg + j], 0, 0)
        return pl.BlockSpec((None, D_MODEL, PAGE_SIZE), index)

    grid_spec = pltpu.PrefetchScalarGridSpec(
        num_scalar_prefetch=1,
        grid=(db, 2 * k_steps),
        in_specs=[pl.BlockSpec((hq, MOBA_BLOCK), lambda b, s, pt: (0, 0)),
                  new_rows, new_rows, new_rows,
                  *[page(0, j) for j in range(n_pg)], *[page(k_steps, j) for j in range(n_pg)]],
        out_specs=new_rows,
        scratch_shapes=[pltpu.VMEM((hq, D_MODEL), F32),
                        pltpu.VMEM((D_MODEL, nb_pad), F32),
                        pltpu.VMEM((nbp, hq, MOBA_BLOCK), F32),
                        pltpu.VMEM((hq, LANES), BF16),
                        pltpu.VMEM((hq, LANES), F32),
                        pltpu.VMEM((hq, D_MODEL), F32)])
    return pl.pallas_call(
        functools.partial(_moba_sample_kernel, t=t, nbp=nbp, bps=bps),
        out_shape=jax.ShapeDtypeStruct((db * t, D_MODEL), F32),
        grid_spec=grid_spec,
        compiler_params=_params("arbitrary", "arbitrary"),
        name="moba_sample",
    )(page_table, slope_rows, q, k_new, v_new, *([kt_pool] * n_pg), *([vt_pool] * n_pg))


def _rglru_kernel(x_ref, win_ref, bin_ref, conv0_ref, h0_ref, cw_ref, cb_ref, wa_ref, ba_ref, wx_ref, bx_ref,
                  lam_ref, wout_ref, bout_ref, g_ref, beta_ref, *rest, tt):
    o_ref, conv_ref, hlast_ref, halo_ref, h_ref, a_ref, b_ref, hs_ref = rest[-8:]
    i = pl.program_id(1)
    keep = CONV_W - 1
    first = CONV_HALO - keep

    @pl.when(i == 0)
    def _():
        halo_ref[first:CONV_HALO, :] = conv0_ref[...]
        h_ref[...] = h0_ref[...]

    x = x_ref[...]
    u = jnp.dot(x.astype(BF16), win_ref[...], preferred_element_type=F32) + bin_ref[...]
    y_in = u[:, :D_RNN]
    y_branch = 0.5 * y_in * (1.0 + jnp.tanh(0.7978845608028654 * (y_in + 0.044715 * (y_in * y_in * y_in))))
    halo_ref[CONV_HALO:CONV_HALO + tt, :] = u[:, D_RNN:]
    xc = cb_ref[...]
    acc = None
    for j in range(CONV_W):
        term = halo_ref[first + j:first + j + tt, :] * cw_ref[j:j + 1, :]
        acc = term if acc is None else acc + term
    xc = xc + acc

    r_parts, i_parts = [], []
    for blk in range(N_LRU_BLOCKS):
        xb = xc[:, blk * LRU_BLOCK_WIDTH:(blk + 1) * LRU_BLOCK_WIDTH].astype(BF16)
        r_parts.append(jnp.dot(xb, wa_ref[blk], preferred_element_type=F32))
        i_parts.append(jnp.dot(xb, wx_ref[blk], preferred_element_type=F32))
    r = jax.nn.sigmoid(jnp.concatenate(r_parts, axis=1) + ba_ref[...])
    ig = jax.nn.sigmoid(jnp.concatenate(i_parts, axis=1) + bx_ref[...])
    log_a = LRU_C * r * jax.nn.log_sigmoid(lam_ref[...])
    a_ref[...] = jnp.exp(log_a)
    b_ref[...] = jnp.sqrt(1.0 - jnp.exp(2.0 * log_a)) * (ig * xc)

    def step(s, h):
        h = a_ref[pl.ds(s, 1), :] * h + b_ref[pl.ds(s, 1), :]
        hs_ref[pl.ds(s, 1), :] = h
        return h

    h = lax.fori_loop(0, tt, step, h_ref[...], unroll=8)
    h_ref[...] = h
    hy = (hs_ref[...] * y_branch).astype(BF16)
    m = jnp.dot(hy, wout_ref[...], preferred_element_type=F32) + bout_ref[...]
    o_ref[...] = _layer_norm(DEEPNORM_ALPHA * x + m, g_ref[...], beta_ref[...])
    tail = halo_ref[first + tt:CONV_HALO + tt, :]
    halo_ref[first:CONV_HALO, :] = tail

    @pl.when(i == pl.num_programs(1) - 1)
    def _():
        conv_ref[...] = tail
        hlast_ref[...] = h


def _rglru(x, row0, conv0, h0, w_in_bf, b_in, cw, cb, wa_bf, ba, wx_bf, bx, lam, w_out_bf, b_out, g, beta,
           b, t, *, into=None):
    tt = _row_tile(t, 256)
    nt = t // tt
    assert row0 % tt == 0
    blk0 = row0 // tt
    keep = CONV_W - 1
    vec = pl.BlockSpec((1, D_RNN), lambda bi, i: (0, 0))
    gate_w = pl.BlockSpec((N_LRU_BLOCKS, LRU_BLOCK_WIDTH, LRU_BLOCK_WIDTH), lambda bi, i: (0, 0, 0))
    state3 = pl.BlockSpec((None, keep, D_RNN), lambda bi, i: (bi, 0, 0))
    state1 = pl.BlockSpec((None, 1, D_RNN), lambda bi, i: (bi, 0, 0))
    rows = pl.BlockSpec((tt, D_MODEL), lambda bi, i: (blk0 + bi * nt + i, 0))
    whole = lambda shape: pl.BlockSpec(shape, lambda bi, i: (0,) * len(shape))
    in_specs = [rows, whole((D_MODEL, 2 * D_RNN)), whole((1, 2 * D_RNN)),
                state3, state1, whole((CONV_W, D_RNN)), vec,
                gate_w, vec, gate_w, vec, vec,
                whole((D_RNN, D_MODEL)), whole((1, D_MODEL)), whole((1, D_MODEL)), whole((1, D_MODEL))]
    args = [x, w_in_bf, b_in.reshape(1, 2 * D_RNN), conv0, h0.reshape(b, 1, D_RNN), cw, cb.reshape(1, D_RNN),
            wa_bf, ba.reshape(1, D_RNN), wx_bf, bx.reshape(1, D_RNN), lam.reshape(1, D_RNN),
            w_out_bf, b_out.reshape(1, D_MODEL), g.reshape(1, D_MODEL), beta.reshape(1, D_MODEL)]
    aliases = {}
    if into is not None:
        in_specs.append(pl.BlockSpec(memory_space=pl.ANY))
        args.append(into)
        aliases = {len(args) - 1: 0}
    return pl.pallas_call(
        functools.partial(_rglru_kernel, tt=tt),
        out_shape=(jax.ShapeDtypeStruct((x.shape[0], D_MODEL), F32),
                   jax.ShapeDtypeStruct((b, keep, D_RNN), F32),
                   jax.ShapeDtypeStruct((b, 1, D_RNN), F32)),
        grid=(b, nt),
        in_specs=in_specs,
        out_specs=(rows, state3, state1),
        input_output_aliases=aliases,
        scratch_shapes=[pltpu.VMEM((CONV_HALO + tt, D_RNN), F32),
                        pltpu.VMEM((1, D_RNN), F32),
                        pltpu.VMEM((tt, D_RNN), F32),
                        pltpu.VMEM((tt, D_RNN), F32),
                        pltpu.VMEM((tt, D_RNN), F32)],
        compiler_params=_params("arbitrary", "arbitrary"),
        name="rglru",
    )(*args)


def _router_kernel(x_ref, wt_ref, b_ref, ids_ref, gates_ref, rank_ref, counts_ref, carry_ref):
    i = pl.program_id(0)
    tm = x_ref.shape[0]

    @pl.when(i == 0)
    def _():
        carry_ref[...] = jnp.zeros_like(carry_ref)

    logits = lax.dot_general(wt_ref[...].astype(BF16), x_ref[...].astype(BF16), (((1,), (1,)), ((), ())),
                             preferred_element_type=F32) + b_ref[:, :1]
    e_io = lax.broadcasted_iota(jnp.int32, logits.shape, 0)
    lt = logits
    vals, ids = [], []
    for _ in range(TOP_K):
        mx = jnp.max(lt, axis=0, keepdims=True)
        idx = jnp.min(jnp.where(lt == mx, e_io, N_EXPERTS), axis=0, keepdims=True)
        vals.append(mx)
        ids.append(idx)
        lt = jnp.where(e_io == idx, -jnp.inf, lt)
    ex = [jnp.exp(v - vals[0]) for v in vals]
    den = ex[0] + ex[1] + ex[2] + ex[3]
    gates_ref[...] = jnp.concatenate([e / den for e in ex], axis=0)
    ids_ref[...] = jnp.concatenate(ids, axis=0)

    onehot = jnp.zeros(logits.shape, F32)
    for idx in ids:
        onehot = onehot + jnp.where(e_io == idx, 1.0, 0.0)
    s_io = lax.broadcasted_iota(jnp.int32, (tm, tm), 0)
    t_io = lax.broadcasted_iota(jnp.int32, (tm, tm), 1)
    before = jnp.where(s_io < t_io, 1.0, 0.0).astype(BF16)
    prefix = jnp.dot(onehot.astype(BF16), before, preferred_element_type=F32) + carry_ref[:, :1]
    ranks = [jnp.sum(jnp.where(e_io == idx, prefix, 0.0), axis=0, keepdims=True) for idx in ids]
    rank_ref[...] = jnp.concatenate(ranks, axis=0).astype(jnp.int32)
    carry_ref[...] += jnp.sum(onehot, axis=1, keepdims=True)
    counts_ref[...] = carry_ref[...].astype(jnp.int32)


def _router(x, w_router, b_router):
    n = x.shape[0]
    tm = _row_tile(n, ROUTER_TILE)
    slot = pl.BlockSpec((TOP_K, tm), lambda i: (0, i))
    per_expert = pl.BlockSpec((N_EXPERTS, LANES), lambda i: (0, 0))
    return pl.pallas_call(
        _router_kernel,
        out_shape=(jax.ShapeDtypeStruct((TOP_K, n), jnp.int32),
                   jax.ShapeDtypeStruct((TOP_K, n), F32),
                   jax.ShapeDtypeStruct((TOP_K, n), jnp.int32),
                   jax.ShapeDtypeStruct((N_EXPERTS, LANES), jnp.int32)),
        grid=(n // tm,),
        in_specs=[pl.BlockSpec((tm, D_MODEL), lambda i: (i, 0)),
                  pl.BlockSpec((N_EXPERTS, D_MODEL), lambda i: (0, 0)),
                  per_expert],
        out_specs=(slot, slot, slot, per_expert),
        scratch_shapes=[pltpu.VMEM((N_EXPERTS, LANES), F32)],
        compiler_params=_params("arbitrary"),
        name="moe_router",
    )(x, w_router.T, jnp.broadcast_to(b_router[:, None], (N_EXPERTS, LANES)))


def _dispatch_kernel(last_ref, dest_ref, x_ref, xg_hbm, zero_ref, sem, zero_sem):
    tm = x_ref.shape[0]
    rows = zero_ref.shape[0]

    def zero_copy(e):
        return pltpu.make_async_copy(zero_ref, xg_hbm.at[pl.ds(last_ref[e] * rows, rows)], zero_sem)

    @pl.when(pl.program_id(0) == 0)
    def _():
        zero_ref[...] = jnp.zeros_like(zero_ref)
        for e in range(N_EXPERTS):
            pl.when(last_ref[e] >= 0)(lambda e=e: zero_copy(e).start())
        for e in range(N_EXPERTS):
            pl.when(last_ref[e] >= 0)(lambda e=e: zero_copy(e).wait())

    def row_copy(r, k):
        return pltpu.make_async_copy(x_ref.at[pl.ds(r, 1)], xg_hbm.at[pl.ds(dest_ref[k * tm + r], 1)], sem)

    def issue(r, c):
        for k in range(TOP_K):
            row_copy(r, k).start(priority=k % 2)
        return c

    def drain(r, c):
        for k in range(TOP_K):
            row_copy(r, k).wait()
        return c

    lax.fori_loop(0, tm, issue, 0, unroll=8)
    lax.fori_loop(0, tm, drain, 0, unroll=8)


def _dispatch(x, dest, last_tile, n_rows):
    n = x.shape[0]
    tm = _row_tile(n, MOE_TOKEN_TILE)
    grid_spec = pltpu.PrefetchScalarGridSpec(
        num_scalar_prefetch=1,
        grid=(n // tm,),
        in_specs=[pl.BlockSpec((TOP_K * tm,), lambda i, last: (i,), memory_space=pltpu.SMEM),
                  pl.BlockSpec((tm, D_MODEL), lambda i, last: (i, 0))],
        out_specs=pl.BlockSpec(memory_space=pl.ANY),
        scratch_shapes=[pltpu.VMEM((EXPERT_TILE_ROWS, D_MODEL), F32),
                        pltpu.SemaphoreType.DMA, pltpu.SemaphoreType.DMA])
    return pl.pallas_call(
        _dispatch_kernel,
        out_shape=jax.ShapeDtypeStruct((n_rows, D_MODEL), F32),
        grid_spec=grid_spec,
        compiler_params=_params("arbitrary"),
        name="moe_dispatch",
    )(last_tile, dest, x)


def _expert_kernel(te_ref, tr_ref, xg_ref, wu_ref, bu_ref, wd_ref, bd_ref, y_ref, wu_bf_ref, wd_bf_ref):
    g = pl.program_id(0)
    e = te_ref[g]
    prev = te_ref[jnp.maximum(g - 1, 0)]
    rows = tr_ref[g]
    half = xg_ref.shape[0] // 2

    @pl.when((g == 0) | (e != prev))
    def _():
        wu_bf_ref[...] = wu_ref[...].astype(BF16)
        wd_bf_ref[...] = wd_ref[...].astype(BF16)

    def ffn(x):
        gu = jnp.dot(x.astype(BF16), wu_bf_ref[...], preferred_element_type=F32) + bu_ref[...]
        gate = jnp.minimum(gu[:, :D_EXPERT], SWIGLU_LIMIT)
        up = jnp.clip(gu[:, D_EXPERT:], -SWIGLU_LIMIT, SWIGLU_LIMIT)
        glu = gate * jax.nn.sigmoid(SWIGLU_ALPHA * gate)
        return jnp.dot(((up + 1.0) * glu).astype(BF16), wd_bf_ref[...], preferred_element_type=F32) + bd_ref[...]

    @pl.when(rows > half)
    def _():
        y_ref[...] = ffn(xg_ref[...])

    @pl.when((rows > 0) & (rows <= half))
    def _():
        y_ref[:half, :] = ffn(xg_ref[:half, :])
        y_ref[half:, :] = jnp.zeros((half, D_MODEL), F32)

    @pl.when(rows == 0)
    def _():
        y_ref[...] = jnp.zeros_like(y_ref)


def _experts(tile_e, tile_rows, xg, layer, w_up, b_up, w_down, b_down):
    n_rows = xg.shape[0]
    tm = EXPERT_TILE_ROWS
    n_layers = w_up.shape[0]
    grid_spec = pltpu.PrefetchScalarGridSpec(
        num_scalar_prefetch=2,
        grid=(n_rows // tm,),
        in_specs=[pl.BlockSpec((tm, D_MODEL), lambda g, te, tv: (g * jnp.minimum(tv[g], 1), 0)),
                  pl.BlockSpec((None, None, D_MODEL, 2 * D_EXPERT), lambda g, te, tv: (layer, te[g], 0, 0)),
                  pl.BlockSpec((None, None, 1, 2 * D_EXPERT), lambda g, te, tv: (layer, te[g], 0, 0)),
                  pl.BlockSpec((None, None, D_EXPERT, D_MODEL), lambda g, te, tv: (layer, te[g], 0, 0)),
                  pl.BlockSpec((None, None, 1, D_MODEL), lambda g, te, tv: (layer, te[g], 0, 0))],
        out_specs=pl.BlockSpec((tm, D_MODEL), lambda g, te, tv: (g, 0)),
        scratch_shapes=[pltpu.VMEM((D_MODEL, 2 * D_EXPERT), BF16),
                        pltpu.VMEM((D_EXPERT, D_MODEL), BF16)])
    return pl.pallas_call(
        _expert_kernel,
        out_shape=jax.ShapeDtypeStruct((n_rows, D_MODEL), F32),
        grid_spec=grid_spec,
        compiler_params=_params("arbitrary"),
        name="moe_experts",
    )(tile_e, tile_rows, xg, w_up, b_up.reshape(n_layers, N_EXPERTS, 1, 2 * D_EXPERT),
      w_down, b_down.reshape(n_layers, N_EXPERTS, 1, D_MODEL))


def _combine_ln_kernel(dest_ref, dest_next_ref, x_ref, gate_ref, g_ref, beta_ref, yg_hbm, o_ref, buf_ref, sem):
    i = pl.program_id(0)
    n_steps = pl.num_programs(0)
    tm = x_ref.shape[0]

    def row_copy(d_ref, slot, r, k):
        return pltpu.make_async_copy(yg_hbm.at[pl.ds(d_ref[k * tm + r], 1)],
                                     buf_ref.at[slot, k, pl.ds(r, 1)], sem.at[slot])

    def issue(d_ref, slot):
        def body(r, c):
            for k in range(TOP_K):
                row_copy(d_ref, slot, r, k).start(priority=k % 2)
            return c
        lax.fori_loop(0, tm, body, 0, unroll=8)

    @pl.when(i == 0)
    def _():
        issue(dest_ref, 0)

    @pl.when(i + 1 < n_steps)
    def _():
        issue(dest_next_ref, (i + 1) % 2)

    slot = i % 2

    def drain(r, c):
        for k in range(TOP_K):
            row_copy(dest_ref, slot, r, k).wait()
        return c

    lax.fori_loop(0, tm, drain, 0, unroll=8)
    gates = gate_ref[...]
    y = gates[:, 0:1] * buf_ref[slot, 0]
    for k in range(1, TOP_K):
        y = y + gates[:, k:k + 1] * buf_ref[slot, k]
    o_ref[...] = _layer_norm(DEEPNORM_ALPHA * x_ref[...] + y, g_ref[...], beta_ref[...])


def _combine_ln(x, yg, dest, gates_t, g, beta):
    n = x.shape[0]
    tm = _row_tile(n, MOE_TOKEN_TILE)
    n_steps = n // tm
    row = pl.BlockSpec((tm, D_MODEL), lambda i: (i, 0))
    vec = pl.BlockSpec((1, D_MODEL), lambda i: (0, 0))
    return pl.pallas_call(
        _combine_ln_kernel,
        out_shape=jax.ShapeDtypeStruct((n, D_MODEL), F32),
        grid=(n_steps,),
        in_specs=[pl.BlockSpec((TOP_K * tm,), lambda i: (i,), memory_space=pltpu.SMEM),
                  pl.BlockSpec((TOP_K * tm,), lambda i: (jnp.minimum(i + 1, n_steps - 1),),
                               memory_space=pltpu.SMEM),
                  row, pl.BlockSpec((tm, TOP_K), lambda i: (i, 0)), vec, vec,
                  pl.BlockSpec(memory_space=pl.ANY)],
        out_specs=row,
        scratch_shapes=[pltpu.VMEM((2, TOP_K, tm, D_MODEL), F32), pltpu.SemaphoreType.DMA((2,))],
        compiler_params=_params("arbitrary"),
        name="moe_combine_ln",
    )(dest, dest, x, gates_t, g.reshape(1, D_MODEL), beta.reshape(1, D_MODEL), yg)


def _moe_ln(x, layer, w_router, b_router, w_up, b_up, w_down, b_down, g, beta):
    n = x.shape[0]
    tm = EXPERT_TILE_ROWS
    ids, gates, rank, counts = _router(x, w_router, b_router)
    counts = counts[:, 0]
    padded = (counts + tm - 1) // tm * tm
    pend = jnp.cumsum(padded)
    pstart = pend - padded
    expert_io = jnp.arange(N_EXPERTS, dtype=jnp.int32)[:, None, None]
    dest = jnp.sum(jnp.where(ids[None] == expert_io, pstart[:, None, None], 0), axis=0) + rank
    n_tiles = -(-(n * TOP_K) // tm) + N_EXPERTS
    tile_row0 = jnp.arange(n_tiles, dtype=jnp.int32) * tm
    tile_e = jnp.sum((pend[None, :] <= tile_row0[:, None]).astype(jnp.int32), axis=1)
    tile_e = jnp.minimum(tile_e, N_EXPERTS - 1)
    seg_end = jnp.sum(jnp.where(tile_e[:, None] == expert_io[:, 0, 0][None, :], (pstart + counts)[None, :], 0), axis=1)
    tile_rows = jnp.clip(seg_end - tile_row0, 0, tm).astype(jnp.int32)
    last_tile = jnp.where(padded > 0, pend // tm - 1, -1).astype(jnp.int32)
    tok_tile = _row_tile(n, MOE_TOKEN_TILE)
    dest_tiles = dest.reshape(TOP_K, n // tok_tile, tok_tile).transpose(1, 0, 2).reshape(-1)
    xg = _dispatch(x, dest_tiles, last_tile, n_tiles * tm)
    yg = _experts(tile_e, tile_rows, xg, layer, w_up, b_up, w_down, b_down)
    return _combine_ln(x, yg, dest_tiles, gates.T, g, beta)


def kernel(x_prompt, x_sample, cache_k, cache_v, state_conv, state_h, page_table, ln_g, ln_b, w_qkv, w_o, w_in, b_in, conv_w, conv_b, w_gate_a, b_gate_a, w_gate_x, b_gate_x, lru_lambda, w_out, b_out, w_router, b_router, w_up, b_up, w_down, b_down):
    bp, tp, d = x_prompt.shape
    db, ts, _ = x_sample.shape
    n_p, n_s = bp * tp, db * ts
    xp = x_prompt.reshape(n_p, d)
    xs = x_sample.reshape(n_s, d)
    slopes = 2.0 ** (-8.0 * jnp.arange(1, N_HEADS + 1, dtype=F32) / N_HEADS)
    zeros_d = jnp.zeros((d,), F32)

    def moe_block(x, layer):
        return _moe_ln(x, layer, w_router[layer], b_router[layer], w_up, b_up, w_down, b_down,
                       ln_g[layer, 1], ln_b[layer, 1])

    n = n_p + n_s

    w_qkv_bf = w_qkv[0].astype(BF16)
    w_o_bf = w_o[0].astype(BF16)
    wkt_bf = w_qkv[0][:, d:2 * d].T.astype(BF16)
    wvt_bf = w_qkv[0][:, 2 * d:].T.astype(BF16)
    wqt_bf = w_qkv[0][:, :d].T.astype(BF16)
    qtp, kp, ktp, vtp = _qkv_prompt(xp, wqt_bf, w_qkv_bf[:, d:2 * d], wkt_bf, wvt_bf, bp, tp)
    qs, ks, vs = _qkv(xs, w_qkv_bf)
    op = _moba_prompt(qtp, kp, vtp, slopes, bp, tp)
    n_phys = cache_k.shape[1]
    kt_pool = jnp.transpose(cache_k[0], (0, 2, 3, 1)).reshape(n_phys, d, PAGE_SIZE)
    vt_pool = jnp.transpose(cache_v[0], (0, 2, 3, 1)).reshape(n_phys, d, PAGE_SIZE)
    os_ = _moba_sample(qs, ks, vs, kt_pool, vt_pool, page_table, slopes, db, ts)
    x = _linear_res_ln(op, w_o_bf, zeros_d, xp, ln_g[0, 0], ln_b[0, 0], out_rows=n)
    x = _linear_res_ln(os_, w_o_bf, zeros_d, xs, ln_g[0, 0], ln_b[0, 0], out_rows=n, row0=n_p, into=x)
    x = moe_block(x, 0)

    w_in_bf = w_in[0].astype(BF16)
    wa_bf = w_gate_a[0].astype(BF16)
    wx_bf = w_gate_x[0].astype(BF16)
    rec = (w_in_bf, b_in[0], conv_w[0], conv_b[0], wa_bf, b_gate_a[0].reshape(-1), wx_bf, b_gate_x[0].reshape(-1),
           lru_lambda[0], w_out[0].astype(BF16), b_out[0], ln_g[1, 0], ln_b[1, 0])
    conv0 = jnp.zeros((bp, CONV_W - 1, D_RNN), F32)
    h0 = jnp.zeros((bp, D_RNN), F32)
    x1, conv_p, h_p = _rglru(x, 0, conv0, h0, *rec, bp, tp)
    x1, conv_s, h_s = _rglru(x, n_p, state_conv[0], state_h[0], *rec, db, ts, into=x1)
    x = moe_block(x1, 1)
    xp, xs = x[:n_p], x[n_p:]

    def kv_rows(a_t):
        return jnp.transpose(a_t.reshape(1, bp, N_HEADS, HEAD_DIM, tp), (0, 1, 4, 2, 3))

    kv_s = (1, db, ts, N_HEADS, HEAD_DIM)
    return (xp.reshape(bp, tp, d), xs.reshape(db, ts, d),
            kv_rows(ktp), kv_rows(vtp), conv_p[None], h_p.reshape(1, bp, D_RNN),
            ks.reshape(kv_s), vs.reshape(kv_s), conv_s[None], h_s.reshape(1, db, D_RNN))
```

```python
import functools

import jax
import jax.numpy as jnp
from jax import lax
from jax.experimental import pallas as pl
from jax.experimental.pallas import tpu as pltpu

F32 = jnp.float32
BF16 = jnp.bfloat16

D_MODEL = 1024
N_HEADS = 16
HEAD_DIM = D_MODEL // N_HEADS
MOBA_BLOCK = 256
MOBA_TOPK = 3
PAGE_SIZE = 128
PAGES_PER_BLOCK = MOBA_BLOCK // PAGE_SIZE
D_RNN = D_MODEL
LRU_BLOCK_WIDTH = 256
N_LRU_BLOCKS = D_RNN // LRU_BLOCK_WIDTH
CONV_W = 4
LRU_C = 8.0
N_EXPERTS = 32
TOP_K = 4
D_EXPERT = D_MODEL
SWIGLU_LIMIT = 7.0
SWIGLU_ALPHA = 1.702
DEPTH = 2
DEEPNORM_ALPHA = (2 * DEPTH) ** 0.25
LN_EPS = 1e-5
NEG_INF = -1e30

V7X_VMEM_LIMIT_BYTES = 56 * 1024 * 1024
SUBLANES = 8
LANES = 128
CONV_HALO = SUBLANES
EXPERT_TILE_ROWS = 512
ROUTER_TILE = 256
SAMPLE_BLOCKS_PER_STEP = 8
DISPATCH_TOKEN_TILE = 1280
COMBINE_TOKEN_TILE = 256


def _params(*sem):
    return pltpu.CompilerParams(dimension_semantics=sem, vmem_limit_bytes=V7X_VMEM_LIMIT_BYTES)


def _row_tile(m, pref):
    t = min(pref, m)
    while m % t or t % SUBLANES:
        t -= SUBLANES
    return t


def _div_pow2(x, n):
    assert n & (n - 1) == 0, n
    return lax.shift_right_logical(x, n.bit_length() - 1)


def _layer_norm(y, g, b):
    mu = jnp.mean(y, axis=-1, keepdims=True)
    yc = y - mu
    var = jnp.mean(yc * yc, axis=-1, keepdims=True)
    return yc * lax.rsqrt(var + LN_EPS) * g + b


def _qkv_kernel(x_ref, w_ref, q_ref, k_ref, v_ref):
    x = x_ref[...].astype(BF16)
    for c, o_ref in enumerate((q_ref, k_ref, v_ref)):
        o_ref[...] = jnp.dot(x, w_ref[:, c * D_MODEL:(c + 1) * D_MODEL], preferred_element_type=F32)


def _qkv(x, w_bf):
    m = x.shape[0]
    tm = _row_tile(m, 512)
    out = jax.ShapeDtypeStruct((m, D_MODEL), F32)
    row = pl.BlockSpec((tm, D_MODEL), lambda i: (i, 0))
    return pl.pallas_call(
        _qkv_kernel,
        out_shape=(out, out, out),
        grid=(m // tm,),
        in_specs=[row, pl.BlockSpec((D_MODEL, 3 * D_MODEL), lambda i: (0, 0))],
        out_specs=(row, row, row),
        compiler_params=_params("arbitrary"),
        name="qkv_proj",
    )(x, w_bf)


def _qkv_prompt_kernel(x_ref, wqt_ref, wk_ref, wkt_ref, wvt_ref, qt_ref, k_ref, kt_ref, vt_ref):
    x = x_ref[...].astype(BF16)
    nt = (((1,), (1,)), ((), ()))
    qt_ref[...] = lax.dot_general(wqt_ref[...], x, nt, preferred_element_type=F32)
    k_ref[...] = jnp.dot(x, wk_ref[...], preferred_element_type=F32)
    kt_ref[...] = lax.dot_general(wkt_ref[...], x, nt, preferred_element_type=F32)
    vt_ref[...] = lax.dot_general(wvt_ref[...], x, nt, preferred_element_type=F32)


def _qkv_prompt(x, wqt_bf, wk_bf, wkt_bf, wvt_bf, b, t):
    tm = _row_tile(t, 512)
    nt = t // tm
    w = pl.BlockSpec((D_MODEL, D_MODEL), lambda bi, i: (0, 0))
    row = pl.BlockSpec((tm, D_MODEL), lambda bi, i: (bi * nt + i, 0))
    col = pl.BlockSpec((None, D_MODEL, tm), lambda bi, i: (bi, 0, i))
    t_out = jax.ShapeDtypeStruct((b, D_MODEL, t), F32)
    return pl.pallas_call(
        _qkv_prompt_kernel,
        out_shape=(t_out, jax.ShapeDtypeStruct((b * t, D_MODEL), F32), t_out, t_out),
        grid=(b, nt),
        in_specs=[row, w, w, w, w],
        out_specs=(col, row, col, col),
        compiler_params=_params("arbitrary", "arbitrary"),
        name="qkv_proj_prompt",
    )(x, wqt_bf, wk_bf, wkt_bf, wvt_bf)


def _linear_res_ln_kernel(a_ref, w_ref, b_ref, res_ref, g_ref, beta_ref, *rest):
    o_ref = rest[-1]
    m = jnp.dot(a_ref[...].astype(BF16), w_ref[...], preferred_element_type=F32) + b_ref[...]
    o_ref[...] = _layer_norm(DEEPNORM_ALPHA * res_ref[...] + m, g_ref[...], beta_ref[...])


def _linear_res_ln(a, w_bf, b, res, g, beta, *, out_rows=None, row0=0, into=None):
    m, k = a.shape
    n = w_bf.shape[1]
    out_rows = m if out_rows is None else out_rows
    tm = _row_tile(m, 512)
    assert row0 % tm == 0
    blk0 = row0 // tm
    row_in = pl.BlockSpec((tm, k), lambda i: (i, 0))
    row = pl.BlockSpec((tm, n), lambda i: (i, 0))
    vec = pl.BlockSpec((1, n), lambda i: (0, 0))
    in_specs = [row_in, pl.BlockSpec((k, n), lambda i: (0, 0)), vec, row, vec, vec]
    args = [a, w_bf, b.reshape(1, n), res, g.reshape(1, n), beta.reshape(1, n)]
    aliases = {}
    if into is not None:
        in_specs.append(pl.BlockSpec(memory_space=pl.ANY))
        args.append(into)
        aliases = {len(args) - 1: 0}
    return pl.pallas_call(
        _linear_res_ln_kernel,
        out_shape=jax.ShapeDtypeStruct((out_rows, n), F32),
        grid=(m // tm,),
        in_specs=in_specs,
        out_specs=pl.BlockSpec((tm, n), lambda i: (blk0 + i, 0)),
        input_output_aliases=aliases,
        compiler_params=_params("arbitrary"),
        name="linear_res_ln",
    )(*args)


def _slot_rank(gm, n_blocks, axis):
    blk = lax.broadcasted_iota(jnp.int32, gm.shape, axis)
    rank = jnp.zeros(gm.shape, jnp.int32)
    for m in range(n_blocks):
        g_m = jnp.sum(jnp.where(blk == m, gm, 0.0), axis=axis, keepdims=True)
        beats = (g_m > gm) | ((g_m == gm) & (m < blk))
        rank = rank + jnp.where(beats, 1, 0)
    return rank


def _moba_prompt_kernel(slope_ref, qt_ref, k_ref, vt_ref, o_ref, *, t):
    nb = t // MOBA_BLOCK
    n_sel = min(MOBA_TOPK, nb)
    bs = MOBA_BLOCK
    hp = pl.program_id(1)
    heads_per_step = LANES // HEAD_DIM

    key_io = lax.broadcasted_iota(jnp.int32, (bs, bs), 0)
    qry_io = lax.broadcasted_iota(jnp.int32, (bs, bs), 1)
    causal_bias = jnp.where(key_io <= qry_io, 0.0, NEG_INF)

    blk = lax.broadcasted_iota(jnp.int32, (nb, t), 0)
    own = _div_pow2(lax.broadcasted_iota(jnp.int32, (nb, t), 1), MOBA_BLOCK)

    def row_of(a, j):
        return jnp.sum(jnp.where(blk == j, a, 0.0), axis=0, keepdims=True)

    outs = []
    for h in range(heads_per_step):
        slope = slope_ref[hp * heads_per_step + h]
        lo, hi = h * HEAD_DIM, (h + 1) * HEAD_DIM
        qt = qt_ref[lo:hi, :]
        kr = k_ref[:, lo:hi]
        vt = vt_ref[lo:hi, :]

        kmean = jnp.sum(kr.reshape(nb, bs, HEAD_DIM), axis=1) * (1.0 / bs)
        g_t = jnp.dot(kmean.astype(BF16), qt.astype(BF16), preferred_element_type=F32)
        gm = jnp.where(blk < own, g_t, NEG_INF)
        rank = _slot_rank(gm, nb, 0)
        chosen = (rank < n_sel) & (rank < own)
        sel_bias = jnp.where(chosen, 0.0, NEG_INF)
        own_mult = jnp.where(chosen, 2.0, 1.0)
        sel_rows = [row_of(sel_bias, j) for j in range(nb)]
        mult_rows = [row_of(own_mult, j) for j in range(nb)]

        qs = (qt * (HEAD_DIM ** -0.5)).astype(BF16)
        kb = kr.astype(BF16)
        vb = vt.astype(BF16)
        alibi = [slope * (key_io + j * bs).astype(F32) for j in range(nb)]
        o_blocks = []
        for i in range(nb):
            cols = slice(i * bs, (i + 1) * bs)
            q_i = qs[:, cols]
            tiles, maxes = [], []
            for j in range(i + 1):
                s_ij = jnp.dot(kb[j * bs:(j + 1) * bs], q_i, preferred_element_type=F32) + alibi[j]
                if j == i:
                    s_ij = s_ij + causal_bias
                mx = jnp.max(s_ij, axis=0, keepdims=True)
                tiles.append(s_ij)
                maxes.append(mx + sel_rows[j][:, cols] if j < i else mx)
            m = functools.reduce(jnp.maximum, maxes)
            ps = [jnp.exp(s_ij + ((sel_rows[j][:, cols] - m) if j < i else -m)) for j, s_ij in enumerate(tiles)]
            ps[-1] = ps[-1] * mult_rows[i][:, cols]
            inv = 1.0 / functools.reduce(jnp.add, [jnp.sum(p, axis=0, keepdims=True) for p in ps])
            p_t = jnp.concatenate([(p * inv).astype(BF16) for p in ps], axis=0)
            o_blocks.append(jnp.dot(vb[:, :(i + 1) * bs], p_t, preferred_element_type=F32))
        outs.append(jnp.concatenate(o_blocks, axis=1))
    o_ref[...] = jnp.transpose(jnp.concatenate(outs, axis=0))


def _moba_prompt(qt, k, vt, slopes, b, t):
    blk = pl.BlockSpec((t, LANES), lambda bi, hp: (bi, hp))
    blk_t = pl.BlockSpec((None, LANES, t), lambda bi, hp: (bi, hp, 0))
    return pl.pallas_call(
        functools.partial(_moba_prompt_kernel, t=t),
        out_shape=jax.ShapeDtypeStruct((b * t, D_MODEL), F32),
        grid=(b, D_MODEL // LANES),
        in_specs=[pl.BlockSpec(memory_space=pltpu.SMEM), blk_t, blk, blk_t],
        out_specs=blk,
        compiler_params=_params("arbitrary", "arbitrary"),
        name="moba_prompt",
    )(slopes, qt, k, vt)


def _moba_sample_kernel(pt_ref, slope_ref, q_ref, kn_ref, vn_ref, *refs, t, nbp, bps):
    del pt_ref
    n_pg = bps * PAGES_PER_BLOCK
    k_refs, v_refs = refs[:n_pg], refs[n_pg:2 * n_pg]
    o_ref, qbd_ref, kmean_ref, s_ref, pown_ref, l_ref, acc_ref = refs[2 * n_pg:]
    k_steps = nbp // bps
    step = pl.program_id(1)
    bs = MOBA_BLOCK
    hq = N_HEADS * t
    past_len = nbp * bs
    own_blk = nbp
    nb = nbp + 1
    n_sel = min(MOBA_TOPK, nb)

    row = lax.broadcasted_iota(jnp.int32, (hq, bs), 0)
    lane = lax.broadcasted_iota(jnp.int32, (hq, bs), 1)
    assert t & (t - 1) == 0, t
    qpos = past_len + (row & (t - 1))
    slope = slope_ref[...]

    def head_diag(x):
        r = _div_pow2(lax.broadcasted_iota(jnp.int32, x.shape, 0), t)
        c = _div_pow2(lax.broadcasted_iota(jnp.int32, x.shape, 1), HEAD_DIM)
        return jnp.where(r == c, x, 0.0)

    @pl.when(step == 0)
    def _():
        q_rep = jnp.concatenate([q_ref[...]] * N_HEADS, axis=0)
        qbd_ref[...] = head_diag(q_rep)
        kmean_ref[...] = jnp.zeros_like(kmean_ref)
        acc_ref[...] = jnp.zeros_like(acc_ref)

    @pl.when(step < k_steps)
    def _():
        qbd = (qbd_ref[...] * (HEAD_DIM ** -0.5)).astype(BF16)
        blk_lane = lax.broadcasted_iota(jnp.int32, kmean_ref.shape, 1)
        kmean_new = kmean_ref[...]
        for jb in range(bps):
            n = step * bps + jb
            ka, kb = k_refs[2 * jb][...], k_refs[2 * jb + 1][...]
            ksum = jnp.sum(ka + kb, axis=1, keepdims=True) * (1.0 / bs)
            kmean_new = kmean_new + jnp.where(blk_lane == n, ksum, 0.0)
            s = jnp.concatenate([jnp.dot(qbd, ka.astype(BF16), preferred_element_type=F32),
                                 jnp.dot(qbd, kb.astype(BF16), preferred_element_type=F32)], axis=1)
            dist = qpos - (n * bs + lane)
            s_ref[n] = s - slope * dist.astype(F32)
        kmean_ref[...] = kmean_new

    @pl.when(step == k_steps - 1)
    def _():
        qbd = qbd_ref[...]
        g = jnp.dot(qbd.astype(BF16), kmean_ref[...].astype(BF16), preferred_element_type=F32)
        col = lax.broadcasted_iota(jnp.int32, g.shape, 1)
        gm = jnp.where(col < own_blk, g, NEG_INF)
        rank = _slot_rank(gm, nb, 1)
        chosen = (rank < n_sel) & (rank < own_blk) & (col < nb)

        kn = jnp.concatenate([kn_ref[...], jnp.zeros((LANES - t, D_MODEL), F32)], axis=0)
        s_own = lax.dot_general((qbd * (HEAD_DIM ** -0.5)).astype(BF16), kn.astype(BF16),
                                (((1,), (1,)), ((), ())), preferred_element_type=F32)
        dist_own = (qpos - (past_len + lane))[:, :LANES]
        s_own = s_own - slope[:, :LANES] * dist_own.astype(F32)
        s_own = jnp.where(dist_own >= 0, s_own, NEG_INF)

        def sel_col(n):
            return jnp.sum(jnp.where((col == n) & chosen, 1.0, 0.0), axis=1, keepdims=True)

        def max_body(n, m):
            sn = jnp.where(sel_col(n) > 0.0, s_ref[n], NEG_INF)
            return jnp.maximum(m, jnp.max(sn, axis=1, keepdims=True))

        m = lax.fori_loop(0, nbp, max_body, jnp.max(s_own, axis=1, keepdims=True))

        def e_body(n, l):
            e = jnp.where(sel_col(n) > 0.0, jnp.exp(s_ref[n] - m), 0.0)
            s_ref[n] = e
            return l + jnp.sum(e, axis=1, keepdims=True)

        own_mult = 1.0 + sel_col(own_blk)
        e_own = jnp.exp(s_own - m) * own_mult
        l = lax.fori_loop(0, nbp, e_body, jnp.sum(e_own, axis=1, keepdims=True))
        pown_ref[...] = (e_own / l).astype(BF16)
        l_ref[...] = jnp.broadcast_to(l, l_ref.shape)

    @pl.when(step >= k_steps)
    def _():
        nt = (((1,), (1,)), ((), ()))
        acc = acc_ref[...]
        for jb in range(bps):
            p = (s_ref[(step - k_steps) * bps + jb] / l_ref[:, :1]).astype(BF16)
            for half in range(PAGES_PER_BLOCK):
                acc = acc + lax.dot_general(p[:, half * PAGE_SIZE:(half + 1) * PAGE_SIZE],
                                            v_refs[2 * jb + half][...].astype(BF16), nt,
                                            preferred_element_type=F32)
        acc_ref[...] = acc

    @pl.when(step == 2 * k_steps - 1)
    def _():
        vn = jnp.concatenate([vn_ref[...], jnp.zeros((LANES - t, D_MODEL), F32)], axis=0).astype(BF16)
        acc = acc_ref[...] + jnp.dot(pown_ref[...], vn, preferred_element_type=F32)
        acc = head_diag(acc)
        o_ref[...] = jnp.sum(acc.reshape(N_HEADS, t, D_MODEL), axis=0)


def _moba_sample(q, k_new, v_new, kt_pool, vt_pool, page_table, slopes, db, t):
    n_pages = page_table.shape[1]
    assert n_pages % PAGES_PER_BLOCK == 0 and PAGES_PER_BLOCK == 2 and t <= LANES
    nbp = n_pages // PAGES_PER_BLOCK
    bps = min(SAMPLE_BLOCKS_PER_STEP, nbp)
    assert nbp % bps == 0
    k_steps = nbp // bps
    n_pg = bps * PAGES_PER_BLOCK
    hq = N_HEADS * t
    nb_pad = -(-(nbp + 1) // LANES) * LANES
    slope_rows = jnp.broadcast_to(jnp.repeat(slopes, t)[:, None], (hq, MOBA_BLOCK))
    new_rows = pl.BlockSpec((t, D_MODEL), lambda b, s, pt: (b, 0))

    def page(first, j):
        def index(b, s, pt):
            return (pt[b, jnp.clip(s - first, 0, k_steps - 1) * n_pg + j], 0, 0)
        return pl.BlockSpec((None, D_MODEL, PAGE_SIZE), index)

    grid_spec = pltpu.PrefetchScalarGridSpec(
        num_scalar_prefetch=1,
        grid=(db, 2 * k_steps),
        in_specs=[pl.BlockSpec((hq, MOBA_BLOCK), lambda b, s, pt: (0, 0)),
                  new_rows, new_rows, new_rows,
                  *[page(0, j) for j in range(n_pg)], *[page(k_steps, j) for j in range(n_pg)]],
        out_specs=new_rows,
        scratch_shapes=[pltpu.VMEM((hq, D_MODEL), F32),
                        pltpu.VMEM((D_MODEL, nb_pad), F32),
                        pltpu.VMEM((nbp, hq, MOBA_BLOCK), F32),
                        pltpu.VMEM((hq, LANES), BF16),
                        pltpu.VMEM((hq, LANES), F32),
                        pltpu.VMEM((hq, D_MODEL), F32)])
    return pl.pallas_call(
        functools.partial(_moba_sample_kernel, t=t, nbp=nbp, bps=bps),
        out_shape=jax.ShapeDtypeStruct((db * t, D_MODEL), F32),
        grid_spec=grid_spec,
        compiler_params=_params("arbitrary", "arbitrary"),
        name="moba_sample",
    )(page_table, slope_rows, q, k_new, v_new, *([kt_pool] * n_pg), *([vt_pool] * n_pg))


def _rglru_kernel(x_ref, win_ref, bin_ref, conv0_ref, h0_ref, cw_ref, cb_ref, wa_ref, ba_ref, wx_ref, bx_ref,
                  lam_ref, wout_ref, bout_ref, g_ref, beta_ref, *rest, tt):
    o_ref, conv_ref, hlast_ref, halo_ref, h_ref, a_ref, b_ref, hs_ref = rest[-8:]
    i = pl.program_id(1)
    keep = CONV_W - 1
    first = CONV_HALO - keep

    @pl.when(i == 0)
    def _():
        halo_ref[first:CONV_HALO, :] = conv0_ref[...]
        h_ref[...] = h0_ref[...]

    x = x_ref[...]
    u = jnp.dot(x.astype(BF16), win_ref[...], preferred_element_type=F32) + bin_ref[...]
    y_in = u[:, :D_RNN]
    y_branch = 0.5 * y_in * (1.0 + jnp.tanh(0.7978845608028654 * (y_in + 0.044715 * (y_in * y_in * y_in))))
    halo_ref[CONV_HALO:CONV_HALO + tt, :] = u[:, D_RNN:]
    xc = cb_ref[...]
    acc = None
    for j in range(CONV_W):
        term = halo_ref[first + j:first + j + tt, :] * cw_ref[j:j + 1, :]
        acc = term if acc is None else acc + term
    xc = xc + acc

    r_parts, i_parts = [], []
    for blk in range(N_LRU_BLOCKS):
        xb = xc[:, blk * LRU_BLOCK_WIDTH:(blk + 1) * LRU_BLOCK_WIDTH].astype(BF16)
        r_parts.append(jnp.dot(xb, wa_ref[blk], preferred_element_type=F32))
        i_parts.append(jnp.dot(xb, wx_ref[blk], preferred_element_type=F32))
    r = jax.nn.sigmoid(jnp.concatenate(r_parts, axis=1) + ba_ref[...])
    ig = jax.nn.sigmoid(jnp.concatenate(i_parts, axis=1) + bx_ref[...])
    log_a = LRU_C * r * jax.nn.log_sigmoid(lam_ref[...])
    a_ref[...] = jnp.exp(log_a)
    b_ref[...] = jnp.sqrt(1.0 - jnp.exp(2.0 * log_a)) * (ig * xc)

    def step(s, h):
        h = a_ref[pl.ds(s, 1), :] * h + b_ref[pl.ds(s, 1), :]
        hs_ref[pl.ds(s, 1), :] = h
        return h

    h = lax.fori_loop(0, tt, step, h_ref[...], unroll=8)
    h_ref[...] = h
    hy = (hs_ref[...] * y_branch).astype(BF16)
    m = jnp.dot(hy, wout_ref[...], preferred_element_type=F32) + bout_ref[...]
    o_ref[...] = _layer_norm(DEEPNORM_ALPHA * x + m, g_ref[...], beta_ref[...])
    tail = halo_ref[first + tt:CONV_HALO + tt, :]
    halo_ref[first:CONV_HALO, :] = tail

    @pl.when(i == pl.num_programs(1) - 1)
    def _():
        conv_ref[...] = tail
        hlast_ref[...] = h


def _rglru(x, row0, conv0, h0, w_in_bf, b_in, cw, cb, wa_bf, ba, wx_bf, bx, lam, w_out_bf, b_out, g, beta,
           b, t, *, into=None):
    tt = _row_tile(t, 256)
    nt = t // tt
    assert row0 % tt == 0
    blk0 = row0 // tt
    keep = CONV_W - 1
    vec = pl.BlockSpec((1, D_RNN), lambda bi, i: (0, 0))
    gate_w = pl.BlockSpec((N_LRU_BLOCKS, LRU_BLOCK_WIDTH, LRU_BLOCK_WIDTH), lambda bi, i: (0, 0, 0))
    state3 = pl.BlockSpec((None, keep, D_RNN), lambda bi, i: (bi, 0, 0))
    state1 = pl.BlockSpec((None, 1, D_RNN), lambda bi, i: (bi, 0, 0))
    rows = pl.BlockSpec((tt, D_MODEL), lambda bi, i: (blk0 + bi * nt + i, 0))
    whole = lambda shape: pl.BlockSpec(shape, lambda bi, i: (0,) * len(shape))
    in_specs = [rows, whole((D_MODEL, 2 * D_RNN)), whole((1, 2 * D_RNN)),
                state3, state1, whole((CONV_W, D_RNN)), vec,
                gate_w, vec, gate_w, vec, vec,
                whole((D_RNN, D_MODEL)), whole((1, D_MODEL)), whole((1, D_MODEL)), whole((1, D_MODEL))]
    args = [x, w_in_bf, b_in.reshape(1, 2 * D_RNN), conv0, h0.reshape(b, 1, D_RNN), cw, cb.reshape(1, D_RNN),
            wa_bf, ba.reshape(1, D_RNN), wx_bf, bx.reshape(1, D_RNN), lam.reshape(1, D_RNN),
            w_out_bf, b_out.reshape(1, D_MODEL), g.reshape(1, D_MODEL), beta.reshape(1, D_MODEL)]
    aliases = {}
    if into is not None:
        in_specs.append(pl.BlockSpec(memory_space=pl.ANY))
        args.append(into)
        aliases = {len(args) - 1: 0}
    return pl.pallas_call(
        functools.partial(_rglru_kernel, tt=tt),
        out_shape=(jax.ShapeDtypeStruct((x.shape[0], D_MODEL), F32),
                   jax.ShapeDtypeStruct((b, keep, D_RNN), F32),
                   jax.ShapeDtypeStruct((b, 1, D_RNN), F32)),
        grid=(b, nt),
        in_specs=in_specs,
        out_specs=(rows, state3, state1),
        input_output_aliases=aliases,
        scratch_shapes=[pltpu.VMEM((CONV_HALO + tt, D_RNN), F32),
                        pltpu.VMEM((1, D_RNN), F32),
                        pltpu.VMEM((tt, D_RNN), F32),
                        pltpu.VMEM((tt, D_RNN), F32),
                        pltpu.VMEM((tt, D_RNN), F32)],
        compiler_params=_params("arbitrary", "arbitrary"),
        name="rglru",
    )(*args)


def _router_kernel(x_ref, wt_ref, b_ref, ids_ref, gates_ref, rank_ref, counts_ref, carry_ref):
    i = pl.program_id(0)
    tm = x_ref.shape[0]

    @pl.when(i == 0)
    def _():
        carry_ref[...] = jnp.zeros_like(carry_ref)

    logits = lax.dot_general(wt_ref[...].astype(BF16), x_ref[...].astype(BF16), (((1,), (1,)), ((), ())),
                             preferred_element_type=F32) + b_ref[:, :1]
    e_io = lax.broadcasted_iota(jnp.int32, logits.shape, 0)
    lt = logits
    vals, ids = [], []
    for _ in range(TOP_K):
        mx = jnp.max(lt, axis=0, keepdims=True)
        idx = jnp.min(jnp.where(lt == mx, e_io, N_EXPERTS), axis=0, keepdims=True)
        vals.append(mx)
        ids.append(idx)
        lt = jnp.where(e_io == idx, -jnp.inf, lt)
    ex = [jnp.exp(v - vals[0]) for v in vals]
    den = ex[0] + ex[1] + ex[2] + ex[3]
    gates_ref[...] = jnp.concatenate([e / den for e in ex], axis=0)
    ids_ref[...] = jnp.concatenate(ids, axis=0)

    onehot = jnp.zeros(logits.shape, F32)
    for idx in ids:
        onehot = onehot + jnp.where(e_io == idx, 1.0, 0.0)
    s_io = lax.broadcasted_iota(jnp.int32, (tm, tm), 0)
    t_io = lax.broadcasted_iota(jnp.int32, (tm, tm), 1)
    before = jnp.where(s_io < t_io, 1.0, 0.0).astype(BF16)
    prefix = jnp.dot(onehot.astype(BF16), before, preferred_element_type=F32) + carry_ref[:, :1]
    ranks = [jnp.sum(jnp.where(e_io == idx, prefix, 0.0), axis=0, keepdims=True) for idx in ids]
    rank_ref[...] = jnp.concatenate(ranks, axis=0).astype(jnp.int32)
    carry_ref[...] += jnp.sum(onehot, axis=1, keepdims=True)
    counts_ref[...] = carry_ref[...].astype(jnp.int32)


def _router(x, w_router, b_router):
    n = x.shape[0]
    tm = _row_tile(n, ROUTER_TILE)
    slot = pl.BlockSpec((TOP_K, tm), lambda i: (0, i))
    per_expert = pl.BlockSpec((N_EXPERTS, LANES), lambda i: (0, 0))
    return pl.pallas_call(
        _router_kernel,
        out_shape=(jax.ShapeDtypeStruct((TOP_K, n), jnp.int32),
                   jax.ShapeDtypeStruct((TOP_K, n), F32),
                   jax.ShapeDtypeStruct((TOP_K, n), jnp.int32),
                   jax.ShapeDtypeStruct((N_EXPERTS, LANES), jnp.int32)),
        grid=(n // tm,),
        in_specs=[pl.BlockSpec((tm, D_MODEL), lambda i: (i, 0)),
                  pl.BlockSpec((N_EXPERTS, D_MODEL), lambda i: (0, 0)),
                  per_expert],
        out_specs=(slot, slot, slot, per_expert),
        scratch_shapes=[pltpu.VMEM((N_EXPERTS, LANES), F32)],
        compiler_params=_params("arbitrary"),
        name="moe_router",
    )(x, w_router.T, jnp.broadcast_to(b_router[:, None], (N_EXPERTS, LANES)))


def _dispatch_kernel(last_ref, dest_ref, x_ref, xg_hbm, zero_ref, sem, zero_sem):
    tm = x_ref.shape[0]
    rows = zero_ref.shape[0]

    def zero_copy(e):
        return pltpu.make_async_copy(zero_ref, xg_hbm.at[pl.ds(last_ref[e] * rows, rows)], zero_sem)

    @pl.when(pl.program_id(0) == 0)
    def _():
        zero_ref[...] = jnp.zeros_like(zero_ref)
        for e in range(N_EXPERTS):
            pl.when(last_ref[e] >= 0)(lambda e=e: zero_copy(e).start())
        for e in range(N_EXPERTS):
            pl.when(last_ref[e] >= 0)(lambda e=e: zero_copy(e).wait())

    def row_copy(r, k):
        return pltpu.make_async_copy(x_ref.at[pl.ds(r, 1)], xg_hbm.at[pl.ds(dest_ref[k * tm + r], 1)], sem)

    def issue(r, c):
        for k in range(TOP_K):
            row_copy(r, k).start(priority=k % 2)
        return c

    def drain(r, c):
        for k in range(TOP_K):
            row_copy(r, k).wait()
        return c

    lax.fori_loop(0, tm, issue, 0, unroll=8)
    lax.fori_loop(0, tm, drain, 0, unroll=8)


def _slot_rows_by_tile(dest, tm):
    k, n = dest.shape
    return dest.reshape(k, n // tm, tm).transpose(1, 0, 2).reshape(-1)


def _dispatch(x, dest, last_tile, n_rows):
    n = x.shape[0]
    tm = _row_tile(n, DISPATCH_TOKEN_TILE)
    dest = _slot_rows_by_tile(dest, tm)
    grid_spec = pltpu.PrefetchScalarGridSpec(
        num_scalar_prefetch=1,
        grid=(n // tm,),
        in_specs=[pl.BlockSpec((TOP_K * tm,), lambda i, last: (i,), memory_space=pltpu.SMEM),
                  pl.BlockSpec((tm, D_MODEL), lambda i, last: (i, 0))],
        out_specs=pl.BlockSpec(memory_space=pl.ANY),
        scratch_shapes=[pltpu.VMEM((EXPERT_TILE_ROWS, D_MODEL), F32),
                        pltpu.SemaphoreType.DMA, pltpu.SemaphoreType.DMA])
    return pl.pallas_call(
        _dispatch_kernel,
        out_shape=jax.ShapeDtypeStruct((n_rows, D_MODEL), F32),
        grid_spec=grid_spec,
        compiler_params=_params("arbitrary"),
        name="moe_dispatch",
    )(last_tile, dest, x)


def _expert_kernel(te_ref, tr_ref, xg_ref, wu_ref, bu_ref, wd_ref, bd_ref, y_ref, wu_bf_ref, wd_bf_ref):
    g = pl.program_id(0)
    e = te_ref[g]
    prev = te_ref[jnp.maximum(g - 1, 0)]
    rows = tr_ref[g]
    half = xg_ref.shape[0] // 2

    @pl.when((g == 0) | (e != prev))
    def _():
        wu_bf_ref[...] = wu_ref[...].astype(BF16)
        wd_bf_ref[...] = wd_ref[...].astype(BF16)

    def ffn(x):
        gu = jnp.dot(x.astype(BF16), wu_bf_ref[...], preferred_element_type=F32) + bu_ref[...]
        gate = jnp.minimum(gu[:, :D_EXPERT], SWIGLU_LIMIT)
        up = jnp.clip(gu[:, D_EXPERT:], -SWIGLU_LIMIT, SWIGLU_LIMIT)
        glu = gate * jax.nn.sigmoid(SWIGLU_ALPHA * gate)
        return jnp.dot(((up + 1.0) * glu).astype(BF16), wd_bf_ref[...], preferred_element_type=F32) + bd_ref[...]

    @pl.when(rows > half)
    def _():
        y_ref[...] = ffn(xg_ref[...])

    @pl.when((rows > 0) & (rows <= half))
    def _():
        y_ref[:half, :] = ffn(xg_ref[:half, :])
        y_ref[half:, :] = jnp.zeros((half, D_MODEL), F32)

    @pl.when(rows == 0)
    def _():
        y_ref[...] = jnp.zeros_like(y_ref)


def _experts(tile_e, tile_rows, xg, layer, w_up, b_up, w_down, b_down):
    n_rows = xg.shape[0]
    tm = EXPERT_TILE_ROWS
    n_layers = w_up.shape[0]
    grid_spec = pltpu.PrefetchScalarGridSpec(
        num_scalar_prefetch=2,
        grid=(n_rows // tm,),
        in_specs=[pl.BlockSpec((tm, D_MODEL), lambda g, te, tv: (g * jnp.minimum(tv[g], 1), 0)),
                  pl.BlockSpec((None, None, D_MODEL, 2 * D_EXPERT), lambda g, te, tv: (layer, te[g], 0, 0)),
                  pl.BlockSpec((None, None, 1, 2 * D_EXPERT), lambda g, te, tv: (layer, te[g], 0, 0)),
                  pl.BlockSpec((None, None, D_EXPERT, D_MODEL), lambda g, te, tv: (layer, te[g], 0, 0)),
                  pl.BlockSpec((None, None, 1, D_MODEL), lambda g, te, tv: (layer, te[g], 0, 0))],
        out_specs=pl.BlockSpec((tm, D_MODEL), lambda g, te, tv: (g, 0)),
        scratch_shapes=[pltpu.VMEM((D_MODEL, 2 * D_EXPERT), BF16),
                        pltpu.VMEM((D_EXPERT, D_MODEL), BF16)])
    return pl.pallas_call(
        _expert_kernel,
        out_shape=jax.ShapeDtypeStruct((n_rows, D_MODEL), F32),
        grid_spec=grid_spec,
        compiler_params=_params("arbitrary"),
        name="moe_experts",
    )(tile_e, tile_rows, xg, w_up, b_up.reshape(n_layers, N_EXPERTS, 1, 2 * D_EXPERT),
      w_down, b_down.reshape(n_layers, N_EXPERTS, 1, D_MODEL))


def _combine_ln_kernel(dest_ref, dest_next_ref, x_ref, gate_ref, g_ref, beta_ref, yg_hbm, o_ref, buf_ref, sem):
    i = pl.program_id(0)
    n_steps = pl.num_programs(0)
    tm = x_ref.shape[0]

    def row_copy(d_ref, slot, r, k):
        return pltpu.make_async_copy(yg_hbm.at[pl.ds(d_ref[k * tm + r], 1)],
                                     buf_ref.at[slot, k, pl.ds(r, 1)], sem.at[slot])

    def issue(d_ref, slot):
        def body(r, c):
            for k in range(TOP_K):
                row_copy(d_ref, slot, r, k).start(priority=k % 2)
            return c
        lax.fori_loop(0, tm, body, 0, unroll=8)

    @pl.when(i == 0)
    def _():
        issue(dest_ref, 0)

    @pl.when(i + 1 < n_steps)
    def _():
        issue(dest_next_ref, (i + 1) % 2)

    slot = i % 2

    def drain(r, c):
        for k in range(TOP_K):
            row_copy(dest_ref, slot, r, k).wait()
        return c

    lax.fori_loop(0, tm, drain, 0, unroll=8)
    gates = gate_ref[...]
    y = gates[:, 0:1] * buf_ref[slot, 0]
    for k in range(1, TOP_K):
        y = y + gates[:, k:k + 1] * buf_ref[slot, k]
    o_ref[...] = _layer_norm(DEEPNORM_ALPHA * x_ref[...] + y, g_ref[...], beta_ref[...])


def _combine_ln(x, yg, dest, gates_t, g, beta):
    n = x.shape[0]
    tm = _row_tile(n, COMBINE_TOKEN_TILE)
    dest = _slot_rows_by_tile(dest, tm)
    n_steps = n // tm
    row = pl.BlockSpec((tm, D_MODEL), lambda i: (i, 0))
    vec = pl.BlockSpec((1, D_MODEL), lambda i: (0, 0))
    return pl.pallas_call(
        _combine_ln_kernel,
        out_shape=jax.ShapeDtypeStruct((n, D_MODEL), F32),
        grid=(n_steps,),
        in_specs=[pl.BlockSpec((TOP_K * tm,), lambda i: (i,), memory_space=pltpu.SMEM),
                  pl.BlockSpec((TOP_K * tm,), lambda i: (jnp.minimum(i + 1, n_steps - 1),),
                               memory_space=pltpu.SMEM),
                  row, pl.BlockSpec((tm, TOP_K), lambda i: (i, 0)), vec, vec,
                  pl.BlockSpec(memory_space=pl.ANY)],
        out_specs=row,
        scratch_shapes=[pltpu.VMEM((2, TOP_K, tm, D_MODEL), F32), pltpu.SemaphoreType.DMA((2,))],
        compiler_params=_params("arbitrary"),
        name="moe_combine_ln",
    )(dest, dest, x, gates_t, g.reshape(1, D_MODEL), beta.reshape(1, D_MODEL), yg)


def _moe_ln(x, layer, w_router, b_router, w_up, b_up, w_down, b_down, g, beta):
    n = x.shape[0]
    tm = EXPERT_TILE_ROWS
    ids, gates, rank, counts = _router(x, w_router, b_router)
    counts = counts[:, 0]
    padded = (counts + tm - 1) // tm * tm
    pend = jnp.cumsum(padded)
    pstart = pend - padded
    expert_io = jnp.arange(N_EXPERTS, dtype=jnp.int32)[:, None, None]
    dest = jnp.sum(jnp.where(ids[None] == expert_io, pstart[:, None, None], 0), axis=0) + rank
    n_tiles = -(-(n * TOP_K) // tm) + N_EXPERTS
    tile_row0 = jnp.arange(n_tiles, dtype=jnp.int32) * tm
    tile_e = jnp.sum((pend[None, :] <= tile_row0[:, None]).astype(jnp.int32), axis=1)
    tile_e = jnp.minimum(tile_e, N_EXPERTS - 1)
    seg_end = jnp.sum(jnp.where(tile_e[:, None] == expert_io[:, 0, 0][None, :], (pstart + counts)[None, :], 0), axis=1)
    tile_rows = jnp.clip(seg_end - tile_row0, 0, tm).astype(jnp.int32)
    last_tile = jnp.where(padded > 0, pend // tm - 1, -1).astype(jnp.int32)
    xg = _dispatch(x, dest, last_tile, n_tiles * tm)
    yg = _experts(tile_e, tile_rows, xg, layer, w_up, b_up, w_down, b_down)
    return _combine_ln(x, yg, dest, gates.T, g, beta)


def kernel(x_prompt, x_sample, cache_k, cache_v, state_conv, state_h, page_table, ln_g, ln_b, w_qkv, w_o, w_in, b_in, conv_w, conv_b, w_gate_a, b_gate_a, w_gate_x, b_gate_x, lru_lambda, w_out, b_out, w_router, b_router, w_up, b_up, w_down, b_down):
    bp, tp, d = x_prompt.shape
    db, ts, _ = x_sample.shape
    n_p, n_s = bp * tp, db * ts
    xp = x_prompt.reshape(n_p, d)
    xs = x_sample.reshape(n_s, d)
    slopes = 2.0 ** (-8.0 * jnp.arange(1, N_HEADS + 1, dtype=F32) / N_HEADS)
    zeros_d = jnp.zeros((d,), F32)

    def moe_block(x, layer):
        return _moe_ln(x, layer, w_router[layer], b_router[layer], w_up, b_up, w_down, b_down,
                       ln_g[layer, 1], ln_b[layer, 1])

    n = n_p + n_s

    w_qkv_bf = w_qkv[0].astype(BF16)
    w_o_bf = w_o[0].astype(BF16)
    wkt_bf = w_qkv[0][:, d:2 * d].T.astype(BF16)
    wvt_bf = w_qkv[0][:, 2 * d:].T.astype(BF16)
    wqt_bf = w_qkv[0][:, :d].T.astype(BF16)
    qtp, kp, ktp, vtp = _qkv_prompt(xp, wqt_bf, w_qkv_bf[:, d:2 * d], wkt_bf, wvt_bf, bp, tp)
    qs, ks, vs = _qkv(xs, w_qkv_bf)
    op = _moba_prompt(qtp, kp, vtp, slopes, bp, tp)
    n_phys = cache_k.shape[1]
    kt_pool = jnp.transpose(cache_k[0], (0, 2, 3, 1)).reshape(n_phys, d, PAGE_SIZE)
    vt_pool = jnp.transpose(cache_v[0], (0, 2, 3, 1)).reshape(n_phys, d, PAGE_SIZE)
    os_ = _moba_sample(qs, ks, vs, kt_pool, vt_pool, page_table, slopes, db, ts)
    x = _linear_res_ln(op, w_o_bf, zeros_d, xp, ln_g[0, 0], ln_b[0, 0], out_rows=n)
    x = _linear_res_ln(os_, w_o_bf, zeros_d, xs, ln_g[0, 0], ln_b[0, 0], out_rows=n, row0=n_p, into=x)
    x = moe_block(x, 0)

    w_in_bf = w_in[0].astype(BF16)
    wa_bf = w_gate_a[0].astype(BF16)
    wx_bf = w_gate_x[0].astype(BF16)
    rec = (w_in_bf, b_in[0], conv_w[0], conv_b[0], wa_bf, b_gate_a[0].reshape(-1), wx_bf, b_gate_x[0].reshape(-1),
           lru_lambda[0], w_out[0].astype(BF16), b_out[0], ln_g[1, 0], ln_b[1, 0])
    conv0 = jnp.zeros((bp, CONV_W - 1, D_RNN), F32)
    h0 = jnp.zeros((bp, D_RNN), F32)
    x1, conv_p, h_p = _rglru(x, 0, conv0, h0, *rec, bp, tp)
    x1, conv_s, h_s = _rglru(x, n_p, state_conv[0], state_h[0], *rec, db, ts, into=x1)
    x = moe_block(x1, 1)
    xp, xs = x[:n_p], x[n_p:]

    def kv_rows(a_t):
        return jnp.transpose(a_t.reshape(1, bp, N_HEADS, HEAD_DIM, tp), (0, 1, 4, 2, 3))

    kv_s = (1, db, ts, N_HEADS, HEAD_DIM)
    return (xp.reshape(bp, tp, d), xs.reshape(db, ts, d),
            kv_rows(ktp), kv_rows(vtp), conv_p[None], h_p.reshape(1, bp, D_RNN),
            ks.reshape(kv_s), vs.reshape(kv_s), conv_s[None], h_s.reshape(1, db, D_RNN))
```

```python
import functools

import jax
import jax.numpy as jnp
from jax import lax
from jax.experimental import pallas as pl
from jax.experimental.pallas import tpu as pltpu

F32 = jnp.float32
BF16 = jnp.bfloat16

D_MODEL = 1024
N_HEADS = 16
HEAD_DIM = D_MODEL // N_HEADS
MOBA_BLOCK = 256
MOBA_TOPK = 3
PAGE_SIZE = 128
PAGES_PER_BLOCK = MOBA_BLOCK // PAGE_SIZE
D_RNN = D_MODEL
LRU_BLOCK_WIDTH = 256
N_LRU_BLOCKS = D_RNN // LRU_BLOCK_WIDTH
CONV_W = 4
LRU_C = 8.0
N_EXPERTS = 32
TOP_K = 4
D_EXPERT = D_MODEL
SWIGLU_LIMIT = 7.0
SWIGLU_ALPHA = 1.702
DEPTH = 2
DEEPNORM_ALPHA = (2 * DEPTH) ** 0.25
LN_EPS = 1e-5
NEG_INF = -1e30

V7X_VMEM_LIMIT_BYTES = 56 * 1024 * 1024
SUBLANES = 8
LANES = 128
CONV_HALO = SUBLANES
EXPERT_TILE_ROWS = 512
ROUTER_TILE = 256
SAMPLE_BLOCKS_PER_STEP = 8
DISPATCH_TOKEN_TILE = 1280
COMBINE_TOKEN_TILE = 256


def _params(*sem):
    return pltpu.CompilerParams(dimension_semantics=sem, vmem_limit_bytes=V7X_VMEM_LIMIT_BYTES)


def _row_tile(m, pref):
    t = min(pref, m)
    while m % t or t % SUBLANES:
        t -= SUBLANES
    return t


def _div_pow2(x, n):
    assert n & (n - 1) == 0, n
    return lax.shift_right_logical(x, n.bit_length() - 1)


def _layer_norm(y, g, b):
    mu = jnp.mean(y, axis=-1, keepdims=True)
    yc = y - mu
    var = jnp.mean(yc * yc, axis=-1, keepdims=True)
    return yc * lax.rsqrt(var + LN_EPS) * g + b


def _qkv_kernel(x_ref, w_ref, q_ref, k_ref, v_ref):
    x = x_ref[...].astype(BF16)
    for c, o_ref in enumerate((q_ref, k_ref, v_ref)):
        o_ref[...] = jnp.dot(x, w_ref[:, c * D_MODEL:(c + 1) * D_MODEL], preferred_element_type=F32)


def _qkv(x, w_bf):
    m = x.shape[0]
    tm = _row_tile(m, 512)
    out = jax.ShapeDtypeStruct((m, D_MODEL), F32)
    row = pl.BlockSpec((tm, D_MODEL), lambda i: (i, 0))
    return pl.pallas_call(
        _qkv_kernel,
        out_shape=(out, out, out),
        grid=(m // tm,),
        in_specs=[row, pl.BlockSpec((D_MODEL, 3 * D_MODEL), lambda i: (0, 0))],
        out_specs=(row, row, row),
        compiler_params=_params("arbitrary"),
        name="qkv_proj",
    )(x, w_bf)


def _qkv_prompt_kernel(x_ref, wqt_ref, wk_ref, wkt_ref, wvt_ref, qt_ref, k_ref, kt_ref, vt_ref):
    x = x_ref[...].astype(BF16)
    nt = (((1,), (1,)), ((), ()))
    qt_ref[...] = lax.dot_general(wqt_ref[...], x, nt, preferred_element_type=F32)
    k_ref[...] = jnp.dot(x, wk_ref[...], preferred_element_type=F32)
    kt_ref[...] = lax.dot_general(wkt_ref[...], x, nt, preferred_element_type=F32)
    vt_ref[...] = lax.dot_general(wvt_ref[...], x, nt, preferred_element_type=F32)


def _qkv_prompt(x, wqt_bf, wk_bf, wkt_bf, wvt_bf, b, t):
    tm = _row_tile(t, 512)
    nt = t // tm
    w = pl.BlockSpec((D_MODEL, D_MODEL), lambda bi, i: (0, 0))
    row = pl.BlockSpec((tm, D_MODEL), lambda bi, i: (bi * nt + i, 0))
    col = pl.BlockSpec((None, D_MODEL, tm), lambda bi, i: (bi, 0, i))
    t_out = jax.ShapeDtypeStruct((b, D_MODEL, t), F32)
    return pl.pallas_call(
        _qkv_prompt_kernel,
        out_shape=(t_out, jax.ShapeDtypeStruct((b * t, D_MODEL), F32), t_out, t_out),
        grid=(b, nt),
        in_specs=[row, w, w, w, w],
        out_specs=(col, row, col, col),
        compiler_params=_params("arbitrary", "arbitrary"),
        name="qkv_proj_prompt",
    )(x, wqt_bf, wk_bf, wkt_bf, wvt_bf)


def _linear_res_ln_kernel(a_ref, w_ref, b_ref, res_ref, g_ref, beta_ref, *rest):
    o_ref = rest[-1]
    m = jnp.dot(a_ref[...].astype(BF16), w_ref[...], preferred_element_type=F32) + b_ref[...]
    o_ref[...] = _layer_norm(DEEPNORM_ALPHA * res_ref[...] + m, g_ref[...], beta_ref[...])


def _linear_res_ln(a, w_bf, b, res, g, beta, *, out_rows=None, row0=0, into=None):
    m, k = a.shape
    n = w_bf.shape[1]
    out_rows = m if out_rows is None else out_rows
    tm = _row_tile(m, 512)
    assert row0 % tm == 0
    blk0 = row0 // tm
    row_in = pl.BlockSpec((tm, k), lambda i: (i, 0))
    row = pl.BlockSpec((tm, n), lambda i: (i, 0))
    vec = pl.BlockSpec((1, n), lambda i: (0, 0))
    in_specs = [row_in, pl.BlockSpec((k, n), lambda i: (0, 0)), vec, row, vec, vec]
    args = [a, w_bf, b.reshape(1, n), res, g.reshape(1, n), beta.reshape(1, n)]
    aliases = {}
    if into is not None:
        in_specs.append(pl.BlockSpec(memory_space=pl.ANY))
        args.append(into)
        aliases = {len(args) - 1: 0}
    return pl.pallas_call(
        _linear_res_ln_kernel,
        out_shape=jax.ShapeDtypeStruct((out_rows, n), F32),
        grid=(m // tm,),
        in_specs=in_specs,
        out_specs=pl.BlockSpec((tm, n), lambda i: (blk0 + i, 0)),
        input_output_aliases=aliases,
        compiler_params=_params("arbitrary"),
        name="linear_res_ln",
    )(*args)


def _slot_rank(gm, n_blocks, axis):
    blk = lax.broadcasted_iota(jnp.int32, gm.shape, axis)
    rank = jnp.zeros(gm.shape, jnp.int32)
    for m in range(n_blocks):
        g_m = jnp.sum(jnp.where(blk == m, gm, 0.0), axis=axis, keepdims=True)
        beats = (g_m > gm) | ((g_m == gm) & (m < blk))
        rank = rank + jnp.where(beats, 1, 0)
    return rank


def _moba_prompt_kernel(slope_ref, qt_ref, k_ref, vt_ref, o_ref, *, t):
    nb = t // MOBA_BLOCK
    n_sel = min(MOBA_TOPK, nb)
    bs = MOBA_BLOCK
    hp = pl.program_id(1)
    heads_per_step = LANES // HEAD_DIM

    key_io = lax.broadcasted_iota(jnp.int32, (bs, bs), 0)
    qry_io = lax.broadcasted_iota(jnp.int32, (bs, bs), 1)
    causal_bias = jnp.where(key_io <= qry_io, 0.0, NEG_INF)

    blk = lax.broadcasted_iota(jnp.int32, (nb, t), 0)
    own = _div_pow2(lax.broadcasted_iota(jnp.int32, (nb, t), 1), MOBA_BLOCK)

    def row_of(a, j):
        return jnp.sum(jnp.where(blk == j, a, 0.0), axis=0, keepdims=True)

    outs = []
    for h in range(heads_per_step):
        slope = slope_ref[hp * heads_per_step + h]
        lo, hi = h * HEAD_DIM, (h + 1) * HEAD_DIM
        qt = qt_ref[lo:hi, :]
        kr = k_ref[:, lo:hi]
        vt = vt_ref[lo:hi, :]

        kmean = jnp.sum(kr.reshape(nb, bs, HEAD_DIM), axis=1) * (1.0 / bs)
        g_t = jnp.dot(kmean.astype(BF16), qt.astype(BF16), preferred_element_type=F32)
        gm = jnp.where(blk < own, g_t, NEG_INF)
        rank = _slot_rank(gm, nb, 0)
        chosen = (rank < n_sel) & (rank < own)
        sel_bias = jnp.where(chosen, 0.0, NEG_INF)
        own_mult = jnp.where(chosen, 2.0, 1.0)
        sel_rows = [row_of(sel_bias, j) for j in range(nb)]
        mult_rows = [row_of(own_mult, j) for j in range(nb)]

        qs = (qt * (HEAD_DIM ** -0.5)).astype(BF16)
        kb = kr.astype(BF16)
        vb = vt.astype(BF16)
        alibi = [slope * (key_io + j * bs).astype(F32) for j in range(nb)]
        o_blocks = []
        for i in range(nb):
            cols = slice(i * bs, (i + 1) * bs)
            q_i = qs[:, cols]
            tiles, maxes = [], []
            for j in range(i + 1):
                s_ij = jnp.dot(kb[j * bs:(j + 1) * bs], q_i, preferred_element_type=F32) + alibi[j]
                if j == i:
                    s_ij = s_ij + causal_bias
                mx = jnp.max(s_ij, axis=0, keepdims=True)
                tiles.append(s_ij)
                maxes.append(mx + sel_rows[j][:, cols] if j < i else mx)
            m = functools.reduce(jnp.maximum, maxes)
            ps = [jnp.exp(s_ij + ((sel_rows[j][:, cols] - m) if j < i else -m)) for j, s_ij in enumerate(tiles)]
            ps[-1] = ps[-1] * mult_rows[i][:, cols]
            inv = 1.0 / functools.reduce(jnp.add, [jnp.sum(p, axis=0, keepdims=True) for p in ps])
            p_t = jnp.concatenate([(p * inv).astype(BF16) for p in ps], axis=0)
            o_blocks.append(jnp.dot(vb[:, :(i + 1) * bs], p_t, preferred_element_type=F32))
        outs.append(jnp.concatenate(o_blocks, axis=1))
    o_ref[...] = jnp.transpose(jnp.concatenate(outs, axis=0))


def _moba_prompt(qt, k, vt, slopes, b, t):
    blk = pl.BlockSpec((t, LANES), lambda bi, hp: (bi, hp))
    blk_t = pl.BlockSpec((None, LANES, t), lambda bi, hp: (bi, hp, 0))
    return pl.pallas_call(
        functools.partial(_moba_prompt_kernel, t=t),
        out_shape=jax.ShapeDtypeStruct((b * t, D_MODEL), F32),
        grid=(b, D_MODEL // LANES),
        in_specs=[pl.BlockSpec(memory_space=pltpu.SMEM), blk_t, blk, blk_t],
        out_specs=blk,
        compiler_params=_params("arbitrary", "arbitrary"),
        name="moba_prompt",
    )(slopes, qt, k, vt)


def _moba_sample_kernel(pt_ref, slope_ref, q_ref, kn_ref, vn_ref, *refs, t, nbp, bps):
    del pt_ref
    n_pg = bps * PAGES_PER_BLOCK
    k_refs, v_refs = refs[:n_pg], refs[n_pg:2 * n_pg]
    o_ref, qbd_ref, kmean_ref, s_ref, pown_ref, l_ref, acc_ref = refs[2 * n_pg:]
    k_steps = nbp // bps
    step = pl.program_id(1)
    bs = MOBA_BLOCK
    hq = N_HEADS * t
    past_len = nbp * bs
    own_blk = nbp
    nb = nbp + 1
    n_sel = min(MOBA_TOPK, nb)

    row = lax.broadcasted_iota(jnp.int32, (hq, bs), 0)
    lane = lax.broadcasted_iota(jnp.int32, (hq, bs), 1)
    assert t & (t - 1) == 0, t
    qpos = past_len + (row & (t - 1))
    slope = slope_ref[...]

    def head_diag(x):
        r = _div_pow2(lax.broadcasted_iota(jnp.int32, x.shape, 0), t)
        c = _div_pow2(lax.broadcasted_iota(jnp.int32, x.shape, 1), HEAD_DIM)
        return jnp.where(r == c, x, 0.0)

    @pl.when(step == 0)
    def _():
        q_rep = jnp.concatenate([q_ref[...]] * N_HEADS, axis=0)
        qbd_ref[...] = head_diag(q_rep)
        kmean_ref[...] = jnp.zeros_like(kmean_ref)
        acc_ref[...] = jnp.zeros_like(acc_ref)

    @pl.when(step < k_steps)
    def _():
        qbd = (qbd_ref[...] * (HEAD_DIM ** -0.5)).astype(BF16)
        blk_lane = lax.broadcasted_iota(jnp.int32, kmean_ref.shape, 1)
        kmean_new = kmean_ref[...]
        for jb in range(bps):
            n = step * bps + jb
            ka, kb = k_refs[2 * jb][...], k_refs[2 * jb + 1][...]
            ksum = jnp.sum(ka + kb, axis=1, keepdims=True) * (1.0 / bs)
            kmean_new = kmean_new + jnp.where(blk_lane == n, ksum, 0.0)
            s = jnp.concatenate([jnp.dot(qbd, ka.astype(BF16), preferred_element_type=F32),
                                 jnp.dot(qbd, kb.astype(BF16), preferred_element_type=F32)], axis=1)
            dist = qpos - (n * bs + lane)
            s_ref[n] = s - slope * dist.astype(F32)
        kmean_ref[...] = kmean_new

    @pl.when(step == k_steps - 1)
    def _():
        qbd = qbd_ref[...]
        g = jnp.dot(qbd.astype(BF16), kmean_ref[...].astype(BF16), preferred_element_type=F32)
        col = lax.broadcasted_iota(jnp.int32, g.shape, 1)
        gm = jnp.where(col < own_blk, g, NEG_INF)
        rank = _slot_rank(gm, nb, 1)
        chosen = (rank < n_sel) & (rank < own_blk) & (col < nb)

        kn = jnp.concatenate([kn_ref[...], jnp.zeros((LANES - t, D_MODEL), F32)], axis=0)
        s_own = lax.dot_general((qbd * (HEAD_DIM ** -0.5)).astype(BF16), kn.astype(BF16),
                                (((1,), (1,)), ((), ())), preferred_element_type=F32)
        dist_own = (qpos - (past_len + lane))[:, :LANES]
        s_own = s_own - slope[:, :LANES] * dist_own.astype(F32)
        s_own = jnp.where(dist_own >= 0, s_own, NEG_INF)

        def sel_col(n):
            return jnp.sum(jnp.where((col == n) & chosen, 1.0, 0.0), axis=1, keepdims=True)

        def max_body(n, m):
            sn = jnp.where(sel_col(n) > 0.0, s_ref[n], NEG_INF)
            return jnp.maximum(m, jnp.max(sn, axis=1, keepdims=True))

        m = lax.fori_loop(0, nbp, max_body, jnp.max(s_own, axis=1, keepdims=True))

        def e_body(n, l):
            e = jnp.where(sel_col(n) > 0.0, jnp.exp(s_ref[n] - m), 0.0)
            s_ref[n] = e
            return l + jnp.sum(e, axis=1, keepdims=True)

        own_mult = 1.0 + sel_col(own_blk)
        e_own = jnp.exp(s_own - m) * own_mult
        l = lax.fori_loop(0, nbp, e_body, jnp.sum(e_own, axis=1, keepdims=True))
        pown_ref[...] = (e_own / l).astype(BF16)
        l_ref[...] = jnp.broadcast_to(l, l_ref.shape)

    @pl.when(step >= k_steps)
    def _():
        nt = (((1,), (1,)), ((), ()))
        acc = acc_ref[...]
        for jb in range(bps):
            p = (s_ref[(step - k_steps) * bps + jb] / l_ref[:, :1]).astype(BF16)
            for half in range(PAGES_PER_BLOCK):
                acc = acc + lax.dot_general(p[:, half * PAGE_SIZE:(half + 1) * PAGE_SIZE],
                                            v_refs[2 * jb + half][...].astype(BF16), nt,
                                            preferred_element_type=F32)
        acc_ref[...] = acc

    @pl.when(step == 2 * k_steps - 1)
    def _():
        vn = jnp.concatenate([vn_ref[...], jnp.zeros((LANES - t, D_MODEL), F32)], axis=0).astype(BF16)
        acc = acc_ref[...] + jnp.dot(pown_ref[...], vn, preferred_element_type=F32)
        acc = head_diag(acc)
        o_ref[...] = jnp.sum(acc.reshape(N_HEADS, t, D_MODEL), axis=0)


def _moba_sample(q, k_new, v_new, kt_pool, vt_pool, page_table, slopes, db, t):
    n_pages = page_table.shape[1]
    assert n_pages % PAGES_PER_BLOCK == 0 and PAGES_PER_BLOCK == 2 and t <= LANES
    nbp = n_pages // PAGES_PER_BLOCK
    bps = min(SAMPLE_BLOCKS_PER_STEP, nbp)
    assert nbp % bps == 0
    k_steps = nbp // bps
    n_pg = bps * PAGES_PER_BLOCK
    hq = N_HEADS * t
    nb_pad = -(-(nbp + 1) // LANES) * LANES
    slope_rows = jnp.broadcast_to(jnp.repeat(slopes, t)[:, None], (hq, MOBA_BLOCK))
    new_rows = pl.BlockSpec((t, D_MODEL), lambda b, s, pt: (b, 0))

    def page(first, j):
        def index(b, s, pt):
            return (pt[b, jnp.clip(s - first, 0, k_steps - 1) * n_pg + j], 0, 0)
        return pl.BlockSpec((None, D_MODEL, PAGE_SIZE), index)

    grid_spec = pltpu.PrefetchScalarGridSpec(
        num_scalar_prefetch=1,
        grid=(db, 2 * k_steps),
        in_specs=[pl.BlockSpec((hq, MOBA_BLOCK), lambda b, s, pt: (0, 0)),
                  new_rows, new_rows, new_rows,
                  *[page(0, j) for j in range(n_pg)], *[page(k_steps, j) for j in range(n_pg)]],
        out_specs=new_rows,
        scratch_shapes=[pltpu.VMEM((hq, D_MODEL), F32),
                        pltpu.VMEM((D_MODEL, nb_pad), F32),
                        pltpu.VMEM((nbp, hq, MOBA_BLOCK), F32),
                        pltpu.VMEM((hq, LANES), BF16),
                        pltpu.VMEM((hq, LANES), F32),
                        pltpu.VMEM((hq, D_MODEL), F32)])
    return pl.pallas_call(
        functools.partial(_moba_sample_kernel, t=t, nbp=nbp, bps=bps),
        out_shape=jax.ShapeDtypeStruct((db * t, D_MODEL), F32),
        grid_spec=grid_spec,
        compiler_params=_params("arbitrary", "arbitrary"),
        name="moba_sample",
    )(page_table, slope_rows, q, k_new, v_new, *([kt_pool] * n_pg), *([vt_pool] * n_pg))


def _rglru_kernel(x_ref, win_ref, bin_ref, conv0_ref, h0_ref, cw_ref, cb_ref, wa_ref, ba_ref, wx_ref, bx_ref,
                  lam_ref, wout_ref, bout_ref, g_ref, beta_ref, *rest, tt):
    o_ref, conv_ref, hlast_ref, halo_ref, h_ref, a_ref, b_ref, hs_ref = rest[-8:]
    i = pl.program_id(1)
    keep = CONV_W - 1
    first = CONV_HALO - keep

    @pl.when(i == 0)
    def _():
        halo_ref[first:CONV_HALO, :] = conv0_ref[...]
        h_ref[...] = h0_ref[...]

    x = x_ref[...]
    u = jnp.dot(x.astype(BF16), win_ref[...], preferred_element_type=F32) + bin_ref[...]
    y_in = u[:, :D_RNN]
    y_branch = 0.5 * y_in * (1.0 + jnp.tanh(0.7978845608028654 * (y_in + 0.044715 * (y_in * y_in * y_in))))
    halo_ref[CONV_HALO:CONV_HALO + tt, :] = u[:, D_RNN:]
    xc = cb_ref[...]
    acc = None
    for j in range(CONV_W):
        term = halo_ref[first + j:first + j + tt, :] * cw_ref[j:j + 1, :]
        acc = term if acc is None else acc + term
    xc = xc + acc

    r_parts, i_parts = [], []
    for blk in range(N_LRU_BLOCKS):
        xb = xc[:, blk * LRU_BLOCK_WIDTH:(blk + 1) * LRU_BLOCK_WIDTH].astype(BF16)
        r_parts.append(jnp.dot(xb, wa_ref[blk], preferred_element_type=F32))
        i_parts.append(jnp.dot(xb, wx_ref[blk], preferred_element_type=F32))
    r = jax.nn.sigmoid(jnp.concatenate(r_parts, axis=1) + ba_ref[...])
    ig = jax.nn.sigmoid(jnp.concatenate(i_parts, axis=1) + bx_ref[...])
    log_a = LRU_C * r * jax.nn.log_sigmoid(lam_ref[...])
    a_ref[...] = jnp.exp(log_a)
    b_ref[...] = jnp.sqrt(1.0 - jnp.exp(2.0 * log_a)) * (ig * xc)

    def step(s, h):
        h = a_ref[pl.ds(s, 1), :] * h + b_ref[pl.ds(s, 1), :]
        hs_ref[pl.ds(s, 1), :] = h
        return h

    h = lax.fori_loop(0, tt, step, h_ref[...], unroll=8)
    h_ref[...] = h
    hy = (hs_ref[...] * y_branch).astype(BF16)
    m = jnp.dot(hy, wout_ref[...], preferred_element_type=F32) + bout_ref[...]
    o_ref[...] = _layer_norm(DEEPNORM_ALPHA * x + m, g_ref[...], beta_ref[...])
    tail = halo_ref[first + tt:CONV_HALO + tt, :]
    halo_ref[first:CONV_HALO, :] = tail

    @pl.when(i == pl.num_programs(1) - 1)
    def _():
        conv_ref[...] = tail
        hlast_ref[...] = h


def _rglru(x, row0, conv0, h0, w_in_bf, b_in, cw, cb, wa_bf, ba, wx_bf, bx, lam, w_out_bf, b_out, g, beta,
           b, t, *, into=None):
    tt = _row_tile(t, 256)
    nt = t // tt
    assert row0 % tt == 0
    blk0 = row0 // tt
    keep = CONV_W - 1
    vec = pl.BlockSpec((1, D_RNN), lambda bi, i: (0, 0))
    gate_w = pl.BlockSpec((N_LRU_BLOCKS, LRU_BLOCK_WIDTH, LRU_BLOCK_WIDTH), lambda bi, i: (0, 0, 0))
    state3 = pl.BlockSpec((None, keep, D_RNN), lambda bi, i: (bi, 0, 0))
    state1 = pl.BlockSpec((None, 1, D_RNN), lambda bi, i: (bi, 0, 0))
    rows = pl.BlockSpec((tt, D_MODEL), lambda bi, i: (blk0 + bi * nt + i, 0))
    whole = lambda shape: pl.BlockSpec(shape, lambda bi, i: (0,) * len(shape))
    in_specs = [rows, whole((D_MODEL, 2 * D_RNN)), whole((1, 2 * D_RNN)),
                state3, state1, whole((CONV_W, D_RNN)), vec,
                gate_w, vec, gate_w, vec, vec,
                whole((D_RNN, D_MODEL)), whole((1, D_MODEL)), whole((1, D_MODEL)), whole((1, D_MODEL))]
    args = [x, w_in_bf, b_in.reshape(1, 2 * D_RNN), conv0, h0.reshape(b, 1, D_RNN), cw, cb.reshape(1, D_RNN),
            wa_bf, ba.reshape(1, D_RNN), wx_bf, bx.reshape(1, D_RNN), lam.reshape(1, D_RNN),
            w_out_bf, b_out.reshape(1, D_MODEL), g.reshape(1, D_MODEL), beta.reshape(1, D_MODEL)]
    aliases = {}
    if into is not None:
        in_specs.append(pl.BlockSpec(memory_space=pl.ANY))
        args.append(into)
        aliases = {len(args) - 1: 0}
    return pl.pallas_call(
        functools.partial(_rglru_kernel, tt=tt),
        out_shape=(jax.ShapeDtypeStruct((x.shape[0], D_MODEL), F32),
                   jax.ShapeDtypeStruct((b, keep, D_RNN), F32),
                   jax.ShapeDtypeStruct((b, 1, D_RNN), F32)),
        grid=(b, nt),
        in_specs=in_specs,
        out_specs=(rows, state3, state1),
        input_output_aliases=aliases,
        scratch_shapes=[pltpu.VMEM((CONV_HALO + tt, D_RNN), F32),
                        pltpu.VMEM((1, D_RNN), F32),
                        pltpu.VMEM((tt, D_RNN), F32),
                        pltpu.VMEM((tt, D_RNN), F32),
                        pltpu.VMEM((tt, D_RNN), F32)],
        compiler_params=_params("arbitrary", "arbitrary"),
        name="rglru",
    )(*args)


def _router_kernel(x_ref, wt_ref, b_ref, ids_ref, gates_ref, rank_ref, counts_ref, carry_ref):
    i = pl.program_id(0)
    tm = x_ref.shape[0]

    @pl.when(i == 0)
    def _():
        carry_ref[...] = jnp.zeros_like(carry_ref)

    logits = lax.dot_general(wt_ref[...].astype(BF16), x_ref[...].astype(BF16), (((1,), (1,)), ((), ())),
                             preferred_element_type=F32) + b_ref[:, :1]
    e_io = lax.broadcasted_iota(jnp.int32, logits.shape, 0)
    lt = logits
    vals, ids = [], []
    for _ in range(TOP_K):
        mx = jnp.max(lt, axis=0, keepdims=True)
        idx = jnp.min(jnp.where(lt == mx, e_io, N_EXPERTS), axis=0, keepdims=True)
        vals.append(mx)
        ids.append(idx)
        lt = jnp.where(e_io == idx, -jnp.inf, lt)
    ex = [jnp.exp(v - vals[0]) for v in vals]
    den = ex[0] + ex[1] + ex[2] + ex[3]
    gates_ref[...] = jnp.concatenate([e / den for e in ex], axis=0)
    ids_ref[...] = jnp.concatenate(ids, axis=0)

    onehot = jnp.zeros(logits.shape, F32)
    for idx in ids:
        onehot = onehot + jnp.where(e_io == idx, 1.0, 0.0)
    s_io = lax.broadcasted_iota(jnp.int32, (tm, tm), 0)
    t_io = lax.broadcasted_iota(jnp.int32, (tm, tm), 1)
    before = jnp.where(s_io < t_io, 1.0, 0.0).astype(BF16)
    prefix = jnp.dot(onehot.astype(BF16), before, preferred_element_type=F32) + carry_ref[:, :1]
    ranks = [jnp.sum(jnp.where(e_io == idx, prefix, 0.0), axis=0, keepdims=True) for idx in ids]
    rank_ref[...] = jnp.concatenate(ranks, axis=0).astype(jnp.int32)
    carry_ref[...] += jnp.sum(onehot, axis=1, keepdims=True)
    counts_ref[...] = carry_ref[...].astype(jnp.int32)


def _router(x, w_router, b_router):
    n = x.shape[0]
    tm = _row_tile(n, ROUTER_TILE)
    slot = pl.BlockSpec((TOP_K, tm), lambda i: (0, i))
    per_expert = pl.BlockSpec((N_EXPERTS, LANES), lambda i: (0, 0))
    return pl.pallas_call(
        _router_kernel,
        out_shape=(jax.ShapeDtypeStruct((TOP_K, n), jnp.int32),
                   jax.ShapeDtypeStruct((TOP_K, n), F32),
                   jax.ShapeDtypeStruct((TOP_K, n), jnp.int32),
                   jax.ShapeDtypeStruct((N_EXPERTS, LANES), jnp.int32)),
        grid=(n // tm,),
        in_specs=[pl.BlockSpec((tm, D_MODEL), lambda i: (i, 0)),
                  pl.BlockSpec((N_EXPERTS, D_MODEL), lambda i: (0, 0)),
                  per_expert],
        out_specs=(slot, slot, slot, per_expert),
        scratch_shapes=[pltpu.VMEM((N_EXPERTS, LANES), F32)],
        compiler_params=_params("arbitrary"),
        name="moe_router",
    )(x, w_router.T, jnp.broadcast_to(b_router[:, None], (N_EXPERTS, LANES)))


def _dispatch_kernel(last_ref, dest_ref, x_ref, xg_hbm, zero_ref, sem, zero_sem):
    tm = x_ref.shape[0]
    rows = zero_ref.shape[0]

    def zero_copy(e):
        return pltpu.make_async_copy(zero_ref, xg_hbm.at[pl.ds(last_ref[e] * rows, rows)], zero_sem)

    @pl.when(pl.program_id(0) == 0)
    def _():
        zero_ref[...] = jnp.zeros_like(zero_ref)
        for e in range(N_EXPERTS):
            pl.when(last_ref[e] >= 0)(lambda e=e: zero_copy(e).start())
        for e in range(N_EXPERTS):
            pl.when(last_ref[e] >= 0)(lambda e=e: zero_copy(e).wait())

    def row_copy(r, k):
        return pltpu.make_async_copy(x_ref.at[pl.ds(r, 1)], xg_hbm.at[pl.ds(dest_ref[k * tm + r], 1)], sem)

    def issue(r, c):
        for k in range(TOP_K):
            row_copy(r, k).start(priority=k % 2)
        return c

    def drain(r, c):
        for k in range(TOP_K):
            row_copy(r, k).wait()
        return c

    lax.fori_loop(0, tm, issue, 0, unroll=8)
    lax.fori_loop(0, tm, drain, 0, unroll=8)


def _slot_rows_by_tile(dest, tm):
    k, n = dest.shape
    return dest.reshape(k, n // tm, tm).transpose(1, 0, 2).reshape(-1)


def _dispatch(x, dest, last_tile, n_rows):
    n = x.shape[0]
    tm = _row_tile(n, DISPATCH_TOKEN_TILE)
    dest = _slot_rows_by_tile(dest, tm)
    grid_spec = pltpu.PrefetchScalarGridSpec(
        num_scalar_prefetch=1,
        grid=(n // tm,),
        in_specs=[pl.BlockSpec((TOP_K * tm,), lambda i, last: (i,), memory_space=pltpu.SMEM),
                  pl.BlockSpec((tm, D_MODEL), lambda i, last: (i, 0))],
        out_specs=pl.BlockSpec(memory_space=pl.ANY),
        scratch_shapes=[pltpu.VMEM((EXPERT_TILE_ROWS, D_MODEL), F32),
                        pltpu.SemaphoreType.DMA, pltpu.SemaphoreType.DMA])
    return pl.pallas_call(
        _dispatch_kernel,
        out_shape=jax.ShapeDtypeStruct((n_rows, D_MODEL), F32),
        grid_spec=grid_spec,
        compiler_params=_params("arbitrary"),
        name="moe_dispatch",
    )(last_tile, dest, x)


def _expert_kernel(te_ref, tr_ref, xg_ref, wu_ref, bu_ref, wd_ref, bd_ref, y_ref, wu_bf_ref, wd_bf_ref):
    g = pl.program_id(0)
    e = te_ref[g]
    prev = te_ref[jnp.maximum(g - 1, 0)]
    rows = tr_ref[g]
    half = xg_ref.shape[0] // 2

    @pl.when((g == 0) | (e != prev))
    def _():
        wu_bf_ref[...] = wu_ref[...].astype(BF16)
        wd_bf_ref[...] = wd_ref[...].astype(BF16)

    def ffn(x):
        gu = jnp.dot(x.astype(BF16), wu_bf_ref[...], preferred_element_type=F32) + bu_ref[...]
        gate = jnp.minimum(gu[:, :D_EXPERT], SWIGLU_LIMIT)
        up = jnp.clip(gu[:, D_EXPERT:], -SWIGLU_LIMIT, SWIGLU_LIMIT)
        glu = gate * jax.nn.sigmoid(SWIGLU_ALPHA * gate)
        return jnp.dot(((up + 1.0) * glu).astype(BF16), wd_bf_ref[...], preferred_element_type=F32) + bd_ref[...]

    @pl.when(rows > half)
    def _():
        y_ref[...] = ffn(xg_ref[...])

    @pl.when((rows > 0) & (rows <= half))
    def _():
        y_ref[:half, :] = ffn(xg_ref[:half, :])
        y_ref[half:, :] = jnp.zeros((half, D_MODEL), F32)

    @pl.when(rows == 0)
    def _():
        y_ref[...] = jnp.zeros_like(y_ref)


def _experts(tile_e, tile_rows, xg, layer, w_up, b_up, w_down, b_down):
    n_rows = xg.shape[0]
    tm = EXPERT_TILE_ROWS
    n_layers = w_up.shape[0]
    grid_spec = pltpu.PrefetchScalarGridSpec(
        num_scalar_prefetch=2,
        grid=(n_rows // tm,),
        in_specs=[pl.BlockSpec((tm, D_MODEL), lambda g, te, tv: (g * jnp.minimum(tv[g], 1), 0)),
                  pl.BlockSpec((None, None, D_MODEL, 2 * D_EXPERT), lambda g, te, tv: (layer, te[g], 0, 0)),
                  pl.BlockSpec((None, None, 1, 2 * D_EXPERT), lambda g, te, tv: (layer, te[g], 0, 0)),
                  pl.BlockSpec((None, None, D_EXPERT, D_MODEL), lambda g, te, tv: (layer, te[g], 0, 0)),
                  pl.BlockSpec((None, None, 1, D_MODEL), lambda g, te, tv: (layer, te[g], 0, 0))],
        out_specs=pl.BlockSpec((tm, D_MODEL), lambda g, te, tv: (g, 0)),
        scratch_shapes=[pltpu.VMEM((D_MODEL, 2 * D_EXPERT), BF16),
                        pltpu.VMEM((D_EXPERT, D_MODEL), BF16)])
    return pl.pallas_call(
        _expert_kernel,
        out_shape=jax.ShapeDtypeStruct((n_rows, D_MODEL), F32),
        grid_spec=grid_spec,
        compiler_params=_params("arbitrary"),
        name="moe_experts",
    )(tile_e, tile_rows, xg, w_up, b_up.reshape(n_layers, N_EXPERTS, 1, 2 * D_EXPERT),
      w_down, b_down.reshape(n_layers, N_EXPERTS, 1, D_MODEL))


def _combine_ln_kernel(dest_ref, dest_next_ref, x_ref, gate_ref, g_ref, beta_ref, yg_hbm, *rest, head_steps):
    *o_refs, buf_ref, sem = rest
    i = pl.program_id(0)
    n_steps = pl.num_programs(0)
    tm = x_ref.shape[0]

    def row_copy(d_ref, slot, r, k):
        return pltpu.make_async_copy(yg_hbm.at[pl.ds(d_ref[k * tm + r], 1)],
                                     buf_ref.at[slot, k, pl.ds(r, 1)], sem.at[slot])

    def issue(d_ref, slot):
        def body(r, c):
            for k in range(TOP_K):
                row_copy(d_ref, slot, r, k).start(priority=k % 2)
            return c
        lax.fori_loop(0, tm, body, 0, unroll=8)

    @pl.when(i == 0)
    def _():
        issue(dest_ref, 0)

    @pl.when(i + 1 < n_steps)
    def _():
        issue(dest_next_ref, (i + 1) % 2)

    slot = i % 2

    def drain(r, c):
        for k in range(TOP_K):
            row_copy(dest_ref, slot, r, k).wait()
        return c

    lax.fori_loop(0, tm, drain, 0, unroll=8)
    gates = gate_ref[...]
    y = gates[:, 0:1] * buf_ref[slot, 0]
    for k in range(1, TOP_K):
        y = y + gates[:, k:k + 1] * buf_ref[slot, k]
    out = _layer_norm(DEEPNORM_ALPHA * x_ref[...] + y, g_ref[...], beta_ref[...])
    if len(o_refs) == 1:
        o_refs[0][...] = out
    else:
        @pl.when(i < head_steps)
        def _():
            o_refs[0][...] = out

        @pl.when(i >= head_steps)
        def _():
            o_refs[1][...] = out


def _combine_ln(x, yg, dest, gates_t, g, beta, split=None):
    n = x.shape[0]
    tm = _row_tile(n, COMBINE_TOKEN_TILE)
    dest = _slot_rows_by_tile(dest, tm)
    n_steps = n // tm
    row = pl.BlockSpec((tm, D_MODEL), lambda i: (i, 0))
    vec = pl.BlockSpec((1, D_MODEL), lambda i: (0, 0))
    if split is None:
        head_steps, out_shape, out_specs = n_steps, jax.ShapeDtypeStruct((n, D_MODEL), F32), row
    else:
        assert split % tm == 0
        head_steps = split // tm
        out_shape = (jax.ShapeDtypeStruct((split, D_MODEL), F32), jax.ShapeDtypeStruct((n - split, D_MODEL), F32))
        out_specs = (pl.BlockSpec((tm, D_MODEL), lambda i: (jnp.minimum(i, head_steps - 1), 0)),
                     pl.BlockSpec((tm, D_MODEL), lambda i: (jnp.maximum(i - head_steps, 0), 0)))
    return pl.pallas_call(
        functools.partial(_combine_ln_kernel, head_steps=head_steps),
        out_shape=out_shape,
        grid=(n_steps,),
        in_specs=[pl.BlockSpec((TOP_K * tm,), lambda i: (i,), memory_space=pltpu.SMEM),
                  pl.BlockSpec((TOP_K * tm,), lambda i: (jnp.minimum(i + 1, n_steps - 1),),
                               memory_space=pltpu.SMEM),
                  row, pl.BlockSpec((tm, TOP_K), lambda i: (i, 0)), vec, vec,
                  pl.BlockSpec(memory_space=pl.ANY)],
        out_specs=out_specs,
        scratch_shapes=[pltpu.VMEM((2, TOP_K, tm, D_MODEL), F32), pltpu.SemaphoreType.DMA((2,))],
        compiler_params=_params("arbitrary"),
        name="moe_combine_ln",
    )(dest, dest, x, gates_t, g.reshape(1, D_MODEL), beta.reshape(1, D_MODEL), yg)


def _moe_ln(x, layer, w_router, b_router, w_up, b_up, w_down, b_down, g, beta, split=None):
    n = x.shape[0]
    tm = EXPERT_TILE_ROWS
    ids, gates, rank, counts = _router(x, w_router, b_router)
    counts = counts[:, 0]
    padded = (counts + tm - 1) // tm * tm
    pend = jnp.cumsum(padded)
    pstart = pend - padded
    expert_io = jnp.arange(N_EXPERTS, dtype=jnp.int32)[:, None, None]
    dest = jnp.sum(jnp.where(ids[None] == expert_io, pstart[:, None, None], 0), axis=0) + rank
    n_tiles = -(-(n * TOP_K) // tm) + N_EXPERTS
    tile_row0 = jnp.arange(n_tiles, dtype=jnp.int32) * tm
    tile_e = jnp.sum((pend[None, :] <= tile_row0[:, None]).astype(jnp.int32), axis=1)
    tile_e = jnp.minimum(tile_e, N_EXPERTS - 1)
    seg_end = jnp.sum(jnp.where(tile_e[:, None] == expert_io[:, 0, 0][None, :], (pstart + counts)[None, :], 0), axis=1)
    tile_rows = jnp.clip(seg_end - tile_row0, 0, tm).astype(jnp.int32)
    last_tile = jnp.where(padded > 0, pend // tm - 1, -1).astype(jnp.int32)
    xg = _dispatch(x, dest, last_tile, n_tiles * tm)
    yg = _experts(tile_e, tile_rows, xg, layer, w_up, b_up, w_down, b_down)
    return _combine_ln(x, yg, dest, gates.T, g, beta, split)


def kernel(x_prompt, x_sample, cache_k, cache_v, state_conv, state_h, page_table, ln_g, ln_b, w_qkv, w_o, w_in, b_in, conv_w, conv_b, w_gate_a, b_gate_a, w_gate_x, b_gate_x, lru_lambda, w_out, b_out, w_router, b_router, w_up, b_up, w_down, b_down):
    bp, tp, d = x_prompt.shape
    db, ts, _ = x_sample.shape
    n_p, n_s = bp * tp, db * ts
    xp = x_prompt.reshape(n_p, d)
    xs = x_sample.reshape(n_s, d)
    slopes = 2.0 ** (-8.0 * jnp.arange(1, N_HEADS + 1, dtype=F32) / N_HEADS)
    zeros_d = jnp.zeros((d,), F32)

    def moe_block(x, layer, split=None):
        return _moe_ln(x, layer, w_router[layer], b_router[layer], w_up, b_up, w_down, b_down,
                       ln_g[layer, 1], ln_b[layer, 1], split)

    n = n_p + n_s

    w_qkv_bf = w_qkv[0].astype(BF16)
    w_o_bf = w_o[0].astype(BF16)
    wkt_bf = w_qkv[0][:, d:2 * d].T.astype(BF16)
    wvt_bf = w_qkv[0][:, 2 * d:].T.astype(BF16)
    wqt_bf = w_qkv[0][:, :d].T.astype(BF16)
    qtp, kp, ktp, vtp = _qkv_prompt(xp, wqt_bf, w_qkv_bf[:, d:2 * d], wkt_bf, wvt_bf, bp, tp)
    qs, ks, vs = _qkv(xs, w_qkv_bf)
    op = _moba_prompt(qtp, kp, vtp, slopes, bp, tp)
    n_phys = cache_k.shape[1]
    kt_pool = jnp.transpose(cache_k[0], (0, 2, 3, 1)).reshape(n_phys, d, PAGE_SIZE)
    vt_pool = jnp.transpose(cache_v[0], (0, 2, 3, 1)).reshape(n_phys, d, PAGE_SIZE)
    os_ = _moba_sample(qs, ks, vs, kt_pool, vt_pool, page_table, slopes, db, ts)
    x = _linear_res_ln(op, w_o_bf, zeros_d, xp, ln_g[0, 0], ln_b[0, 0], out_rows=n)
    x = _linear_res_ln(os_, w_o_bf, zeros_d, xs, ln_g[0, 0], ln_b[0, 0], out_rows=n, row0=n_p, into=x)
    x = moe_block(x, 0)

    w_in_bf = w_in[0].astype(BF16)
    wa_bf = w_gate_a[0].astype(BF16)
    wx_bf = w_gate_x[0].astype(BF16)
    rec = (w_in_bf, b_in[0], conv_w[0], conv_b[0], wa_bf, b_gate_a[0].reshape(-1), wx_bf, b_gate_x[0].reshape(-1),
           lru_lambda[0], w_out[0].astype(BF16), b_out[0], ln_g[1, 0], ln_b[1, 0])
    conv0 = jnp.zeros((bp, CONV_W - 1, D_RNN), F32)
    h0 = jnp.zeros((bp, D_RNN), F32)
    x1, conv_p, h_p = _rglru(x, 0, conv0, h0, *rec, bp, tp)
    x1, conv_s, h_s = _rglru(x, n_p, state_conv[0], state_h[0], *rec, db, ts, into=x1)
    xp, xs = moe_block(x1, 1, split=n_p)

    def kv_rows(a_t):
        return jnp.transpose(a_t.reshape(1, bp, N_HEADS, HEAD_DIM, tp), (0, 1, 4, 2, 3))

    kv_s = (1, db, ts, N_HEADS, HEAD_DIM)
    return (xp.reshape(bp, tp, d), xs.reshape(db, ts, d),
            kv_rows(ktp), kv_rows(vtp), conv_p[None], h_p.reshape(1, bp, D_RNN),
            ks.reshape(kv_s), vs.reshape(kv_s), conv_s[None], h_s.reshape(1, db, D_RNN))
```

```python
import functools

import jax
import jax.numpy as jnp
from jax import lax
from jax.experimental import pallas as pl
from jax.experimental.pallas import tpu as pltpu

F32 = jnp.float32
BF16 = jnp.bfloat16

D_MODEL = 1024
N_HEADS = 16
HEAD_DIM = D_MODEL // N_HEADS
MOBA_BLOCK = 256
MOBA_TOPK = 3
PAGE_SIZE = 128
PAGES_PER_BLOCK = MOBA_BLOCK // PAGE_SIZE
D_RNN = D_MODEL
LRU_BLOCK_WIDTH = 256
N_LRU_BLOCKS = D_RNN // LRU_BLOCK_WIDTH
CONV_W = 4
LRU_C = 8.0
N_EXPERTS = 32
TOP_K = 4
D_EXPERT = D_MODEL
SWIGLU_LIMIT = 7.0
SWIGLU_ALPHA = 1.702
DEPTH = 2
DEEPNORM_ALPHA = (2 * DEPTH) ** 0.25
LN_EPS = 1e-5
NEG_INF = -1e30

V7X_VMEM_LIMIT_BYTES = 56 * 1024 * 1024
SUBLANES = 8
LANES = 128
CONV_HALO = SUBLANES
EXPERT_TILE_ROWS = 512
RGLRU_TIME_TILE = 512
ROUTER_TILE = 256
SAMPLE_BLOCKS_PER_STEP = 8
DISPATCH_TOKEN_TILE = 1280
COMBINE_TOKEN_TILE = 256


def _params(*sem):
    return pltpu.CompilerParams(dimension_semantics=sem, vmem_limit_bytes=V7X_VMEM_LIMIT_BYTES)


def _row_tile(m, pref):
    t = min(pref, m)
    while m % t or t % SUBLANES:
        t -= SUBLANES
    return t


def _div_pow2(x, n):
    assert n & (n - 1) == 0, n
    return lax.shift_right_logical(x, n.bit_length() - 1)


def _layer_norm(y, g, b):
    mu = jnp.mean(y, axis=-1, keepdims=True)
    yc = y - mu
    var = jnp.mean(yc * yc, axis=-1, keepdims=True)
    return yc * lax.rsqrt(var + LN_EPS) * g + b


def _qkv_kernel(x_ref, w_ref, q_ref, k_ref, v_ref):
    x = x_ref[...].astype(BF16)
    for c, o_ref in enumerate((q_ref, k_ref, v_ref)):
        o_ref[...] = jnp.dot(x, w_ref[:, c * D_MODEL:(c + 1) * D_MODEL], preferred_element_type=F32)


def _qkv(x, w_bf):
    m = x.shape[0]
    tm = _row_tile(m, 512)
    out = jax.ShapeDtypeStruct((m, D_MODEL), F32)
    row = pl.BlockSpec((tm, D_MODEL), lambda i: (i, 0))
    return pl.pallas_call(
        _qkv_kernel,
        out_shape=(out, out, out),
        grid=(m // tm,),
        in_specs=[row, pl.BlockSpec((D_MODEL, 3 * D_MODEL), lambda i: (0, 0))],
        out_specs=(row, row, row),
        compiler_params=_params("arbitrary"),
        name="qkv_proj",
    )(x, w_bf)


def _qkv_prompt_kernel(x_ref, wqt_ref, wk_ref, wkt_ref, wvt_ref, qt_ref, k_ref, kt_ref, vt_ref):
    x = x_ref[...].astype(BF16)
    nt = (((1,), (1,)), ((), ()))
    qt_ref[...] = lax.dot_general(wqt_ref[...], x, nt, preferred_element_type=F32)
    k_ref[...] = jnp.dot(x, wk_ref[...], preferred_element_type=F32)
    kt_ref[...] = lax.dot_general(wkt_ref[...], x, nt, preferred_element_type=F32)
    vt_ref[...] = lax.dot_general(wvt_ref[...], x, nt, preferred_element_type=F32)


def _qkv_prompt(x, wqt_bf, wk_bf, wkt_bf, wvt_bf, b, t):
    tm = _row_tile(t, 512)
    nt = t // tm
    w = pl.BlockSpec((D_MODEL, D_MODEL), lambda bi, i: (0, 0))
    row = pl.BlockSpec((tm, D_MODEL), lambda bi, i: (bi * nt + i, 0))
    col = pl.BlockSpec((None, D_MODEL, tm), lambda bi, i: (bi, 0, i))
    t_out = jax.ShapeDtypeStruct((b, D_MODEL, t), F32)
    return pl.pallas_call(
        _qkv_prompt_kernel,
        out_shape=(t_out, jax.ShapeDtypeStruct((b * t, D_MODEL), F32), t_out, t_out),
        grid=(b, nt),
        in_specs=[row, w, w, w, w],
        out_specs=(col, row, col, col),
        compiler_params=_params("arbitrary", "arbitrary"),
        name="qkv_proj_prompt",
    )(x, wqt_bf, wk_bf, wkt_bf, wvt_bf)


def _linear_res_ln_kernel(a_ref, w_ref, b_ref, res_ref, g_ref, beta_ref, *rest):
    o_ref = rest[-1]
    m = jnp.dot(a_ref[...].astype(BF16), w_ref[...], preferred_element_type=F32) + b_ref[...]
    o_ref[...] = _layer_norm(DEEPNORM_ALPHA * res_ref[...] + m, g_ref[...], beta_ref[...])


def _linear_res_ln(a, w_bf, b, res, g, beta, *, out_rows=None, row0=0, into=None):
    m, k = a.shape
    n = w_bf.shape[1]
    out_rows = m if out_rows is None else out_rows
    tm = _row_tile(m, 512)
    assert row0 % tm == 0
    blk0 = row0 // tm
    row_in = pl.BlockSpec((tm, k), lambda i: (i, 0))
    row = pl.BlockSpec((tm, n), lambda i: (i, 0))
    vec = pl.BlockSpec((1, n), lambda i: (0, 0))
    in_specs = [row_in, pl.BlockSpec((k, n), lambda i: (0, 0)), vec, row, vec, vec]
    args = [a, w_bf, b.reshape(1, n), res, g.reshape(1, n), beta.reshape(1, n)]
    aliases = {}
    if into is not None:
        in_specs.append(pl.BlockSpec(memory_space=pl.ANY))
        args.append(into)
        aliases = {len(args) - 1: 0}
    return pl.pallas_call(
        _linear_res_ln_kernel,
        out_shape=jax.ShapeDtypeStruct((out_rows, n), F32),
        grid=(m // tm,),
        in_specs=in_specs,
        out_specs=pl.BlockSpec((tm, n), lambda i: (blk0 + i, 0)),
        input_output_aliases=aliases,
        compiler_params=_params("arbitrary"),
        name="linear_res_ln",
    )(*args)


def _slot_rank(gm, n_blocks, axis):
    blk = lax.broadcasted_iota(jnp.int32, gm.shape, axis)
    rank = jnp.zeros(gm.shape, jnp.int32)
    for m in range(n_blocks):
        g_m = jnp.sum(jnp.where(blk == m, gm, 0.0), axis=axis, keepdims=True)
        beats = (g_m > gm) | ((g_m == gm) & (m < blk))
        rank = rank + jnp.where(beats, 1, 0)
    return rank


def _moba_prompt_kernel(slope_ref, qt_ref, k_ref, vt_ref, o_ref, *, t):
    nb = t // MOBA_BLOCK
    n_sel = min(MOBA_TOPK, nb)
    bs = MOBA_BLOCK
    hp = pl.program_id(1)
    heads_per_step = LANES // HEAD_DIM

    key_io = lax.broadcasted_iota(jnp.int32, (bs, bs), 0)
    qry_io = lax.broadcasted_iota(jnp.int32, (bs, bs), 1)
    causal_bias = jnp.where(key_io <= qry_io, 0.0, NEG_INF)

    blk = lax.broadcasted_iota(jnp.int32, (nb, t), 0)
    own = _div_pow2(lax.broadcasted_iota(jnp.int32, (nb, t), 1), MOBA_BLOCK)

    def row_of(a, j):
        return jnp.sum(jnp.where(blk == j, a, 0.0), axis=0, keepdims=True)

    outs = []
    for h in range(heads_per_step):
        slope = slope_ref[hp * heads_per_step + h]
        lo, hi = h * HEAD_DIM, (h + 1) * HEAD_DIM
        qt = qt_ref[lo:hi, :]
        kr = k_ref[:, lo:hi]
        vt = vt_ref[lo:hi, :]

        kmean = jnp.sum(kr.reshape(nb, bs, HEAD_DIM), axis=1) * (1.0 / bs)
        g_t = jnp.dot(kmean.astype(BF16), qt.astype(BF16), preferred_element_type=F32)
        gm = jnp.where(blk < own, g_t, NEG_INF)
        rank = _slot_rank(gm, nb, 0)
        chosen = (rank < n_sel) & (rank < own)
        sel_bias = jnp.where(chosen, 0.0, NEG_INF)
        own_mult = jnp.where(chosen, 2.0, 1.0)
        sel_rows = [row_of(sel_bias, j) for j in range(nb)]
        mult_rows = [row_of(own_mult, j) for j in range(nb)]

        qs = (qt * (HEAD_DIM ** -0.5)).astype(BF16)
        kb = kr.astype(BF16)
        vb = vt.astype(BF16)
        alibi = [slope * (key_io + j * bs).astype(F32) for j in range(nb)]
        o_blocks = []
        for i in range(nb):
            cols = slice(i * bs, (i + 1) * bs)
            q_i = qs[:, cols]
            tiles, maxes = [], []
            for j in range(i + 1):
                s_ij = jnp.dot(kb[j * bs:(j + 1) * bs], q_i, preferred_element_type=F32) + alibi[j]
                if j == i:
                    s_ij = s_ij + causal_bias
                mx = jnp.max(s_ij, axis=0, keepdims=True)
                tiles.append(s_ij)
                maxes.append(mx + sel_rows[j][:, cols] if j < i else mx)
            m = functools.reduce(jnp.maximum, maxes)
            ps = [jnp.exp(s_ij + ((sel_rows[j][:, cols] - m) if j < i else -m)) for j, s_ij in enumerate(tiles)]
            ps[-1] = ps[-1] * mult_rows[i][:, cols]
            inv = 1.0 / functools.reduce(jnp.add, [jnp.sum(p, axis=0, keepdims=True) for p in ps])
            p_t = jnp.concatenate([(p * inv).astype(BF16) for p in ps], axis=0)
            o_blocks.append(jnp.dot(vb[:, :(i + 1) * bs], p_t, preferred_element_type=F32))
        outs.append(jnp.concatenate(o_blocks, axis=1))
    o_ref[...] = jnp.transpose(jnp.concatenate(outs, axis=0))


def _moba_prompt(qt, k, vt, slopes, b, t):
    blk = pl.BlockSpec((t, LANES), lambda bi, hp: (bi, hp))
    blk_t = pl.BlockSpec((None, LANES, t), lambda bi, hp: (bi, hp, 0))
    return pl.pallas_call(
        functools.partial(_moba_prompt_kernel, t=t),
        out_shape=jax.ShapeDtypeStruct((b * t, D_MODEL), F32),
        grid=(b, D_MODEL // LANES),
        in_specs=[pl.BlockSpec(memory_space=pltpu.SMEM), blk_t, blk, blk_t],
        out_specs=blk,
        compiler_params=_params("arbitrary", "arbitrary"),
        name="moba_prompt",
    )(slopes, qt, k, vt)


def _moba_sample_kernel(pt_ref, slope_ref, q_ref, kn_ref, vn_ref, *refs, t, nbp, bps):
    del pt_ref
    n_pg = bps * PAGES_PER_BLOCK
    k_refs, v_refs = refs[:n_pg], refs[n_pg:2 * n_pg]
    o_ref, qbd_ref, kmean_ref, s_ref, pown_ref, l_ref, acc_ref = refs[2 * n_pg:]
    k_steps = nbp // bps
    step = pl.program_id(1)
    bs = MOBA_BLOCK
    hq = N_HEADS * t
    past_len = nbp * bs
    own_blk = nbp
    nb = nbp + 1
    n_sel = min(MOBA_TOPK, nb)

    row = lax.broadcasted_iota(jnp.int32, (hq, bs), 0)
    lane = lax.broadcasted_iota(jnp.int32, (hq, bs), 1)
    assert t & (t - 1) == 0, t
    qpos = past_len + (row & (t - 1))
    slope = slope_ref[...]

    def head_diag(x):
        r = _div_pow2(lax.broadcasted_iota(jnp.int32, x.shape, 0), t)
        c = _div_pow2(lax.broadcasted_iota(jnp.int32, x.shape, 1), HEAD_DIM)
        return jnp.where(r == c, x, 0.0)

    @pl.when(step == 0)
    def _():
        q_rep = jnp.concatenate([q_ref[...]] * N_HEADS, axis=0)
        qbd_ref[...] = head_diag(q_rep)
        kmean_ref[...] = jnp.zeros_like(kmean_ref)
        acc_ref[...] = jnp.zeros_like(acc_ref)

    @pl.when(step < k_steps)
    def _():
        qbd = (qbd_ref[...] * (HEAD_DIM ** -0.5)).astype(BF16)
        blk_lane = lax.broadcasted_iota(jnp.int32, kmean_ref.shape, 1)
        kmean_new = kmean_ref[...]
        for jb in range(bps):
            n = step * bps + jb
            ka, kb = k_refs[2 * jb][...], k_refs[2 * jb + 1][...]
            ksum = jnp.sum(ka + kb, axis=1, keepdims=True) * (1.0 / bs)
            kmean_new = kmean_new + jnp.where(blk_lane == n, ksum, 0.0)
            s = jnp.concatenate([jnp.dot(qbd, ka.astype(BF16), preferred_element_type=F32),
                                 jnp.dot(qbd, kb.astype(BF16), preferred_element_type=F32)], axis=1)
            dist = qpos - (n * bs + lane)
            s_ref[n] = s - slope * dist.astype(F32)
        kmean_ref[...] = kmean_new

    @pl.when(step == k_steps - 1)
    def _():
        qbd = qbd_ref[...]
        g = jnp.dot(qbd.astype(BF16), kmean_ref[...].astype(BF16), preferred_element_type=F32)
        col = lax.broadcasted_iota(jnp.int32, g.shape, 1)
        gm = jnp.where(col < own_blk, g, NEG_INF)
        rank = _slot_rank(gm, nb, 1)
        chosen = (rank < n_sel) & (rank < own_blk) & (col < nb)

        kn = jnp.concatenate([kn_ref[...], jnp.zeros((LANES - t, D_MODEL), F32)], axis=0)
        s_own = lax.dot_general((qbd * (HEAD_DIM ** -0.5)).astype(BF16), kn.astype(BF16),
                                (((1,), (1,)), ((), ())), preferred_element_type=F32)
        dist_own = (qpos - (past_len + lane))[:, :LANES]
        s_own = s_own - slope[:, :LANES] * dist_own.astype(F32)
        s_own = jnp.where(dist_own >= 0, s_own, NEG_INF)

        def sel_col(n):
            return jnp.sum(jnp.where((col == n) & chosen, 1.0, 0.0), axis=1, keepdims=True)

        def max_body(n, m):
            sn = jnp.where(sel_col(n) > 0.0, s_ref[n], NEG_INF)
            return jnp.maximum(m, jnp.max(sn, axis=1, keepdims=True))

        m = lax.fori_loop(0, nbp, max_body, jnp.max(s_own, axis=1, keepdims=True))

        def e_body(n, l):
            e = jnp.where(sel_col(n) > 0.0, jnp.exp(s_ref[n] - m), 0.0)
            s_ref[n] = e
            return l + jnp.sum(e, axis=1, keepdims=True)

        own_mult = 1.0 + sel_col(own_blk)
        e_own = jnp.exp(s_own - m) * own_mult
        l = lax.fori_loop(0, nbp, e_body, jnp.sum(e_own, axis=1, keepdims=True))
        pown_ref[...] = (e_own / l).astype(BF16)
        l_ref[...] = jnp.broadcast_to(l, l_ref.shape)

    @pl.when(step >= k_steps)
    def _():
        nt = (((1,), (1,)), ((), ()))
        acc = acc_ref[...]
        for jb in range(bps):
            p = (s_ref[(step - k_steps) * bps + jb] / l_ref[:, :1]).astype(BF16)
            for half in range(PAGES_PER_BLOCK):
                acc = acc + lax.dot_general(p[:, half * PAGE_SIZE:(half + 1) * PAGE_SIZE],
                                            v_refs[2 * jb + half][...].astype(BF16), nt,
                                            preferred_element_type=F32)
        acc_ref[...] = acc

    @pl.when(step == 2 * k_steps - 1)
    def _():
        vn = jnp.concatenate([vn_ref[...], jnp.zeros((LANES - t, D_MODEL), F32)], axis=0).astype(BF16)
        acc = acc_ref[...] + jnp.dot(pown_ref[...], vn, preferred_element_type=F32)
        acc = head_diag(acc)
        o_ref[...] = jnp.sum(acc.reshape(N_HEADS, t, D_MODEL), axis=0)


def _moba_sample(q, k_new, v_new, kt_pool, vt_pool, page_table, slopes, db, t):
    n_pages = page_table.shape[1]
    assert n_pages % PAGES_PER_BLOCK == 0 and PAGES_PER_BLOCK == 2 and t <= LANES
    nbp = n_pages // PAGES_PER_BLOCK
    bps = min(SAMPLE_BLOCKS_PER_STEP, nbp)
    assert nbp % bps == 0
    k_steps = nbp // bps
    n_pg = bps * PAGES_PER_BLOCK
    hq = N_HEADS * t
    nb_pad = -(-(nbp + 1) // LANES) * LANES
    slope_rows = jnp.broadcast_to(jnp.repeat(slopes, t)[:, None], (hq, MOBA_BLOCK))
    new_rows = pl.BlockSpec((t, D_MODEL), lambda b, s, pt: (b, 0))

    def page(first, j):
        def index(b, s, pt):
            return (pt[b, jnp.clip(s - first, 0, k_steps - 1) * n_pg + j], 0, 0)
        return pl.BlockSpec((None, D_MODEL, PAGE_SIZE), index)

    grid_spec = pltpu.PrefetchScalarGridSpec(
        num_scalar_prefetch=1,
        grid=(db, 2 * k_steps),
        in_specs=[pl.BlockSpec((hq, MOBA_BLOCK), lambda b, s, pt: (0, 0)),
                  new_rows, new_rows, new_rows,
                  *[page(0, j) for j in range(n_pg)], *[page(k_steps, j) for j in range(n_pg)]],
        out_specs=new_rows,
        scratch_shapes=[pltpu.VMEM((hq, D_MODEL), F32),
                        pltpu.VMEM((D_MODEL, nb_pad), F32),
                        pltpu.VMEM((nbp, hq, MOBA_BLOCK), F32),
                        pltpu.VMEM((hq, LANES), BF16),
                        pltpu.VMEM((hq, LANES), F32),
                        pltpu.VMEM((hq, D_MODEL), F32)])
    return pl.pallas_call(
        functools.partial(_moba_sample_kernel, t=t, nbp=nbp, bps=bps),
        out_shape=jax.ShapeDtypeStruct((db * t, D_MODEL), F32),
        grid_spec=grid_spec,
        compiler_params=_params("arbitrary", "arbitrary"),
        name="moba_sample",
    )(page_table, slope_rows, q, k_new, v_new, *([kt_pool] * n_pg), *([vt_pool] * n_pg))


def _rglru_kernel(x_ref, win_ref, bin_ref, conv0_ref, h0_ref, cw_ref, cb_ref, wa_ref, ba_ref, wx_ref, bx_ref,
                  lam_ref, wout_ref, bout_ref, g_ref, beta_ref, *rest, tt):
    o_ref, conv_ref, hlast_ref, halo_ref, h_ref, a_ref, b_ref, hs_ref = rest[-8:]
    i = pl.program_id(1)
    keep = CONV_W - 1
    first = CONV_HALO - keep

    @pl.when(i == 0)
    def _():
        halo_ref[first:CONV_HALO, :] = conv0_ref[...]
        h_ref[...] = h0_ref[...]

    x = x_ref[...]
    u = jnp.dot(x.astype(BF16), win_ref[...], preferred_element_type=F32) + bin_ref[...]
    y_in = u[:, :D_RNN]
    y_branch = 0.5 * y_in * (1.0 + jnp.tanh(0.7978845608028654 * (y_in + 0.044715 * (y_in * y_in * y_in))))
    halo_ref[CONV_HALO:CONV_HALO + tt, :] = u[:, D_RNN:]
    xc = cb_ref[...]
    acc = None
    for j in range(CONV_W):
        term = halo_ref[first + j:first + j + tt, :] * cw_ref[j:j + 1, :]
        acc = term if acc is None else acc + term
    xc = xc + acc

    r_parts, i_parts = [], []
    for blk in range(N_LRU_BLOCKS):
        xb = xc[:, blk * LRU_BLOCK_WIDTH:(blk + 1) * LRU_BLOCK_WIDTH].astype(BF16)
        r_parts.append(jnp.dot(xb, wa_ref[blk], preferred_element_type=F32))
        i_parts.append(jnp.dot(xb, wx_ref[blk], preferred_element_type=F32))
    r = jax.nn.sigmoid(jnp.concatenate(r_parts, axis=1) + ba_ref[...])
    ig = jax.nn.sigmoid(jnp.concatenate(i_parts, axis=1) + bx_ref[...])
    log_a = LRU_C * r * jax.nn.log_sigmoid(lam_ref[...])
    a_ref[...] = jnp.exp(log_a)
    b_ref[...] = jnp.sqrt(1.0 - jnp.exp(2.0 * log_a)) * (ig * xc)

    def step(s, h):
        h = a_ref[pl.ds(s, 1), :] * h + b_ref[pl.ds(s, 1), :]
        hs_ref[pl.ds(s, 1), :] = h
        return h

    h = lax.fori_loop(0, tt, step, h_ref[...], unroll=8)
    h_ref[...] = h
    hy = (hs_ref[...] * y_branch).astype(BF16)
    m = jnp.dot(hy, wout_ref[...], preferred_element_type=F32) + bout_ref[...]
    o_ref[...] = _layer_norm(DEEPNORM_ALPHA * x + m, g_ref[...], beta_ref[...])
    tail = halo_ref[first + tt:CONV_HALO + tt, :]
    halo_ref[first:CONV_HALO, :] = tail

    @pl.when(i == pl.num_programs(1) - 1)
    def _():
        conv_ref[...] = tail
        hlast_ref[...] = h


def _rglru(x, row0, conv0, h0, w_in_bf, b_in, cw, cb, wa_bf, ba, wx_bf, bx, lam, w_out_bf, b_out, g, beta,
           b, t, *, into=None):
    tt = _row_tile(t, RGLRU_TIME_TILE)
    nt = t // tt
    assert row0 % tt == 0
    blk0 = row0 // tt
    keep = CONV_W - 1
    vec = pl.BlockSpec((1, D_RNN), lambda bi, i: (0, 0))
    gate_w = pl.BlockSpec((N_LRU_BLOCKS, LRU_BLOCK_WIDTH, LRU_BLOCK_WIDTH), lambda bi, i: (0, 0, 0))
    state3 = pl.BlockSpec((None, keep, D_RNN), lambda bi, i: (bi, 0, 0))
    state1 = pl.BlockSpec((None, 1, D_RNN), lambda bi, i: (bi, 0, 0))
    rows = pl.BlockSpec((tt, D_MODEL), lambda bi, i: (blk0 + bi * nt + i, 0))
    whole = lambda shape: pl.BlockSpec(shape, lambda bi, i: (0,) * len(shape))
    in_specs = [rows, whole((D_MODEL, 2 * D_RNN)), whole((1, 2 * D_RNN)),
                state3, state1, whole((CONV_W, D_RNN)), vec,
                gate_w, vec, gate_w, vec, vec,
                whole((D_RNN, D_MODEL)), whole((1, D_MODEL)), whole((1, D_MODEL)), whole((1, D_MODEL))]
    args = [x, w_in_bf, b_in.reshape(1, 2 * D_RNN), conv0, h0.reshape(b, 1, D_RNN), cw, cb.reshape(1, D_RNN),
            wa_bf, ba.reshape(1, D_RNN), wx_bf, bx.reshape(1, D_RNN), lam.reshape(1, D_RNN),
            w_out_bf, b_out.reshape(1, D_MODEL), g.reshape(1, D_MODEL), beta.reshape(1, D_MODEL)]
    aliases = {}
    if into is not None:
        in_specs.append(pl.BlockSpec(memory_space=pl.ANY))
        args.append(into)
        aliases = {len(args) - 1: 0}
    return pl.pallas_call(
        functools.partial(_rglru_kernel, tt=tt),
        out_shape=(jax.ShapeDtypeStruct((x.shape[0], D_MODEL), F32),
                   jax.ShapeDtypeStruct((b, keep, D_RNN), F32),
                   jax.ShapeDtypeStruct((b, 1, D_RNN), F32)),
        grid=(b, nt),
        in_specs=in_specs,
        out_specs=(rows, state3, state1),
        input_output_aliases=aliases,
        scratch_shapes=[pltpu.VMEM((CONV_HALO + tt, D_RNN), F32),
                        pltpu.VMEM((1, D_RNN), F32),
                        pltpu.VMEM((tt, D_RNN), F32),
                        pltpu.VMEM((tt, D_RNN), F32),
                        pltpu.VMEM((tt, D_RNN), F32)],
        compiler_params=_params("arbitrary", "arbitrary"),
        name="rglru",
    )(*args)


def _router_kernel(x_ref, wt_ref, b_ref, ids_ref, gates_ref, rank_ref, counts_ref, carry_ref):
    i = pl.program_id(0)
    tm = x_ref.shape[0]

    @pl.when(i == 0)
    def _():
        carry_ref[...] = jnp.zeros_like(carry_ref)

    logits = lax.dot_general(wt_ref[...].astype(BF16), x_ref[...].astype(BF16), (((1,), (1,)), ((), ())),
                             preferred_element_type=F32) + b_ref[:, :1]
    e_io = lax.broadcasted_iota(jnp.int32, logits.shape, 0)
    lt = logits
    vals, ids = [], []
    for _ in range(TOP_K):
        mx = jnp.max(lt, axis=0, keepdims=True)
        idx = jnp.min(jnp.where(lt == mx, e_io, N_EXPERTS), axis=0, keepdims=True)
        vals.append(mx)
        ids.append(idx)
        lt = jnp.where(e_io == idx, -jnp.inf, lt)
    ex = [jnp.exp(v - vals[0]) for v in vals]
    den = ex[0] + ex[1] + ex[2] + ex[3]
    gates_ref[...] = jnp.concatenate([e / den for e in ex], axis=0)
    ids_ref[...] = jnp.concatenate(ids, axis=0)

    onehot = jnp.zeros(logits.shape, F32)
    for idx in ids:
        onehot = onehot + jnp.where(e_io == idx, 1.0, 0.0)
    s_io = lax.broadcasted_iota(jnp.int32, (tm, tm), 0)
    t_io = lax.broadcasted_iota(jnp.int32, (tm, tm), 1)
    before = jnp.where(s_io < t_io, 1.0, 0.0).astype(BF16)
    prefix = jnp.dot(onehot.astype(BF16), before, preferred_element_type=F32) + carry_ref[:, :1]
    ranks = [jnp.sum(jnp.where(e_io == idx, prefix, 0.0), axis=0, keepdims=True) for idx in ids]
    rank_ref[...] = jnp.concatenate(ranks, axis=0).astype(jnp.int32)
    carry_ref[...] += jnp.sum(onehot, axis=1, keepdims=True)
    counts_ref[...] = carry_ref[...].astype(jnp.int32)


def _router(x, w_router, b_router):
    n = x.shape[0]
    tm = _row_tile(n, ROUTER_TILE)
    slot = pl.BlockSpec((TOP_K, tm), lambda i: (0, i))
    per_expert = pl.BlockSpec((N_EXPERTS, LANES), lambda i: (0, 0))
    return pl.pallas_call(
        _router_kernel,
        out_shape=(jax.ShapeDtypeStruct((TOP_K, n), jnp.int32),
                   jax.ShapeDtypeStruct((TOP_K, n), F32),
                   jax.ShapeDtypeStruct((TOP_K, n), jnp.int32),
                   jax.ShapeDtypeStruct((N_EXPERTS, LANES), jnp.int32)),
        grid=(n // tm,),
        in_specs=[pl.BlockSpec((tm, D_MODEL), lambda i: (i, 0)),
                  pl.BlockSpec((N_EXPERTS, D_MODEL), lambda i: (0, 0)),
                  per_expert],
        out_specs=(slot, slot, slot, per_expert),
        scratch_shapes=[pltpu.VMEM((N_EXPERTS, LANES), F32)],
        compiler_params=_params("arbitrary"),
        name="moe_router",
    )(x, w_router.T, jnp.broadcast_to(b_router[:, None], (N_EXPERTS, LANES)))


def _dispatch_kernel(last_ref, dest_ref, x_ref, xg_hbm, zero_ref, sem, zero_sem):
    tm = x_ref.shape[0]
    rows = zero_ref.shape[0]

    def zero_copy(e):
        return pltpu.make_async_copy(zero_ref, xg_hbm.at[pl.ds(last_ref[e] * rows, rows)], zero_sem)

    @pl.when(pl.program_id(0) == 0)
    def _():
        zero_ref[...] = jnp.zeros_like(zero_ref)
        for e in range(N_EXPERTS):
            pl.when(last_ref[e] >= 0)(lambda e=e: zero_copy(e).start())
        for e in range(N_EXPERTS):
            pl.when(last_ref[e] >= 0)(lambda e=e: zero_copy(e).wait())

    def row_copy(r, k):
        return pltpu.make_async_copy(x_ref.at[pl.ds(r, 1)], xg_hbm.at[pl.ds(dest_ref[k * tm + r], 1)], sem)

    def issue(r, c):
        for k in range(TOP_K):
            row_copy(r, k).start(priority=k % 2)
        return c

    def drain(r, c):
        for k in range(TOP_K):
            row_copy(r, k).wait()
        return c

    lax.fori_loop(0, tm, issue, 0, unroll=8)
    lax.fori_loop(0, tm, drain, 0, unroll=8)


def _slot_rows_by_tile(dest, tm):
    k, n = dest.shape
    return dest.reshape(k, n // tm, tm).transpose(1, 0, 2).reshape(-1)


def _dispatch(x, dest, last_tile, n_rows):
    n = x.shape[0]
    tm = _row_tile(n, DISPATCH_TOKEN_TILE)
    dest = _slot_rows_by_tile(dest, tm)
    grid_spec = pltpu.PrefetchScalarGridSpec(
        num_scalar_prefetch=1,
        grid=(n // tm,),
        in_specs=[pl.BlockSpec((TOP_K * tm,), lambda i, last: (i,), memory_space=pltpu.SMEM),
                  pl.BlockSpec((tm, D_MODEL), lambda i, last: (i, 0))],
        out_specs=pl.BlockSpec(memory_space=pl.ANY),
        scratch_shapes=[pltpu.VMEM((EXPERT_TILE_ROWS, D_MODEL), F32),
                        pltpu.SemaphoreType.DMA, pltpu.SemaphoreType.DMA])
    return pl.pallas_call(
        _dispatch_kernel,
        out_shape=jax.ShapeDtypeStruct((n_rows, D_MODEL), F32),
        grid_spec=grid_spec,
        compiler_params=_params("arbitrary"),
        name="moe_dispatch",
    )(last_tile, dest, x)


def _expert_kernel(te_ref, tr_ref, xg_ref, wu_ref, bu_ref, wd_ref, bd_ref, y_ref, wu_bf_ref, wd_bf_ref):
    g = pl.program_id(0)
    e = te_ref[g]
    prev = te_ref[jnp.maximum(g - 1, 0)]
    rows = tr_ref[g]
    half = xg_ref.shape[0] // 2

    @pl.when((g == 0) | (e != prev))
    def _():
        wu_bf_ref[...] = wu_ref[...].astype(BF16)
        wd_bf_ref[...] = wd_ref[...].astype(BF16)

    def ffn(x):
        gu = jnp.dot(x.astype(BF16), wu_bf_ref[...], preferred_element_type=F32) + bu_ref[...]
        gate = jnp.minimum(gu[:, :D_EXPERT], SWIGLU_LIMIT)
        up = jnp.clip(gu[:, D_EXPERT:], -SWIGLU_LIMIT, SWIGLU_LIMIT)
        glu = gate * jax.nn.sigmoid(SWIGLU_ALPHA * gate)
        return jnp.dot(((up + 1.0) * glu).astype(BF16), wd_bf_ref[...], preferred_element_type=F32) + bd_ref[...]

    @pl.when(rows > half)
    def _():
        y_ref[...] = ffn(xg_ref[...])

    @pl.when((rows > 0) & (rows <= half))
    def _():
        y_ref[:half, :] = ffn(xg_ref[:half, :])
        y_ref[half:, :] = jnp.zeros((half, D_MODEL), F32)

    @pl.when(rows == 0)
    def _():
        y_ref[...] = jnp.zeros_like(y_ref)


def _experts(tile_e, tile_rows, xg, layer, w_up, b_up, w_down, b_down):
    n_rows = xg.shape[0]
    tm = EXPERT_TILE_ROWS
    n_layers = w_up.shape[0]
    grid_spec = pltpu.PrefetchScalarGridSpec(
        num_scalar_prefetch=2,
        grid=(n_rows // tm,),
        in_specs=[pl.BlockSpec((tm, D_MODEL), lambda g, te, tv: (g * jnp.minimum(tv[g], 1), 0)),
                  pl.BlockSpec((None, None, D_MODEL, 2 * D_EXPERT), lambda g, te, tv: (layer, te[g], 0, 0)),
                  pl.BlockSpec((None, None, 1, 2 * D_EXPERT), lambda g, te, tv: (layer, te[g], 0, 0)),
                  pl.BlockSpec((None, None, D_EXPERT, D_MODEL), lambda g, te, tv: (layer, te[g], 0, 0)),
                  pl.BlockSpec((None, None, 1, D_MODEL), lambda g, te, tv: (layer, te[g], 0, 0))],
        out_specs=pl.BlockSpec((tm, D_MODEL), lambda g, te, tv: (g, 0)),
        scratch_shapes=[pltpu.VMEM((D_MODEL, 2 * D_EXPERT), BF16),
                        pltpu.VMEM((D_EXPERT, D_MODEL), BF16)])
    return pl.pallas_call(
        _expert_kernel,
        out_shape=jax.ShapeDtypeStruct((n_rows, D_MODEL), F32),
        grid_spec=grid_spec,
        compiler_params=_params("arbitrary"),
        name="moe_experts",
    )(tile_e, tile_rows, xg, w_up, b_up.reshape(n_layers, N_EXPERTS, 1, 2 * D_EXPERT),
      w_down, b_down.reshape(n_layers, N_EXPERTS, 1, D_MODEL))


def _combine_ln_kernel(dest_ref, dest_next_ref, x_ref, gate_ref, g_ref, beta_ref, yg_hbm, *rest, head_steps):
    *o_refs, buf_ref, sem = rest
    i = pl.program_id(0)
    n_steps = pl.num_programs(0)
    tm = x_ref.shape[0]

    def row_copy(d_ref, slot, r, k):
        return pltpu.make_async_copy(yg_hbm.at[pl.ds(d_ref[k * tm + r], 1)],
                                     buf_ref.at[slot, k, pl.ds(r, 1)], sem.at[slot])

    def issue(d_ref, slot):
        def body(r, c):
            for k in range(TOP_K):
                row_copy(d_ref, slot, r, k).start(priority=k % 2)
            return c
        lax.fori_loop(0, tm, body, 0, unroll=8)

    @pl.when(i == 0)
    def _():
        issue(dest_ref, 0)

    @pl.when(i + 1 < n_steps)
    def _():
        issue(dest_next_ref, (i + 1) % 2)

    slot = i % 2

    def drain(r, c):
        for k in range(TOP_K):
            row_copy(dest_ref, slot, r, k).wait()
        return c

    lax.fori_loop(0, tm, drain, 0, unroll=8)
    gates = gate_ref[...]
    y = gates[:, 0:1] * buf_ref[slot, 0]
    for k in range(1, TOP_K):
        y = y + gates[:, k:k + 1] * buf_ref[slot, k]
    out = _layer_norm(DEEPNORM_ALPHA * x_ref[...] + y, g_ref[...], beta_ref[...])
    if len(o_refs) == 1:
        o_refs[0][...] = out
    else:
        @pl.when(i < head_steps)
        def _():
            o_refs[0][...] = out

        @pl.when(i >= head_steps)
        def _():
            o_refs[1][...] = out


def _combine_ln(x, yg, dest, gates_t, g, beta, split=None):
    n = x.shape[0]
    tm = _row_tile(n, COMBINE_TOKEN_TILE)
    dest = _slot_rows_by_tile(dest, tm)
    n_steps = n // tm
    row = pl.BlockSpec((tm, D_MODEL), lambda i: (i, 0))
    vec = pl.BlockSpec((1, D_MODEL), lambda i: (0, 0))
    if split is None:
        head_steps, out_shape, out_specs = n_steps, jax.ShapeDtypeStruct((n, D_MODEL), F32), row
    else:
        assert split % tm == 0
        head_steps = split // tm
        out_shape = (jax.ShapeDtypeStruct((split, D_MODEL), F32), jax.ShapeDtypeStruct((n - split, D_MODEL), F32))
        out_specs = (pl.BlockSpec((tm, D_MODEL), lambda i: (jnp.minimum(i, head_steps - 1), 0)),
                     pl.BlockSpec((tm, D_MODEL), lambda i: (jnp.maximum(i - head_steps, 0), 0)))
    return pl.pallas_call(
        functools.partial(_combine_ln_kernel, head_steps=head_steps),
        out_shape=out_shape,
        grid=(n_steps,),
        in_specs=[pl.BlockSpec((TOP_K * tm,), lambda i: (i,), memory_space=pltpu.SMEM),
                  pl.BlockSpec((TOP_K * tm,), lambda i: (jnp.minimum(i + 1, n_steps - 1),),
                               memory_space=pltpu.SMEM),
                  row, pl.BlockSpec((tm, TOP_K), lambda i: (i, 0)), vec, vec,
                  pl.BlockSpec(memory_space=pl.ANY)],
        out_specs=out_specs,
        scratch_shapes=[pltpu.VMEM((2, TOP_K, tm, D_MODEL), F32), pltpu.SemaphoreType.DMA((2,))],
        compiler_params=_params("arbitrary"),
        name="moe_combine_ln",
    )(dest, dest, x, gates_t, g.reshape(1, D_MODEL), beta.reshape(1, D_MODEL), yg)


def _moe_ln(x, layer, w_router, b_router, w_up, b_up, w_down, b_down, g, beta, split=None):
    n = x.shape[0]
    tm = EXPERT_TILE_ROWS
    ids, gates, rank, counts = _router(x, w_router, b_router)
    counts = counts[:, 0]
    padded = (counts + tm - 1) // tm * tm
    pend = jnp.cumsum(padded)
    pstart = pend - padded
    expert_io = jnp.arange(N_EXPERTS, dtype=jnp.int32)[:, None, None]
    dest = jnp.sum(jnp.where(ids[None] == expert_io, pstart[:, None, None], 0), axis=0) + rank
    n_tiles = -(-(n * TOP_K) // tm) + N_EXPERTS
    tile_row0 = jnp.arange(n_tiles, dtype=jnp.int32) * tm
    tile_e = jnp.sum((pend[None, :] <= tile_row0[:, None]).astype(jnp.int32), axis=1)
    tile_e = jnp.minimum(tile_e, N_EXPERTS - 1)
    seg_end = jnp.sum(jnp.where(tile_e[:, None] == expert_io[:, 0, 0][None, :], (pstart + counts)[None, :], 0), axis=1)
    tile_rows = jnp.clip(seg_end - tile_row0, 0, tm).astype(jnp.int32)
    last_tile = jnp.where(padded > 0, pend // tm - 1, -1).astype(jnp.int32)
    xg = _dispatch(x, dest, last_tile, n_tiles * tm)
    yg = _experts(tile_e, tile_rows, xg, layer, w_up, b_up, w_down, b_down)
    return _combine_ln(x, yg, dest, gates.T, g, beta, split)


def kernel(x_prompt, x_sample, cache_k, cache_v, state_conv, state_h, page_table, ln_g, ln_b, w_qkv, w_o, w_in, b_in, conv_w, conv_b, w_gate_a, b_gate_a, w_gate_x, b_gate_x, lru_lambda, w_out, b_out, w_router, b_router, w_up, b_up, w_down, b_down):
    bp, tp, d = x_prompt.shape
    db, ts, _ = x_sample.shape
    n_p, n_s = bp * tp, db * ts
    xp = x_prompt.reshape(n_p, d)
    xs = x_sample.reshape(n_s, d)
    slopes = 2.0 ** (-8.0 * jnp.arange(1, N_HEADS + 1, dtype=F32) / N_HEADS)
    zeros_d = jnp.zeros((d,), F32)

    def moe_block(x, layer, split=None):
        return _moe_ln(x, layer, w_router[layer], b_router[layer], w_up, b_up, w_down, b_down,
                       ln_g[layer, 1], ln_b[layer, 1], split)

    n = n_p + n_s

    w_qkv_bf = w_qkv[0].astype(BF16)
    w_o_bf = w_o[0].astype(BF16)
    wkt_bf = w_qkv[0][:, d:2 * d].T.astype(BF16)
    wvt_bf = w_qkv[0][:, 2 * d:].T.astype(BF16)
    wqt_bf = w_qkv[0][:, :d].T.astype(BF16)
    qtp, kp, ktp, vtp = _qkv_prompt(xp, wqt_bf, w_qkv_bf[:, d:2 * d], wkt_bf, wvt_bf, bp, tp)
    qs, ks, vs = _qkv(xs, w_qkv_bf)
    op = _moba_prompt(qtp, kp, vtp, slopes, bp, tp)
    n_phys = cache_k.shape[1]
    kt_pool = jnp.transpose(cache_k[0], (0, 2, 3, 1)).reshape(n_phys, d, PAGE_SIZE)
    vt_pool = jnp.transpose(cache_v[0], (0, 2, 3, 1)).reshape(n_phys, d, PAGE_SIZE)
    os_ = _moba_sample(qs, ks, vs, kt_pool, vt_pool, page_table, slopes, db, ts)
    x = _linear_res_ln(op, w_o_bf, zeros_d, xp, ln_g[0, 0], ln_b[0, 0], out_rows=n)
    x = _linear_res_ln(os_, w_o_bf, zeros_d, xs, ln_g[0, 0], ln_b[0, 0], out_rows=n, row0=n_p, into=x)
    x = moe_block(x, 0)

    w_in_bf = w_in[0].astype(BF16)
    wa_bf = w_gate_a[0].astype(BF16)
    wx_bf = w_gate_x[0].astype(BF16)
    rec = (w_in_bf, b_in[0], conv_w[0], conv_b[0], wa_bf, b_gate_a[0].reshape(-1), wx_bf, b_gate_x[0].reshape(-1),
           lru_lambda[0], w_out[0].astype(BF16), b_out[0], ln_g[1, 0], ln_b[1, 0])
    conv0 = jnp.zeros((bp, CONV_W - 1, D_RNN), F32)
    h0 = jnp.zeros((bp, D_RNN), F32)
    x1, conv_p, h_p = _rglru(x, 0, conv0, h0, *rec, bp, tp)
    x1, conv_s, h_s = _rglru(x, n_p, state_conv[0], state_h[0], *rec, db, ts, into=x1)
    xp, xs = moe_block(x1, 1, split=n_p)

    def kv_rows(a_t):
        return jnp.transpose(a_t.reshape(1, bp, N_HEADS, HEAD_DIM, tp), (0, 1, 4, 2, 3))

    kv_s = (1, db, ts, N_HEADS, HEAD_DIM)
    return (xp.reshape(bp, tp, d), xs.reshape(db, ts, d),
            kv_rows(ktp), kv_rows(vtp), conv_p[None], h_p.reshape(1, bp, D_RNN),
            ks.reshape(kv_s), vs.reshape(kv_s), conv_s[None], h_s.reshape(1, db, D_RNN))
```
